```python
import math
import jax, jax.numpy as jnp
from jax import lax
import numpy as np

D_MODEL = 1024
BATCH = 8
SEQ = 2048
DEPTH = 2

DIFF_HEADS = 4
DIFF_DH = 64
DIFF_DV = 2 * DIFF_DH
DIFF_QK = DIFF_HEADS * 2 * DIFF_DH
DIFF_WIDTH = DIFF_HEADS * DIFF_DV
ROPE_THETA = 500000.0
ROPE_DIMS = DIFF_DH // 4
Q_BLOCK = 128
CONV_CH = D_MODEL - DIFF_WIDTH
CONV_WIDTH = 3
RET_HEADS = 4
RET_DK = D_MODEL // RET_HEADS
RET_DV = 2 * RET_DK
RET_QK = RET_HEADS * RET_DK
RET_VW = RET_HEADS * RET_DV
RET_CHUNK = 128
RET_ROPE_THETA = 10000.0
D_FF = ((8 * D_MODEL // 3 + 127) // 128) * 128

EVEN_IN = 2 * DIFF_QK + DIFF_WIDTH + 3 * CONV_CH
ODD_IN = 2 * RET_QK + 2 * RET_VW
N_EVEN = (DEPTH + 1) // 2
N_ODD = DEPTH // 2
DEEPNORM_ALPHA = (2.0 * DEPTH) ** 0.25
DEEPNORM_BETA = (8.0 * DEPTH) ** -0.25
LN_EPS = 1e-5
NEG_INF = -1e30

kernel_name = "hybrid_diffattn_shortconv_retention_macaron"


def layer_norm(x, g, b):
    xf = x.astype(jnp.float32)
    mu = jnp.mean(xf, -1, keepdims=True)
    var = jnp.mean(jnp.square(xf - mu), -1, keepdims=True)
    return ((xf - mu) * lax.rsqrt(var + LN_EPS)).astype(x.dtype) * g + b


def rms_norm(x, g):
    xf = x.astype(jnp.float32)
    y = xf * lax.rsqrt(jnp.mean(jnp.square(xf), -1, keepdims=True) + LN_EPS)
    return y.astype(x.dtype) * g


def rotary(x, n_rot, theta):
    s = x.shape[1]
    half = n_rot // 2
    inv = jnp.exp(-math.log(theta) * jnp.arange(half, dtype=jnp.float32) * (2.0 / n_rot))
    ang = jnp.arange(s, dtype=jnp.float32)[:, None] * inv[None, :]
    cos = jnp.cos(ang)[None, :, None, :].astype(x.dtype)
    sin = jnp.sin(ang)[None, :, None, :].astype(x.dtype)
    x1, x2, xp = x[..., :half], x[..., half:n_rot], x[..., n_rot:]
    return jnp.concatenate([x1 * cos - x2 * sin, x2 * cos + x1 * sin, xp], -1)


def swiglu_ffn(x, w_in, w_out):
    gate, up = jnp.split(x @ w_in, 2, axis=-1)
    return (jax.nn.silu(gate) * up) @ w_out


def diff_attention(q, k, v, lam):
    b, s, h2, d = q.shape
    h = h2 // 2
    nblk = s // Q_BLOCK
    scale = d ** -0.5
    qb = q.reshape(b, nblk, Q_BLOCK, h2, d).transpose(1, 0, 3, 2, 4)
    kt = k.transpose(0, 2, 1, 3)
    vt = v.transpose(0, 2, 1, 3)
    kpos = jnp.arange(s)
    starts = jnp.arange(nblk) * Q_BLOCK

    def one_block(args):
        qblk, start = args
        sc = jnp.einsum('bhqd,bhkd->bhqk', qblk, kt).astype(jnp.float32) * scale
        qpos = start + jnp.arange(Q_BLOCK)
        sc = jnp.where(kpos[None, :] <= qpos[:, None], sc, NEG_INF)
        p = jax.nn.softmax(sc, axis=-1).reshape(b, h, 2, Q_BLOCK, s)
        a = p[:, :, 0] - lam * p[:, :, 1]
        return jnp.einsum('bhqk,bhkv->bhqv', a.astype(vt.dtype), vt)

    out = lax.map(one_block, (qb, starts))
    return out.transpose(1, 0, 3, 2, 4).reshape(b, s, h, v.shape[-1])


def short_gated_conv(gate_b, gate_c, hx, w):
    u = gate_c * hx
    y = lax.conv_general_dilated(
        u, w[:, None, :].astype(u.dtype), window_strides=(1,),
        padding=[(CONV_WIDTH - 1, 0)], dimension_numbers=('NWC', 'WIO', 'NWC'),
        feature_group_count=u.shape[-1])
    return gate_b * y


def even_mixer(x, w_in, w_out, lam_vecs, norm_g, conv_w, layer_idx):
    b, s, _ = x.shape
    hcat = x @ w_in
    o1 = DIFF_QK
    o2 = 2 * DIFF_QK
    o3 = o2 + DIFF_WIDTH
    o4 = o3 + CONV_CH
    o5 = o4 + CONV_CH
    q, k, v, cb, cc, cx = jnp.split(hcat, [o1, o2, o3, o4, o5], axis=-1)
    q = rotary(q.reshape(b, s, 2 * DIFF_HEADS, DIFF_DH), ROPE_DIMS, ROPE_THETA)
    k = rotary(k.reshape(b, s, 2 * DIFF_HEADS, DIFF_DH), ROPE_DIMS, ROPE_THETA)
    v = v.reshape(b, s, DIFF_HEADS, DIFF_DV)
    lambda_init = 0.8 - 0.6 * math.exp(-0.3 * layer_idx)
    lv = lam_vecs.astype(jnp.float32)
    lam = jnp.exp(jnp.sum(lv[0] * lv[1])) - jnp.exp(jnp.sum(lv[2] * lv[3])) + lambda_init
    attn = diff_attention(q, k, v, lam)
    attn = (rms_norm(attn, norm_g) * (1.0 - lambda_init)).reshape(b, s, DIFF_WIDTH)
    conv = short_gated_conv(cb, cc, cx, conv_w)
    return jnp.concatenate([attn, conv], axis=-1) @ w_out


def retention(q, k, v):
    b, s, h, dk = q.shape
    dv = v.shape[-1]
    n = s // RET_CHUNK
    log_g = jnp.log1p(-jnp.exp2(-5.0 - jnp.arange(h, dtype=jnp.float32)))
    idx = jnp.arange(RET_CHUNK, dtype=jnp.float32)
    rel = idx[:, None] - idx[None, :]
    intra_decay = jnp.where(rel >= 0, jnp.exp(log_g[:, None, None] * jnp.maximum(rel, 0.0)), 0.0)
    q_decay = jnp.exp(log_g[:, None] * (idx + 1.0))[None, :, :, None]
    k_decay = jnp.exp(log_g[:, None] * (RET_CHUNK - 1.0 - idx))[None, :, :, None]
    chunk_decay = jnp.exp(log_g * RET_CHUNK)[None, :, None, None]

    def to_chunks(t):
        return t.reshape(b, n, RET_CHUNK, h, t.shape[-1]).transpose(1, 0, 3, 2, 4)

    def step(state, inp):
        qc, kc, vc = inp
        sc = jnp.einsum('bhqd,bhkd->bhqk', qc, kc) * intra_decay[None]
        intra = jnp.einsum('bhqk,bhkv->bhqv', sc, vc)
        cross = jnp.einsum('bhqd,bhdv->bhqv', qc, state) * q_decay
        state = state * chunk_decay + jnp.einsum('bhkd,bhkv->bhdv', kc * k_decay, vc)
        return state, intra + cross

    state0 = jnp.zeros((b, h, dk, dv), jnp.float32)
    _, ys = lax.scan(step, state0, (to_chunks(q), to_chunks(k), to_chunks(v)))
    return ys.transpose(1, 0, 3, 2, 4).reshape(b, s, h, dv).astype(q.dtype)


def odd_mixer(x, w_in, w_out, norm_g):
    b, s, _ = x.shape
    q, k, v, g = jnp.split(x @ w_in, [RET_QK, 2 * RET_QK, 2 * RET_QK + RET_VW], axis=-1)
    q = rotary(q.reshape(b, s, RET_HEADS, RET_DK), RET_DK, RET_ROPE_THETA)
    k = rotary(k.reshape(b, s, RET_HEADS, RET_DK), RET_DK, RET_ROPE_THETA) * (RET_DK ** -0.5)
    v = v.reshape(b, s, RET_HEADS, RET_DV)
    y = retention(q, k, v)
    yf = y.astype(jnp.float32)
    mu = jnp.mean(yf, -1, keepdims=True)
    var = jnp.mean(jnp.square(yf - mu), -1, keepdims=True)
    y = ((yf - mu) * lax.rsqrt(var + LN_EPS)).astype(x.dtype).reshape(b, s, RET_VW) * norm_g
    return (jax.nn.silu(g) * y) @ w_out


def setup_inputs(seed: int = 0) -> dict:
    key = jax.random.key(seed)
    ks = jax.random.split(key, 14)
    beta = DEEPNORM_BETA
    x = jax.random.normal(ks[0], (BATCH, SEQ, D_MODEL), jnp.float32)
    ln_g = 1.0 + 0.02 * jax.random.normal(ks[1], (DEPTH, 3, D_MODEL), jnp.float32)
    ln_b = 0.02 * jax.random.normal(ks[2], (DEPTH, 3, D_MODEL), jnp.float32)
    ffn_w_in = jax.random.normal(ks[3], (DEPTH, 2, D_MODEL, 2 * D_FF), jnp.float32) * (D_MODEL ** -0.5 * beta)
    ffn_w_out = jax.random.normal(ks[4], (DEPTH, 2, D_FF, D_MODEL), jnp.float32) * (D_FF ** -0.5 * beta)
    even_scale = jnp.concatenate([
        jnp.ones((2 * DIFF_QK,), jnp.float32),
        jnp.full((DIFF_WIDTH,), beta, jnp.float32),
        jnp.ones((2 * CONV_CH,), jnp.float32),
        jnp.full((CONV_CH,), beta, jnp.float32)])
    even_w_in = jax.random.normal(ks[5], (N_EVEN, D_MODEL, EVEN_IN), jnp.float32) * (D_MODEL ** -0.5) * even_scale
    even_w_out = jax.random.normal(ks[6], (N_EVEN, DIFF_WIDTH + CONV_CH, D_MODEL), jnp.float32) * ((DIFF_WIDTH + CONV_CH) ** -0.5 * beta)
    diff_lambda = 0.1 * jax.random.normal(ks[7], (N_EVEN, 4, DIFF_DH), jnp.float32)
    diff_norm_g = 1.0 + 0.02 * jax.random.normal(ks[8], (N_EVEN, DIFF_DV), jnp.float32)
    conv_w = jax.random.normal(ks[9], (N_EVEN, CONV_WIDTH, CONV_CH), jnp.float32) * (CONV_WIDTH ** -0.5)
    odd_scale = jnp.concatenate([
        jnp.ones((2 * RET_QK,), jnp.float32),
        jnp.full((RET_VW,), beta, jnp.float32),
        jnp.ones((RET_VW,), jnp.float32)])
    odd_w_in = jax.random.normal(ks[10], (N_ODD, D_MODEL, ODD_IN), jnp.float32) * (D_MODEL ** -0.5) * odd_scale
    odd_w_out = jax.random.normal(ks[11], (N_ODD, RET_VW, D_MODEL), jnp.float32) * (RET_VW ** -0.5 * beta)
    ret_norm_g = 1.0 + 0.02 * jax.random.normal(ks[12], (N_ODD, RET_VW), jnp.float32)
    return {"x": x, "ln_g": ln_g, "ln_b": ln_b, "ffn_w_in": ffn_w_in, "ffn_w_out": ffn_w_out,
            "even_w_in": even_w_in, "even_w_out": even_w_out, "diff_lambda": diff_lambda,
            "diff_norm_g": diff_norm_g, "conv_w": conv_w, "odd_w_in": odd_w_in,
            "odd_w_out": odd_w_out, "ret_norm_g": ret_norm_g}


def reference(x, ln_g, ln_b, ffn_w_in, ffn_w_out, even_w_in, even_w_out, diff_lambda,
              diff_norm_g, conv_w, odd_w_in, odd_w_out, ret_norm_g):
    a = DEEPNORM_ALPHA
    for i in range(DEPTH):
        j = i // 2
        x = layer_norm(a * x + 0.5 * swiglu_ffn(x, ffn_w_in[i, 0], ffn_w_out[i, 0]), ln_g[i, 0], ln_b[i, 0])
        if i % 2 == 0:
            mix = even_mixer(x, even_w_in[j], even_w_out[j], diff_lambda[j], diff_norm_g[j], conv_w[j], i)
        else:
            mix = odd_mixer(x, odd_w_in[j], odd_w_out[j], ret_norm_g[j])
        x = layer_norm(a * x + mix, ln_g[i, 1], ln_b[i, 1])
        x = layer_norm(a * x + 0.5 * swiglu_ffn(x, ffn_w_in[i, 1], ffn_w_out[i, 1]), ln_g[i, 2], ln_b[i, 2])
    return x
```

```python
import functools
import math

import jax
import jax.numpy as jnp
from jax import lax
from jax.experimental import pallas as pl
from jax.experimental.pallas import tpu as pltpu

F32 = jnp.float32
BF16 = jnp.bfloat16

D_MODEL = 1024
DEPTH = 2
D_FF = 2816
DIFF_HEADS = 4
DIFF_DH = 64
DIFF_DV = 128
DIFF_QK = 512
DIFF_WIDTH = 512
ROPE_THETA = 500000.0
ROPE_DIMS = 16
CONV_CH = 512
EVEN_IN = 3072
RET_HEADS = 4
RET_DK = 256
RET_DV = 512
RET_QK = 1024
RET_VW = 2048
RET_CHUNK = 128
RET_ROPE_THETA = 10000.0
ODD_IN = 6144
ALPHA = (2.0 * DEPTH) ** 0.25
LN_EPS = 1e-5
NEG_INF = -1e30

LANES = 128
VMEM_LIMIT = 56 * 1024 * 1024

TM = 1024
FF_CHUNK = 256
ATT_TQ = 256
ATT_TK = 256


def _layer_norm(z, g, b):
    mu = jnp.mean(z, -1, keepdims=True)
    d = z - mu
    var = jnp.mean(d * d, -1, keepdims=True)
    return d * lax.rsqrt(var + LN_EPS) * g + b


def _params(n_axes, vmem=VMEM_LIMIT):
    return pltpu.CompilerParams(dimension_semantics=("arbitrary",) * n_axes,
                                vmem_limit_bytes=vmem)


def _resident(shape):
    nd = len(shape)
    return pl.BlockSpec(shape, lambda *_: (0,) * nd, pipeline_mode=pl.Buffered(1))


def _ffn_kernel(x_ref, win_ref, wout_ref, g_ref, b_ref, o_ref, act_ref):
    x = x_ref[...]
    xb = x.astype(BF16)
    for c in range(D_FF // FF_CHUNK):
        lo = c * FF_CHUNK
        gate = jnp.dot(xb, win_ref[:, lo:lo + FF_CHUNK], preferred_element_type=F32)
        up = jnp.dot(xb, win_ref[:, D_FF + lo:D_FF + lo + FF_CHUNK], preferred_element_type=F32)
        act_ref[:, lo:lo + FF_CHUNK] = (gate * jax.nn.sigmoid(gate) * up).astype(BF16)
    y = jnp.dot(act_ref[...], wout_ref[...], preferred_element_type=F32)
    o_ref[...] = _layer_norm(ALPHA * x + 0.5 * y, g_ref[...], b_ref[...])


def _ffn_ln(x, w_in, w_out, g, b):
    t = x.shape[0]
    return pl.pallas_call(
        _ffn_kernel,
        grid=(t // TM,),
        in_specs=[pl.BlockSpec((TM, D_MODEL), lambda i: (i, 0)),
                  _resident((D_MODEL, 2 * D_FF)),
                  _resident((D_FF, D_MODEL)),
                  _resident((1, D_MODEL)),
                  _resident((1, D_MODEL))],
        out_specs=pl.BlockSpec((TM, D_MODEL), lambda i: (i, 0)),
        out_shape=jax.ShapeDtypeStruct((t, D_MODEL), F32),
        scratch_shapes=[pltpu.VMEM((TM, D_FF), BF16)],
        compiler_params=_params(1),
        name="ffn_ln",
    )(x, w_in, w_out, g, b)


def _even_proj_kernel(x_ref, w_ref, c_ref, sa_ref, sb_ref, o_ref):
    xb = x_ref[...].astype(BF16)
    c = c_ref[...]
    sa = sa_ref[...]
    sb = sb_ref[...]
    for n in range(EVEN_IN // 512):
        r = jnp.dot(xb, w_ref[:, n * 512:(n + 1) * 512], preferred_element_type=F32)
        if n < 2:
            scale = DIFF_DH ** -0.5 if n == 0 else 1.0
            for j in range(512 // LANES):
                blk = r[:, j * LANES:(j + 1) * LANES]
                rot = (blk * c + pltpu.roll(blk, LANES - 8, 1) * sa
                       + pltpu.roll(blk, 8, 1) * sb)
                if n == 0:
                    rot = rot * scale
                o_ref[:, n * 512 + j * LANES:n * 512 + (j + 1) * LANES] = rot.astype(BF16)
        else:
            o_ref[:, n * 512:(n + 1) * 512] = r.astype(BF16)


def _even_proj(x, w, c, sa, sb, seq):
    t = x.shape[0]
    per_seq = seq // TM
    tbl = pl.BlockSpec((TM, LANES), lambda i: (i % per_seq, 0))
    return pl.pallas_call(
        _even_proj_kernel,
        grid=(t // TM,),
        in_specs=[pl.BlockSpec((TM, D_MODEL), lambda i: (i, 0)),
                  _resident((D_MODEL, EVEN_IN)), tbl, tbl, tbl],
        out_specs=pl.BlockSpec((TM, EVEN_IN), lambda i: (i, 0)),
        out_shape=jax.ShapeDtypeStruct((t, EVEN_IN), BF16),
        compiler_params=_params(1),
        name="even_proj",
    )(x, w, c, sa, sb)


def _attn_kernel(q_ref, k_ref, v_ref, lv_ref, g_ref, o_ref, s_ref, *, lambda_init):
    i = pl.program_id(2)
    q = q_ref[...]
    lane = lax.broadcasted_iota(jnp.int32, q.shape, 1)
    zero = jnp.zeros_like(q)
    qs = (jnp.where(lane < DIFF_DH, q, zero), jnp.where(lane >= DIFF_DH, q, zero))

    def scores(kb):
        kblk = k_ref[pl.ds(pl.multiple_of(kb * ATT_TK, ATT_TK), ATT_TK), :]
        return [lax.dot_general(qc, kblk, (((1,), (1,)), ((), ())),
                                preferred_element_type=F32) for qc in qs]

    def fold(m, s):
        return jnp.maximum(m, jnp.maximum(s[:, :LANES], s[:, LANES:]))

    def pass1(kb, m):
        s = scores(kb)
        s_ref[0, kb] = s[0]
        s_ref[1, kb] = s[1]
        return fold(m[0], s[0]), fold(m[1], s[1])

    m_init = jnp.full((ATT_TQ, LANES), NEG_INF, F32)
    m = lax.fori_loop(0, i, pass1, (m_init, m_init))
    row = lax.broadcasted_iota(jnp.int32, (ATT_TQ, ATT_TK), 0)
    col = lax.broadcasted_iota(jnp.int32, (ATT_TQ, ATT_TK), 1)
    s = [jnp.where(col <= row, sc, NEG_INF) for sc in scores(i)]
    s_ref[0, i] = s[0]
    s_ref[1, i] = s[1]
    m = [jnp.max(fold(m[c], s[c]), -1, keepdims=True) for c in range(2)]

    def pass2(kb, l):
        out = []
        for c in range(2):
            p = jnp.exp(s_ref[c, kb] - m[c])
            s_ref[c, kb] = p
            out.append(l[c] + p[:, :LANES] + p[:, LANES:])
        return tuple(out)

    l_init = jnp.zeros((ATT_TQ, LANES), F32)
    l = lax.fori_loop(0, i + 1, pass2, (l_init, l_init))
    lv = lv_ref[...]
    lam = (jnp.exp(jnp.sum(lv[0:1] * lv[1:2], -1, keepdims=True))
           - jnp.exp(jnp.sum(lv[2:3] * lv[3:4], -1, keepdims=True)) + lambda_init)
    r1 = 1.0 / jnp.sum(l[0], -1, keepdims=True)
    r2 = lam / jnp.sum(l[1], -1, keepdims=True)

    def pass3(kb, acc):
        a = (s_ref[0, kb] * r1 - s_ref[1, kb] * r2).astype(BF16)
        vblk = v_ref[pl.ds(pl.multiple_of(kb * ATT_TK, ATT_TK), ATT_TK), :]
        return acc + jnp.dot(a, vblk, preferred_element_type=F32)

    acc = lax.fori_loop(0, i + 1, pass3, jnp.zeros((ATT_TQ, DIFF_DV), F32))
    y = acc * lax.rsqrt(jnp.mean(acc * acc, -1, keepdims=True) + LN_EPS)
    o_ref[...] = (y * g_ref[...] * (1.0 - lambda_init)).astype(BF16)


def _diff_attention(hcat, lam_vecs, norm_g, batch, seq, lambda_init):
    nq = seq // ATT_TQ
    return pl.pallas_call(
        functools.partial(_attn_kernel, lambda_init=lambda_init),
        grid=(batch, DIFF_HEADS, nq),
        in_specs=[pl.BlockSpec((ATT_TQ, LANES), lambda b, h, i: (b * nq + i, h)),
                  pl.BlockSpec((seq, LANES), lambda b, h, i: (b, DIFF_QK // LANES + h)),
                  pl.BlockSpec((seq, LANES), lambda b, h, i: (b, 2 * DIFF_QK // LANES + h)),
                  _resident((4, DIFF_DH)),
                  _resident((1, DIFF_DV))],
        out_specs=pl.BlockSpec((ATT_TQ, DIFF_DV), lambda b, h, i: (b * nq + i, h)),
        out_shape=jax.ShapeDtypeStruct((batch * seq, DIFF_WIDTH), BF16),
        scratch_shapes=[pltpu.VMEM((2, seq // ATT_TK, ATT_TQ, ATT_TK), F32)],
        compiler_params=_params(3),
        name="diff_attn",
    )(hcat, hcat, hcat, lam_vecs, norm_g)


HALO = 16


def _even_out_kernel(x_ref, attn_ref, cb_ref, cc_ref, cx_ref, hc_ref, hx_ref, cw_ref,
                     wout_ref, g_ref, b_ref, o_ref, *, per_seq):
    i = pl.program_id(0)
    u = cc_ref[...].astype(F32) * cx_ref[...].astype(F32)
    halo = hc_ref[...].astype(F32) * hx_ref[...].astype(F32)
    halo = jnp.where(i % per_seq == 0, jnp.zeros_like(halo), halo)
    row = lax.broadcasted_iota(jnp.int32, u.shape, 0)
    u1 = pltpu.roll(u, 1, 0)
    u1 = jnp.where(row == 0, halo[HALO - 1:HALO, :], u1)
    u2 = pltpu.roll(u, 2, 0)
    u2 = jnp.where(row == 0, halo[HALO - 2:HALO - 1, :], u2)
    u2 = jnp.where(row == 1, halo[HALO - 1:HALO, :], u2)
    cw = cw_ref[...]
    conv = cb_ref[...].astype(F32) * (cw[0:1] * u2 + cw[1:2] * u1 + cw[2:3] * u)
    y = jnp.dot(attn_ref[...], wout_ref[:DIFF_WIDTH, :], preferred_element_type=F32)
    y = y + jnp.dot(conv.astype(BF16), wout_ref[DIFF_WIDTH:, :], preferred_element_type=F32)
    o_ref[...] = _layer_norm(ALPHA * x_ref[...] + y, g_ref[...], b_ref[...])


def _even_out(x, attn, hcat, conv_w, w_out, g, b, seq):
    t = x.shape[0]
    per_seq = seq // TM
    col0 = (2 * DIFF_QK + DIFF_WIDTH) // CONV_CH
    halo_blocks = TM // HALO

    def halo_spec(col):
        return pl.BlockSpec((HALO, CONV_CH),
                            lambda i: (jnp.maximum(i * halo_blocks - 1, 0), col))

    return pl.pallas_call(
        functools.partial(_even_out_kernel, per_seq=per_seq),
        grid=(t // TM,),
        in_specs=[pl.BlockSpec((TM, D_MODEL), lambda i: (i, 0)),
                  pl.BlockSpec((TM, DIFF_WIDTH), lambda i: (i, 0)),
                  pl.BlockSpec((TM, CONV_CH), lambda i: (i, col0)),
                  pl.BlockSpec((TM, CONV_CH), lambda i: (i, col0 + 1)),
                  pl.BlockSpec((TM, CONV_CH), lambda i: (i, col0 + 2)),
                  halo_spec(col0 + 1), halo_spec(col0 + 2),
                  _resident((3, CONV_CH)),
                  _resident((DIFF_WIDTH + CONV_CH, D_MODEL)),
                  _resident((1, D_MODEL)), _resident((1, D_MODEL))],
        out_specs=pl.BlockSpec((TM, D_MODEL), lambda i: (i, 0)),
        out_shape=jax.ShapeDtypeStruct((t, D_MODEL), F32),
        compiler_params=_params(1),
        name="even_out",
    )(x, attn, hcat, hcat, hcat, hcat, hcat, conv_w, w_out, g, b)


def _odd_proj_kernel(x_ref, w_ref, cos_ref, sin_ref, o_ref):
    xb = x_ref[...].astype(BF16)
    cos = cos_ref[...]
    sin = sin_ref[...]
    half = RET_DK // 2
    for n in range(ODD_IN // 512):
        r = jnp.dot(xb, w_ref[:, n * 512:(n + 1) * 512], preferred_element_type=F32)
        if n < 2 * RET_QK // 512:
            for hh in range(512 // RET_DK):
                x1 = r[:, hh * RET_DK:hh * RET_DK + half]
                x2 = r[:, hh * RET_DK + half:(hh + 1) * RET_DK]
                o1 = x1 * cos - x2 * sin
                o2 = x2 * cos + x1 * sin
                if n >= RET_QK // 512:
                    o1 = o1 * RET_DK ** -0.5
                    o2 = o2 * RET_DK ** -0.5
                base = n * 512 + hh * RET_DK
                o_ref[:, base:base + half] = o1.astype(BF16)
                o_ref[:, base + half:base + RET_DK] = o2.astype(BF16)
        else:
            o_ref[:, n * 512:(n + 1) * 512] = r.astype(BF16)


def _odd_proj(x, w, cos, sin, seq):
    t = x.shape[0]
    per_seq = seq // TM
    tbl = pl.BlockSpec((TM, LANES), lambda i: (i % per_seq, 0))
    return pl.pallas_call(
        _odd_proj_kernel,
        grid=(t // TM,),
        in_specs=[pl.BlockSpec((TM, D_MODEL), lambda i: (i, 0)),
                  _resident((D_MODEL, ODD_IN)), tbl, tbl],
        out_specs=pl.BlockSpec((TM, ODD_IN), lambda i: (i, 0)),
        out_shape=jax.ShapeDtypeStruct((t, ODD_IN), BF16),
        compiler_params=_params(1),
        name="odd_proj",
    )(x, w, cos, sin)


def _retention_kernel(cd_ref, q_ref, k_ref, v_ref, gate_ref, dmat_ref, qd_ref, kd_ref, ng_ref,
                      o_ref, *, seq):
    h = pl.program_id(1)
    cd = cd_ref[h]
    dmat = dmat_ref[...]
    qd = jnp.concatenate([qd_ref[...]] * (RET_DV // LANES), axis=1)
    kd = jnp.concatenate([kd_ref[...]] * (RET_DK // LANES), axis=1)
    ng = ng_ref[...]
    state = jnp.zeros((RET_DK, RET_DV), F32)
    for c in range(seq // RET_CHUNK):
        rows = pl.ds(c * RET_CHUNK, RET_CHUNK)
        qc = q_ref[rows, :]
        kc = k_ref[rows, :]
        vc = v_ref[rows, :]
        sc = lax.dot_general(qc, kc, (((1,), (1,)), ((), ())), preferred_element_type=F32) * dmat
        intra = jnp.dot(sc.astype(BF16), vc, preferred_element_type=F32)
        cross = jnp.dot(qc, state.astype(BF16), preferred_element_type=F32) * qd
        kdec = (kc.astype(F32) * kd).astype(BF16)
        state = state * cd + lax.dot_general(kdec, vc, (((0,), (0,)), ((), ())),
                                             preferred_element_type=F32)
        y = intra + cross
        mu = jnp.mean(y, -1, keepdims=True)
        d = y - mu
        var = jnp.mean(d * d, -1, keepdims=True)
        yn = d * lax.rsqrt(var + LN_EPS) * ng
        gt = gate_ref[rows, :].astype(F32)
        o_ref[rows, :] = (gt * jax.nn.sigmoid(gt) * yn).astype(BF16)


def _retention(hcat, tables, norm_g, batch, seq):
    cd, dmat, qd, kd = tables
    tbl = pl.BlockSpec((None, RET_CHUNK, RET_CHUNK), lambda b, h: (h, 0, 0))
    qk_blocks = RET_QK // RET_DK
    v0 = 2 * RET_QK // RET_DV
    g0 = v0 + RET_VW // RET_DV
    return pl.pallas_call(
        functools.partial(_retention_kernel, seq=seq),
        grid=(batch, RET_HEADS),
        in_specs=[pl.BlockSpec(memory_space=pltpu.SMEM),
                  pl.BlockSpec((seq, RET_DK), lambda b, h: (b, h)),
                  pl.BlockSpec((seq, RET_DK), lambda b, h: (b, qk_blocks + h)),
                  pl.BlockSpec((seq, RET_DV), lambda b, h: (b, v0 + h)),
                  pl.BlockSpec((seq, RET_DV), lambda b, h: (b, g0 + h)),
                  tbl, tbl, tbl,
                  pl.BlockSpec((1, RET_DV), lambda b, h: (0, h))],
        out_specs=pl.BlockSpec((seq, RET_DV), lambda b, h: (b, h)),
        out_shape=jax.ShapeDtypeStruct((batch * seq, RET_VW), BF16),
        compiler_params=_params(2),
        name="retention",
    )(cd, hcat, hcat, hcat, hcat, dmat, qd, kd, norm_g)


def _odd_out_kernel(x_ref, y_ref, wout_ref, g_ref, b_ref, o_ref):
    y = jnp.dot(y_ref[...], wout_ref[...], preferred_element_type=F32)
    o_ref[...] = _layer_norm(ALPHA * x_ref[...] + y, g_ref[...], b_ref[...])


def _odd_out(x, y, w_out, g, b):
    t = x.shape[0]
    return pl.pallas_call(
        _odd_out_kernel,
        grid=(t // TM,),
        in_specs=[pl.BlockSpec((TM, D_MODEL), lambda i: (i, 0)),
                  pl.BlockSpec((TM, RET_VW), lambda i: (i, 0)),
                  _resident((RET_VW, D_MODEL)),
                  _resident((1, D_MODEL)), _resident((1, D_MODEL))],
        out_specs=pl.BlockSpec((TM, D_MODEL), lambda i: (i, 0)),
        out_shape=jax.ShapeDtypeStruct((t, D_MODEL), F32),
        compiler_params=_params(1),
        name="odd_out",
    )(x, y, w_out, g, b)


def _even_rope_tables(seq):
    half = ROPE_DIMS // 2
    inv = jnp.exp(-math.log(ROPE_THETA) * jnp.arange(half, dtype=F32) * (2.0 / ROPE_DIMS))
    ang = jnp.arange(seq, dtype=F32)[:, None] * inv[None, :]
    cos, sin = jnp.cos(ang), jnp.sin(ang)
    pad = jnp.zeros((seq, DIFF_DH - ROPE_DIMS), F32)
    zeros = jnp.zeros((seq, half), F32)
    c = jnp.concatenate([cos, cos, pad + 1.0], -1)
    sa = jnp.concatenate([-sin, zeros, pad], -1)
    sb = jnp.concatenate([zeros, sin, pad], -1)
    reps = LANES // DIFF_DH
    return tuple(jnp.tile(t, (1, reps)) for t in (c, sa, sb))


def _odd_rope_tables(seq):
    half = RET_DK // 2
    inv = jnp.exp(-math.log(RET_ROPE_THETA) * jnp.arange(half, dtype=F32) * (2.0 / RET_DK))
    ang = jnp.arange(seq, dtype=F32)[:, None] * inv[None, :]
    return jnp.cos(ang), jnp.sin(ang)


def _retention_tables():
    log_g = jnp.log1p(-jnp.exp2(-5.0 - jnp.arange(RET_HEADS, dtype=F32)))
    idx = jnp.arange(RET_CHUNK, dtype=F32)
    rel = idx[:, None] - idx[None, :]
    dmat = jnp.where(rel >= 0, jnp.exp(log_g[:, None, None] * jnp.maximum(rel, 0.0)), 0.0)
    q_decay = jnp.exp(log_g[:, None] * (idx + 1.0))
    k_decay = jnp.exp(log_g[:, None] * (RET_CHUNK - 1.0 - idx))
    cd = jnp.exp(log_g * RET_CHUNK)
    rep = lambda t: jnp.broadcast_to(t[:, :, None], (RET_HEADS, RET_CHUNK, LANES))
    return cd, dmat, rep(q_decay), rep(k_decay)


def kernel(x, ln_g, ln_b, ffn_w_in, ffn_w_out, even_w_in, even_w_out, diff_lambda,
           diff_norm_g, conv_w, odd_w_in, odd_w_out, ret_norm_g):
    batch, seq, _ = x.shape
    assert seq % TM == 0 and seq % ATT_TQ == 0 and ATT_TQ == ATT_TK
    h = x.reshape(batch * seq, D_MODEL)
    ffn_w_in = ffn_w_in.astype(BF16)
    ffn_w_out = ffn_w_out.astype(BF16)
    row = lambda v: v.reshape(1, -1)
    for i in range(DEPTH):
        j = i // 2
        h = _ffn_ln(h, ffn_w_in[i, 0], ffn_w_out[i, 0], row(ln_g[i, 0]), row(ln_b[i, 0]))
        if i % 2 == 0:
            lambda_init = 0.8 - 0.6 * math.exp(-0.3 * i)
            hcat = _even_proj(h, even_w_in[j].astype(BF16), *_even_rope_tables(seq), seq)
            attn = _diff_attention(hcat, diff_lambda[j], row(diff_norm_g[j]), batch, seq,
                                   lambda_init)
            h = _even_out(h, attn, hcat, conv_w[j], even_w_out[j].astype(BF16),
                          row(ln_g[i, 1]), row(ln_b[i, 1]), seq)
        else:
            hcat = _odd_proj(h, odd_w_in[j].astype(BF16), *_odd_rope_tables(seq), seq)
            y = _retention(hcat, _retention_tables(), row(ret_norm_g[j]), batch, seq)
            h = _odd_out(h, y, odd_w_out[j].astype(BF16), row(ln_g[i, 1]), row(ln_b[i, 1]))
        h = _ffn_ln(h, ffn_w_in[i, 1], ffn_w_out[i, 1], row(ln_g[i, 2]), row(ln_b[i, 2]))
    return h.reshape(batch, seq, D_MODEL)
```

```python
import functools
import math

import jax
import jax.numpy as jnp
from jax import lax
from jax.experimental import pallas as pl
from jax.experimental.pallas import tpu as pltpu

F32 = jnp.float32
BF16 = jnp.bfloat16

D_MODEL = 1024
DEPTH = 2
D_FF = 2816
DIFF_HEADS = 4
DIFF_DH = 64
DIFF_DV = 128
DIFF_QK = 512
DIFF_WIDTH = 512
ROPE_THETA = 500000.0
ROPE_DIMS = 16
CONV_CH = 512
EVEN_IN = 3072
RET_HEADS = 4
RET_DK = 256
RET_DV = 512
RET_QK = 1024
RET_VW = 2048
RET_CHUNK = 128
RET_ROPE_THETA = 10000.0
ODD_IN = 6144
ALPHA = (2.0 * DEPTH) ** 0.25
LN_EPS = 1e-5
NEG_INF = -1e30

LANES = 128
VMEM_LIMIT = 56 * 1024 * 1024

TM = 1024
FF_CHUNK = 256
ATT_TQ = 256


def _layer_norm(z, g, b):
    mu = jnp.mean(z, -1, keepdims=True)
    d = z - mu
    var = jnp.mean(d * d, -1, keepdims=True)
    return d * lax.rsqrt(var + LN_EPS) * g + b


def _params(n_axes, vmem=VMEM_LIMIT):
    return pltpu.CompilerParams(dimension_semantics=("arbitrary",) * n_axes,
                                vmem_limit_bytes=vmem)


def _resident(shape):
    nd = len(shape)
    return pl.BlockSpec(shape, lambda *_: (0,) * nd, pipeline_mode=pl.Buffered(1))


def _ffn_kernel(x_ref, win_ref, wout_ref, g_ref, b_ref, o_ref, act_ref):
    x = x_ref[...]
    xb = x.astype(BF16)
    for c in range(D_FF // FF_CHUNK):
        lo = c * FF_CHUNK
        gate = jnp.dot(xb, win_ref[:, lo:lo + FF_CHUNK], preferred_element_type=F32)
        up = jnp.dot(xb, win_ref[:, D_FF + lo:D_FF + lo + FF_CHUNK], preferred_element_type=F32)
        act_ref[:, lo:lo + FF_CHUNK] = (gate * jax.nn.sigmoid(gate) * up).astype(BF16)
    y = jnp.dot(act_ref[...], wout_ref[...], preferred_element_type=F32)
    o_ref[...] = _layer_norm(ALPHA * x + 0.5 * y, g_ref[...], b_ref[...])


def _ffn_ln(x, w_in, w_out, layer, which, g, b):
    t = x.shape[0]
    pick = lambda i: (layer, which, 0, 0)
    return pl.pallas_call(
        _ffn_kernel,
        grid=(t // TM,),
        in_specs=[pl.BlockSpec((TM, D_MODEL), lambda i: (i, 0)),
                  pl.BlockSpec((None, None, D_MODEL, 2 * D_FF), pick,
                               pipeline_mode=pl.Buffered(1)),
                  pl.BlockSpec((None, None, D_FF, D_MODEL), pick,
                               pipeline_mode=pl.Buffered(1)),
                  _resident((1, D_MODEL)),
                  _resident((1, D_MODEL))],
        out_specs=pl.BlockSpec((TM, D_MODEL), lambda i: (i, 0)),
        out_shape=jax.ShapeDtypeStruct((t, D_MODEL), F32),
        scratch_shapes=[pltpu.VMEM((TM, D_FF), BF16)],
        compiler_params=_params(1),
        name="ffn_ln",
    )(x, w_in, w_out, g, b)


def _even_proj_kernel(x_ref, w_ref, c_ref, sa_ref, sb_ref, o_ref):
    xb = x_ref[...].astype(BF16)
    c = c_ref[...]
    sa = sa_ref[...]
    sb = sb_ref[...]
    for n in range(EVEN_IN // 512):
        r = jnp.dot(xb, w_ref[:, n * 512:(n + 1) * 512], preferred_element_type=F32)
        if n < 2:
            scale = DIFF_DH ** -0.5 if n == 0 else 1.0
            for j in range(512 // LANES):
                blk = r[:, j * LANES:(j + 1) * LANES]
                rot = (blk * c + pltpu.roll(blk, LANES - 8, 1) * sa
                       + pltpu.roll(blk, 8, 1) * sb)
                if n == 0:
                    rot = rot * scale
                o_ref[:, n * 512 + j * LANES:n * 512 + (j + 1) * LANES] = rot.astype(BF16)
        else:
            o_ref[:, n * 512:(n + 1) * 512] = r.astype(BF16)


def _even_proj(x, w, c, sa, sb, seq):
    t = x.shape[0]
    per_seq = seq // TM
    tbl = pl.BlockSpec((TM, LANES), lambda i: (i % per_seq, 0))
    return pl.pallas_call(
        _even_proj_kernel,
        grid=(t // TM,),
        in_specs=[pl.BlockSpec((TM, D_MODEL), lambda i: (i, 0)),
                  _resident((D_MODEL, EVEN_IN)), tbl, tbl, tbl],
        out_specs=pl.BlockSpec((TM, EVEN_IN), lambda i: (i, 0)),
        out_shape=jax.ShapeDtypeStruct((t, EVEN_IN), BF16),
        compiler_params=_params(1),
        name="even_proj",
    )(x, w, c, sa, sb)


def _attn_kernel(q_ref, k_ref, v_ref, lv_ref, g_ref, o_ref, *, lambda_init, seq):
    lv = lv_ref[...]
    lam = (jnp.exp(jnp.sum(lv[0:1] * lv[1:2], -1, keepdims=True))
           - jnp.exp(jnp.sum(lv[2:3] * lv[3:4], -1, keepdims=True)) + lambda_init)
    lane = lax.broadcasted_iota(jnp.int32, (ATT_TQ, LANES), 1)
    row = lax.broadcasted_iota(jnp.int32, (ATT_TQ, ATT_TQ), 0)
    col = lax.broadcasted_iota(jnp.int32, (ATT_TQ, ATT_TQ), 1)
    causal = col <= row
    gain = g_ref[...]
    nt = (((1,), (1,)), ((), ()))
    for i in range(seq // ATT_TQ):
        lo = i * ATT_TQ
        q = q_ref[lo:lo + ATT_TQ, :]
        zero = jnp.zeros_like(q)
        qs = (jnp.where(lane < DIFF_DH, q, zero), jnp.where(lane >= DIFF_DH, q, zero))
        k_diag = k_ref[lo:lo + ATT_TQ, :]
        s_diag = [jnp.where(causal, lax.dot_general(qc, k_diag, nt, preferred_element_type=F32),
                            NEG_INF) for qc in qs]
        m = [jnp.max(s, -1, keepdims=True) for s in s_diag]
        if i > 0:
            s_off = [lax.dot_general(qc, k_ref[:lo, :], nt, preferred_element_type=F32)
                     for qc in qs]
            m = [jnp.maximum(m[c], jnp.max(s_off[c], -1, keepdims=True)) for c in range(2)]
        p_diag = [jnp.exp(s_diag[c] - m[c]) for c in range(2)]
        l = [jnp.sum(p, -1, keepdims=True) for p in p_diag]
        if i > 0:
            p_off = [jnp.exp(s_off[c] - m[c]) for c in range(2)]
            l = [l[c] + jnp.sum(p_off[c], -1, keepdims=True) for c in range(2)]
        r1 = 1.0 / l[0]
        r2 = lam / l[1]
        a = (p_diag[0] * r1 - p_diag[1] * r2).astype(BF16)
        acc = jnp.dot(a, v_ref[lo:lo + ATT_TQ, :], preferred_element_type=F32)
        if i > 0:
            a = (p_off[0] * r1 - p_off[1] * r2).astype(BF16)
            acc = acc + jnp.dot(a, v_ref[:lo, :], preferred_element_type=F32)
        y = acc * lax.rsqrt(jnp.mean(acc * acc, -1, keepdims=True) + LN_EPS)
        o_ref[lo:lo + ATT_TQ, :] = (y * gain * (1.0 - lambda_init)).astype(BF16)


def _diff_attention(hcat, lam_vecs, norm_g, batch, seq, lambda_init):
    return pl.pallas_call(
        functools.partial(_attn_kernel, lambda_init=lambda_init, seq=seq),
        grid=(batch, DIFF_HEADS),
        in_specs=[pl.BlockSpec((seq, LANES), lambda b, h: (b, h)),
                  pl.BlockSpec((seq, LANES), lambda b, h: (b, DIFF_QK // LANES + h)),
                  pl.BlockSpec((seq, LANES), lambda b, h: (b, 2 * DIFF_QK // LANES + h)),
                  _resident((4, DIFF_DH)),
                  _resident((1, DIFF_DV))],
        out_specs=pl.BlockSpec((seq, DIFF_DV), lambda b, h: (b, h)),
        out_shape=jax.ShapeDtypeStruct((batch * seq, DIFF_WIDTH), BF16),
        compiler_params=_params(2),
        name="diff_attn",
    )(hcat, hcat, hcat, lam_vecs, norm_g)


HALO = 16


def _even_out_kernel(x_ref, attn_ref, cb_ref, cc_ref, cx_ref, hc_ref, hx_ref, cw_ref,
                     wout_ref, g_ref, b_ref, o_ref, *, per_seq):
    i = pl.program_id(0)
    u = cc_ref[...].astype(F32) * cx_ref[...].astype(F32)
    halo = hc_ref[...].astype(F32) * hx_ref[...].astype(F32)
    halo = jnp.where(i % per_seq == 0, jnp.zeros_like(halo), halo)
    row = lax.broadcasted_iota(jnp.int32, u.shape, 0)
    u1 = pltpu.roll(u, 1, 0)
    u1 = jnp.where(row == 0, halo[HALO - 1:HALO, :], u1)
    u2 = pltpu.roll(u, 2, 0)
    u2 = jnp.where(row == 0, halo[HALO - 2:HALO - 1, :], u2)
    u2 = jnp.where(row == 1, halo[HALO - 1:HALO, :], u2)
    cw = cw_ref[...]
    conv = cb_ref[...].astype(F32) * (cw[0:1] * u2 + cw[1:2] * u1 + cw[2:3] * u)
    y = jnp.dot(attn_ref[...], wout_ref[:DIFF_WIDTH, :], preferred_element_type=F32)
    y = y + jnp.dot(conv.astype(BF16), wout_ref[DIFF_WIDTH:, :], preferred_element_type=F32)
    o_ref[...] = _layer_norm(ALPHA * x_ref[...] + y, g_ref[...], b_ref[...])


def _even_out(x, attn, hcat, conv_w, w_out, g, b, seq):
    t = x.shape[0]
    per_seq = seq // TM
    col0 = (2 * DIFF_QK + DIFF_WIDTH) // CONV_CH
    halo_blocks = TM // HALO

    def halo_spec(col):
        return pl.BlockSpec((HALO, CONV_CH),
                            lambda i: (jnp.maximum(i * halo_blocks - 1, 0), col))

    return pl.pallas_call(
        functools.partial(_even_out_kernel, per_seq=per_seq),
        grid=(t // TM,),
        in_specs=[pl.BlockSpec((TM, D_MODEL), lambda i: (i, 0)),
                  pl.BlockSpec((TM, DIFF_WIDTH), lambda i: (i, 0)),
                  pl.BlockSpec((TM, CONV_CH), lambda i: (i, col0)),
                  pl.BlockSpec((TM, CONV_CH), lambda i: (i, col0 + 1)),
                  pl.BlockSpec((TM, CONV_CH), lambda i: (i, col0 + 2)),
                  halo_spec(col0 + 1), halo_spec(col0 + 2),
                  _resident((3, CONV_CH)),
                  _resident((DIFF_WIDTH + CONV_CH, D_MODEL)),
                  _resident((1, D_MODEL)), _resident((1, D_MODEL))],
        out_specs=pl.BlockSpec((TM, D_MODEL), lambda i: (i, 0)),
        out_shape=jax.ShapeDtypeStruct((t, D_MODEL), F32),
        compiler_params=_params(1),
        name="even_out",
    )(x, attn, hcat, hcat, hcat, hcat, hcat, conv_w, w_out, g, b)


def _odd_proj_kernel(x_ref, w_ref, cos_ref, sin_ref, o_ref):
    xb = x_ref[...].astype(BF16)
    cos = cos_ref[...]
    sin = sin_ref[...]
    half = RET_DK // 2
    for n in range(ODD_IN // 512):
        r = jnp.dot(xb, w_ref[:, n * 512:(n + 1) * 512], preferred_element_type=F32)
        if n < 2 * RET_QK // 512:
            for hh in range(512 // RET_DK):
                x1 = r[:, hh * RET_DK:hh * RET_DK + half]
                x2 = r[:, hh * RET_DK + half:(hh + 1) * RET_DK]
                o1 = x1 * cos - x2 * sin
                o2 = x2 * cos + x1 * sin
                if n >= RET_QK // 512:
                    o1 = o1 * RET_DK ** -0.5
                    o2 = o2 * RET_DK ** -0.5
                base = n * 512 + hh * RET_DK
                o_ref[:, base:base + half] = o1.astype(BF16)
                o_ref[:, base + half:base + RET_DK] = o2.astype(BF16)
        else:
            o_ref[:, n * 512:(n + 1) * 512] = r.astype(BF16)


def _odd_proj(x, w, cos, sin, seq):
    t = x.shape[0]
    per_seq = seq // TM
    tbl = pl.BlockSpec((TM, LANES), lambda i: (i % per_seq, 0))
    return pl.pallas_call(
        _odd_proj_kernel,
        grid=(t // TM,),
        in_specs=[pl.BlockSpec((TM, D_MODEL), lambda i: (i, 0)),
                  _resident((D_MODEL, ODD_IN)), tbl, tbl],
        out_specs=pl.BlockSpec((TM, ODD_IN), lambda i: (i, 0)),
        out_shape=jax.ShapeDtypeStruct((t, ODD_IN), BF16),
        compiler_params=_params(1),
        name="odd_proj",
    )(x, w, cos, sin)


def _retention_kernel(cd_ref, q_ref, k_ref, v_ref, gate_ref, dmat_ref, qd_ref, kd_ref, ng_ref,
                      o_ref, *, seq):
    h = pl.program_id(1)
    cd = cd_ref[h]
    dmat = dmat_ref[...]
    qd = jnp.concatenate([qd_ref[...]] * (RET_DV // LANES), axis=1)
    kd = jnp.concatenate([kd_ref[...]] * (RET_DK // LANES), axis=1)
    ng = ng_ref[...]
    state = jnp.zeros((RET_DK, RET_DV), F32)
    for c in range(seq // RET_CHUNK):
        rows = pl.ds(c * RET_CHUNK, RET_CHUNK)
        qc = q_ref[rows, :]
        kc = k_ref[rows, :]
        vc = v_ref[rows, :]
        sc = lax.dot_general(qc, kc, (((1,), (1,)), ((), ())), preferred_element_type=F32) * dmat
        intra = jnp.dot(sc.astype(BF16), vc, preferred_element_type=F32)
        cross = jnp.dot(qc, state.astype(BF16), preferred_element_type=F32) * qd
        kdec = (kc.astype(F32) * kd).astype(BF16)
        state = state * cd + lax.dot_general(kdec, vc, (((0,), (0,)), ((), ())),
                                             preferred_element_type=F32)
        y = intra + cross
        mu = jnp.mean(y, -1, keepdims=True)
        d = y - mu
        var = jnp.mean(d * d, -1, keepdims=True)
        yn = d * lax.rsqrt(var + LN_EPS) * ng
        gt = gate_ref[rows, :].astype(F32)
        o_ref[rows, :] = (gt * jax.nn.sigmoid(gt) * yn).astype(BF16)


def _retention(hcat, tables, norm_g, batch, seq):
    cd, dmat, qd, kd = tables
    tbl = pl.BlockSpec((None, RET_CHUNK, RET_CHUNK), lambda b, h: (h, 0, 0))
    qk_blocks = RET_QK // RET_DK
    v0 = 2 * RET_QK // RET_DV
    g0 = v0 + RET_VW // RET_DV
    return pl.pallas_call(
        functools.partial(_retention_kernel, seq=seq),
        grid=(batch, RET_HEADS),
        in_specs=[pl.BlockSpec(memory_space=pltpu.SMEM),
                  pl.BlockSpec((seq, RET_DK), lambda b, h: (b, h)),
                  pl.BlockSpec((seq, RET_DK), lambda b, h: (b, qk_blocks + h)),
                  pl.BlockSpec((seq, RET_DV), lambda b, h: (b, v0 + h)),
                  pl.BlockSpec((seq, RET_DV), lambda b, h: (b, g0 + h)),
                  tbl, tbl, tbl,
                  pl.BlockSpec((1, RET_DV), lambda b, h: (0, h))],
        out_specs=pl.BlockSpec((seq, RET_DV), lambda b, h: (b, h)),
        out_shape=jax.ShapeDtypeStruct((batch * seq, RET_VW), BF16),
        compiler_params=_params(2),
        name="retention",
    )(cd, hcat, hcat, hcat, hcat, dmat, qd, kd, norm_g)


def _odd_out_kernel(x_ref, y_ref, wout_ref, g_ref, b_ref, o_ref):
    y = jnp.dot(y_ref[...], wout_ref[...], preferred_element_type=F32)
    o_ref[...] = _layer_norm(ALPHA * x_ref[...] + y, g_ref[...], b_ref[...])


def _odd_out(x, y, w_out, g, b):
    t = x.shape[0]
    return pl.pallas_call(
        _odd_out_kernel,
        grid=(t // TM,),
        in_specs=[pl.BlockSpec((TM, D_MODEL), lambda i: (i, 0)),
                  pl.BlockSpec((TM, RET_VW), lambda i: (i, 0)),
                  _resident((RET_VW, D_MODEL)),
                  _resident((1, D_MODEL)), _resident((1, D_MODEL))],
        out_specs=pl.BlockSpec((TM, D_MODEL), lambda i: (i, 0)),
        out_shape=jax.ShapeDtypeStruct((t, D_MODEL), F32),
        compiler_params=_params(1),
        name="odd_out",
    )(x, y, w_out, g, b)


def _even_rope_tables(seq):
    half = ROPE_DIMS // 2
    inv = jnp.exp(-math.log(ROPE_THETA) * jnp.arange(half, dtype=F32) * (2.0 / ROPE_DIMS))
    ang = jnp.arange(seq, dtype=F32)[:, None] * inv[None, :]
    cos, sin = jnp.cos(ang), jnp.sin(ang)
    pad = jnp.zeros((seq, DIFF_DH - ROPE_DIMS), F32)
    zeros = jnp.zeros((seq, half), F32)
    c = jnp.concatenate([cos, cos, pad + 1.0], -1)
    sa = jnp.concatenate([-sin, zeros, pad], -1)
    sb = jnp.concatenate([zeros, sin, pad], -1)
    reps = LANES // DIFF_DH
    return tuple(jnp.tile(t, (1, reps)) for t in (c, sa, sb))


def _odd_rope_tables(seq):
    half = RET_DK // 2
    inv = jnp.exp(-math.log(RET_ROPE_THETA) * jnp.arange(half, dtype=F32) * (2.0 / RET_DK))
    ang = jnp.arange(seq, dtype=F32)[:, None] * inv[None, :]
    return jnp.cos(ang), jnp.sin(ang)


def _retention_tables():
    log_g = jnp.log1p(-jnp.exp2(-5.0 - jnp.arange(RET_HEADS, dtype=F32)))
    idx = jnp.arange(RET_CHUNK, dtype=F32)
    rel = idx[:, None] - idx[None, :]
    dmat = jnp.where(rel >= 0, jnp.exp(log_g[:, None, None] * jnp.maximum(rel, 0.0)), 0.0)
    q_decay = jnp.exp(log_g[:, None] * (idx + 1.0))
    k_decay = jnp.exp(log_g[:, None] * (RET_CHUNK - 1.0 - idx))
    cd = jnp.exp(log_g * RET_CHUNK)
    rep = lambda t: jnp.broadcast_to(t[:, :, None], (RET_HEADS, RET_CHUNK, LANES))
    return cd, dmat, rep(q_decay), rep(k_decay)


def kernel(x, ln_g, ln_b, ffn_w_in, ffn_w_out, even_w_in, even_w_out, diff_lambda,
           diff_norm_g, conv_w, odd_w_in, odd_w_out, ret_norm_g):
    batch, seq, _ = x.shape
    assert seq % TM == 0 and seq % ATT_TQ == 0
    h = x.reshape(batch * seq, D_MODEL)
    ffn_w_in = ffn_w_in.astype(BF16)
    ffn_w_out = ffn_w_out.astype(BF16)
    row = lambda v: v.reshape(1, -1)
    for i in range(DEPTH):
        j = i // 2
        h = _ffn_ln(h, ffn_w_in, ffn_w_out, i, 0, row(ln_g[i, 0]), row(ln_b[i, 0]))
        if i % 2 == 0:
            lambda_init = 0.8 - 0.6 * math.exp(-0.3 * i)
            hcat = _even_proj(h, even_w_in[j].astype(BF16), *_even_rope_tables(seq), seq)
            attn = _diff_attention(hcat, diff_lambda[j], row(diff_norm_g[j]), batch, seq,
                                   lambda_init)
            h = _even_out(h, attn, hcat, conv_w[j], even_w_out[j].astype(BF16),
                          row(ln_g[i, 1]), row(ln_b[i, 1]), seq)
        else:
            hcat = _odd_proj(h, odd_w_in[j].astype(BF16), *_odd_rope_tables(seq), seq)
            y = _retention(hcat, _retention_tables(), row(ret_norm_g[j]), batch, seq)
            h = _odd_out(h, y, odd_w_out[j].astype(BF16), row(ln_g[i, 1]), row(ln_b[i, 1]))
        h = _ffn_ln(h, ffn_w_in, ffn_w_out, i, 1, row(ln_g[i, 2]), row(ln_b[i, 2]))
    return h.reshape(batch, seq, D_MODEL)
```

```python
import functools
import math

import jax
import jax.numpy as jnp
from jax import lax
from jax.experimental import pallas as pl
from jax.experimental.pallas import tpu as pltpu

F32 = jnp.float32
BF16 = jnp.bfloat16

D_MODEL = 1024
DEPTH = 2
D_FF = 2816
DIFF_HEADS = 4
DIFF_DH = 64
DIFF_DV = 128
DIFF_QK = 512
DIFF_WIDTH = 512
ROPE_THETA = 500000.0
ROPE_DIMS = 16
CONV_CH = 512
EVEN_IN = 3072
RET_HEADS = 4
RET_DK = 256
RET_DV = 512
RET_QK = 1024
RET_VW = 2048
RET_CHUNK = 128
RET_ROPE_THETA = 10000.0
ODD_IN = 6144
ALPHA = (2.0 * DEPTH) ** 0.25
LN_EPS = 1e-5
NEG_INF = -1e30

LANES = 128
VMEM_LIMIT = 56 * 1024 * 1024

TM = 1024
FF_CHUNK = 256
OUT_SPLIT = 4
ATT_TQ = 256
ATT_STRIP = 16


def _layer_norm(z, g, b):
    mu = jnp.mean(z, -1, keepdims=True)
    d = z - mu
    var = jnp.mean(d * d, -1, keepdims=True)
    return d * lax.rsqrt(var + LN_EPS) * g + b


def _params(n_axes, vmem=VMEM_LIMIT):
    return pltpu.CompilerParams(dimension_semantics=("arbitrary",) * n_axes,
                                vmem_limit_bytes=vmem)


def _resident(shape):
    nd = len(shape)
    return pl.BlockSpec(shape, lambda *_: (0,) * nd, pipeline_mode=pl.Buffered(1))


def _ffn_kernel(x_ref, win_ref, wout_ref, g_ref, b_ref, o_ref, act_ref):
    x = x_ref[...]
    xb = x.astype(BF16)
    for c in range(D_FF // FF_CHUNK):
        lo = c * FF_CHUNK
        gate = jnp.dot(xb, win_ref[:, lo:lo + FF_CHUNK], preferred_element_type=F32)
        up = jnp.dot(xb, win_ref[:, D_FF + lo:D_FF + lo + FF_CHUNK], preferred_element_type=F32)
        act_ref[:, lo:lo + FF_CHUNK] = (gate * jax.nn.sigmoid(gate) * up).astype(BF16)
    rows = TM // OUT_SPLIT
    ys = [jnp.dot(act_ref[r * rows:(r + 1) * rows, :], wout_ref[...],
                  preferred_element_type=F32) for r in range(OUT_SPLIT)]
    for r in range(OUT_SPLIT):
        sl = slice(r * rows, (r + 1) * rows)
        o_ref[sl, :] = _layer_norm(ALPHA * x_ref[sl, :] + 0.5 * ys[r], g_ref[...], b_ref[...])


def _ffn_ln(x, w_in, w_out, layer, which, g, b):
    t = x.shape[0]
    pick = lambda i: (layer, which, 0, 0)
    return pl.pallas_call(
        _ffn_kernel,
        grid=(t // TM,),
        in_specs=[pl.BlockSpec((TM, D_MODEL), lambda i: (i, 0)),
                  pl.BlockSpec((None, None, D_MODEL, 2 * D_FF), pick,
                               pipeline_mode=pl.Buffered(1)),
                  pl.BlockSpec((None, None, D_FF, D_MODEL), pick,
                               pipeline_mode=pl.Buffered(1)),
                  _resident((1, D_MODEL)),
                  _resident((1, D_MODEL))],
        out_specs=pl.BlockSpec((TM, D_MODEL), lambda i: (i, 0)),
        out_shape=jax.ShapeDtypeStruct((t, D_MODEL), F32),
        scratch_shapes=[pltpu.VMEM((TM, D_FF), BF16)],
        compiler_params=_params(1),
        name="ffn_ln",
    )(x, w_in, w_out, g, b)


def _even_proj_kernel(x_ref, w_ref, c_ref, sa_ref, sb_ref, o_ref):
    xb = x_ref[...].astype(BF16)
    c = c_ref[...]
    sa = sa_ref[...]
    sb = sb_ref[...]
    for n in range(EVEN_IN // 512):
        r = jnp.dot(xb, w_ref[:, n * 512:(n + 1) * 512], preferred_element_type=F32)
        if n < 2:
            scale = DIFF_DH ** -0.5 if n == 0 else 1.0
            for j in range(512 // LANES):
                blk = r[:, j * LANES:(j + 1) * LANES]
                rot = (blk * c + pltpu.roll(blk, LANES - 8, 1) * sa
                       + pltpu.roll(blk, 8, 1) * sb)
                if n == 0:
                    rot = rot * scale
                o_ref[:, n * 512 + j * LANES:n * 512 + (j + 1) * LANES] = rot.astype(BF16)
        else:
            o_ref[:, n * 512:(n + 1) * 512] = r.astype(BF16)


def _even_proj(x, w, c, sa, sb, seq):
    t = x.shape[0]
    per_seq = seq // TM
    tbl = pl.BlockSpec((TM, LANES), lambda i: (i % per_seq, 0))
    return pl.pallas_call(
        _even_proj_kernel,
        grid=(t // TM,),
        in_specs=[pl.BlockSpec((TM, D_MODEL), lambda i: (i, 0)),
                  _resident((D_MODEL, EVEN_IN)), tbl, tbl, tbl],
        out_specs=pl.BlockSpec((TM, EVEN_IN), lambda i: (i, 0)),
        out_shape=jax.ShapeDtypeStruct((t, EVEN_IN), BF16),
        compiler_params=_params(1),
        name="even_proj",
    )(x, w, c, sa, sb)


def _attn_kernel(q_ref, k_ref, v_ref, lv_ref, g_ref, o_ref, s_ref, p_ref, *, lambda_init, seq):
    lv = lv_ref[...]
    lam = (jnp.exp(jnp.sum(lv[0:1] * lv[1:2], -1, keepdims=True))
           - jnp.exp(jnp.sum(lv[2:3] * lv[3:4], -1, keepdims=True)) + lambda_init)
    lane = lax.broadcasted_iota(jnp.int32, (ATT_TQ, LANES), 1)
    row = lax.broadcasted_iota(jnp.int32, (ATT_TQ, ATT_TQ), 0)
    col = lax.broadcasted_iota(jnp.int32, (ATT_TQ, ATT_TQ), 1)
    causal = col <= row
    gain = g_ref[...]
    nt = (((1,), (1,)), ((), ()))
    n_tiles = seq // ATT_TQ
    order = [t for pair in zip(range(n_tiles - 1, -1, -1), range(n_tiles)) for t in pair][:n_tiles]
    for pos, i in enumerate(order):
        buf = pos % 2
        lo = i * ATT_TQ
        q = q_ref[lo:lo + ATT_TQ, :]
        zero = jnp.zeros_like(q)
        qs = (jnp.where(lane < DIFF_DH, q, zero), jnp.where(lane >= DIFF_DH, q, zero))
        hi = lo + ATT_TQ
        l = []
        for c, qc in enumerate(qs):
            s = lax.dot_general(qc, k_ref[:hi, :], nt, preferred_element_type=F32)
            sc_ref = s_ref.at[buf, c]
            if i > 0:
                sc_ref[:, :lo] = s[:, :lo]
            sc_ref[:, lo:hi] = jnp.where(causal, s[:, lo:], NEG_INF)
            m = jnp.max(sc_ref[:, :hi], -1, keepdims=True)
            lsum = []
            for r in range(ATT_TQ // ATT_STRIP):
                sl = slice(r * ATT_STRIP, (r + 1) * ATT_STRIP)
                p = jnp.exp(sc_ref[sl, :hi] - m[sl])
                lsum.append(jnp.sum(p, -1, keepdims=True))
                p_ref[buf, c * ATT_TQ + r * ATT_STRIP:c * ATT_TQ + (r + 1) * ATT_STRIP, :hi] = (
                    p.astype(BF16))
            l.append(jnp.concatenate(lsum, 0))
        acc = jnp.dot(p_ref[buf, :, :hi], v_ref[:hi, :], preferred_element_type=F32)
        acc = acc[:ATT_TQ] * (1.0 / l[0]) - acc[ATT_TQ:] * (lam / l[1])
        y = acc * lax.rsqrt(jnp.mean(acc * acc, -1, keepdims=True) + LN_EPS)
        o_ref[lo:lo + ATT_TQ, :] = (y * gain * (1.0 - lambda_init)).astype(BF16)


def _diff_attention(hcat, lam_vecs, norm_g, batch, seq, lambda_init):
    return pl.pallas_call(
        functools.partial(_attn_kernel, lambda_init=lambda_init, seq=seq),
        grid=(batch, DIFF_HEADS),
        in_specs=[pl.BlockSpec((seq, LANES), lambda b, h: (b, h)),
                  pl.BlockSpec((seq, LANES), lambda b, h: (b, DIFF_QK // LANES + h)),
                  pl.BlockSpec((seq, LANES), lambda b, h: (b, 2 * DIFF_QK // LANES + h)),
                  _resident((4, DIFF_DH)),
                  _resident((1, DIFF_DV))],
        out_specs=pl.BlockSpec((seq, DIFF_DV), lambda b, h: (b, h)),
        out_shape=jax.ShapeDtypeStruct((batch * seq, DIFF_WIDTH), BF16),
        scratch_shapes=[pltpu.VMEM((2, 2, ATT_TQ, seq), F32),
                        pltpu.VMEM((2, 2 * ATT_TQ, seq), BF16)],
        compiler_params=_params(2),
        name="diff_attn",
    )(hcat, hcat, hcat, lam_vecs, norm_g)


HALO = 16


def _even_out_kernel(x_ref, attn_ref, cb_ref, cc_ref, cx_ref, hc_ref, hx_ref, cw_ref,
                     wout_ref, g_ref, b_ref, o_ref, *, per_seq):
    i = pl.program_id(0)
    u = cc_ref[...].astype(F32) * cx_ref[...].astype(F32)
    halo = hc_ref[...].astype(F32) * hx_ref[...].astype(F32)
    halo = jnp.where(i % per_seq == 0, jnp.zeros_like(halo), halo)
    row = lax.broadcasted_iota(jnp.int32, (8, CONV_CH), 0)
    u1 = pltpu.roll(u, 1, 0)
    u2 = pltpu.roll(u, 2, 0)
    head1 = jnp.where(row == 0, halo[HALO - 1:HALO, :], u1[:8])
    head2 = jnp.where(row == 0, halo[HALO - 2:HALO - 1, :], u2[:8])
    head2 = jnp.where(row == 1, halo[HALO - 1:HALO, :], head2)
    u1 = jnp.concatenate([head1, u1[8:]], 0)
    u2 = jnp.concatenate([head2, u2[8:]], 0)
    cw = cw_ref[...]
    conv = (cb_ref[...].astype(F32) * (cw[0:1] * u2 + cw[1:2] * u1 + cw[2:3] * u)).astype(BF16)
    rows = TM // OUT_SPLIT
    ys = []
    for r in range(OUT_SPLIT):
        sl = slice(r * rows, (r + 1) * rows)
        mix = jnp.concatenate([attn_ref[sl, :], conv[sl]], 1)
        ys.append(jnp.dot(mix, wout_ref[...], preferred_element_type=F32))
    for r in range(OUT_SPLIT):
        sl = slice(r * rows, (r + 1) * rows)
        o_ref[sl, :] = _layer_norm(ALPHA * x_ref[sl, :] + ys[r], g_ref[...], b_ref[...])


def _even_out(x, attn, hcat, conv_w, w_out, g, b, seq):
    t = x.shape[0]
    per_seq = seq // TM
    col0 = (2 * DIFF_QK + DIFF_WIDTH) // CONV_CH
    halo_blocks = TM // HALO

    def halo_spec(col):
        return pl.BlockSpec((HALO, CONV_CH),
                            lambda i: (jnp.maximum(i * halo_blocks - 1, 0), col))

    return pl.pallas_call(
        functools.partial(_even_out_kernel, per_seq=per_seq),
        grid=(t // TM,),
        in_specs=[pl.BlockSpec((TM, D_MODEL), lambda i: (i, 0)),
                  pl.BlockSpec((TM, DIFF_WIDTH), lambda i: (i, 0)),
                  pl.BlockSpec((TM, CONV_CH), lambda i: (i, col0)),
                  pl.BlockSpec((TM, CONV_CH), lambda i: (i, col0 + 1)),
                  pl.BlockSpec((TM, CONV_CH), lambda i: (i, col0 + 2)),
                  halo_spec(col0 + 1), halo_spec(col0 + 2),
                  _resident((3, CONV_CH)),
                  _resident((DIFF_WIDTH + CONV_CH, D_MODEL)),
                  _resident((1, D_MODEL)), _resident((1, D_MODEL))],
        out_specs=pl.BlockSpec((TM, D_MODEL), lambda i: (i, 0)),
        out_shape=jax.ShapeDtypeStruct((t, D_MODEL), F32),
        compiler_params=_params(1),
        name="even_out",
    )(x, attn, hcat, hcat, hcat, hcat, hcat, conv_w, w_out, g, b)


def _odd_proj_kernel(x_ref, w_ref, cos_ref, sin_ref, o_ref):
    xb = x_ref[...].astype(BF16)
    cos = cos_ref[...]
    sin = sin_ref[...]
    half = RET_DK // 2
    for n in range(ODD_IN // 512):
        r = jnp.dot(xb, w_ref[:, n * 512:(n + 1) * 512], preferred_element_type=F32)
        if n < 2 * RET_QK // 512:
            for hh in range(512 // RET_DK):
                x1 = r[:, hh * RET_DK:hh * RET_DK + half]
                x2 = r[:, hh * RET_DK + half:(hh + 1) * RET_DK]
                o1 = x1 * cos - x2 * sin
                o2 = x2 * cos + x1 * sin
                if n >= RET_QK // 512:
                    o1 = o1 * RET_DK ** -0.5
                    o2 = o2 * RET_DK ** -0.5
                base = n * 512 + hh * RET_DK
                o_ref[:, base:base + half] = o1.astype(BF16)
                o_ref[:, base + half:base + RET_DK] = o2.astype(BF16)
        else:
            o_ref[:, n * 512:(n + 1) * 512] = r.astype(BF16)


def _odd_proj(x, w, cos, sin, seq):
    t = x.shape[0]
    per_seq = seq // TM
    tbl = pl.BlockSpec((TM, LANES), lambda i: (i % per_seq, 0))
    return pl.pallas_call(
        _odd_proj_kernel,
        grid=(t // TM,),
        in_specs=[pl.BlockSpec((TM, D_MODEL), lambda i: (i, 0)),
                  _resident((D_MODEL, ODD_IN)), tbl, tbl],
        out_specs=pl.BlockSpec((TM, ODD_IN), lambda i: (i, 0)),
        out_shape=jax.ShapeDtypeStruct((t, ODD_IN), BF16),
        compiler_params=_params(1),
        name="odd_proj",
    )(x, w, cos, sin)


def _retention_kernel(cd_ref, q_ref, k_ref, v_ref, gate_ref, dmat_ref, qd_ref, kd_ref, ng_ref,
                      o_ref, *, seq):
    h = pl.program_id(1)
    cd = cd_ref[h]
    dmat = dmat_ref[...]
    qd = jnp.concatenate([qd_ref[...]] * (RET_DV // LANES), axis=1)
    kd = jnp.concatenate([kd_ref[...]] * (RET_DK // LANES), axis=1)
    ng = ng_ref[...]
    state = jnp.zeros((RET_DK, RET_DV), F32)
    for c in range(seq // RET_CHUNK):
        rows = pl.ds(c * RET_CHUNK, RET_CHUNK)
        qc = q_ref[rows, :]
        kc = k_ref[rows, :]
        vc = v_ref[rows, :]
        sc = lax.dot_general(qc, kc, (((1,), (1,)), ((), ())), preferred_element_type=F32) * dmat
        intra = jnp.dot(sc.astype(BF16), vc, preferred_element_type=F32)
        cross = jnp.dot(qc, state.astype(BF16), preferred_element_type=F32) * qd
        kdec = (kc.astype(F32) * kd).astype(BF16)
        state = state * cd + lax.dot_general(kdec, vc, (((0,), (0,)), ((), ())),
                                             preferred_element_type=F32)
        y = intra + cross
        mu = jnp.mean(y, -1, keepdims=True)
        d = y - mu
        var = jnp.mean(d * d, -1, keepdims=True)
        yn = d * lax.rsqrt(var + LN_EPS) * ng
        gt = gate_ref[rows, :].astype(F32)
        o_ref[rows, :] = (gt * jax.nn.sigmoid(gt) * yn).astype(BF16)


def _retention(hcat, tables, norm_g, batch, seq):
    cd, dmat, qd, kd = tables
    tbl = pl.BlockSpec((None, RET_CHUNK, RET_CHUNK), lambda b, h: (h, 0, 0))
    qk_blocks = RET_QK // RET_DK
    v0 = 2 * RET_QK // RET_DV
    g0 = v0 + RET_VW // RET_DV
    return pl.pallas_call(
        functools.partial(_retention_kernel, seq=seq),
        grid=(batch, RET_HEADS),
        in_specs=[pl.BlockSpec(memory_space=pltpu.SMEM),
                  pl.BlockSpec((seq, RET_DK), lambda b, h: (b, h)),
                  pl.BlockSpec((seq, RET_DK), lambda b, h: (b, qk_blocks + h)),
                  pl.BlockSpec((seq, RET_DV), lambda b, h: (b, v0 + h)),
                  pl.BlockSpec((seq, RET_DV), lambda b, h: (b, g0 + h)),
                  tbl, tbl, tbl,
                  pl.BlockSpec((1, RET_DV), lambda b, h: (0, h))],
        out_specs=pl.BlockSpec((seq, RET_DV), lambda b, h: (b, h)),
        out_shape=jax.ShapeDtypeStruct((batch * seq, RET_VW), BF16),
        compiler_params=_params(2),
        name="retention",
    )(cd, hcat, hcat, hcat, hcat, dmat, qd, kd, norm_g)


def _odd_out_kernel(x_ref, y_ref, wout_ref, g_ref, b_ref, o_ref):
    rows = TM // OUT_SPLIT
    ys = [jnp.dot(y_ref[r * rows:(r + 1) * rows, :], wout_ref[...],
                  preferred_element_type=F32) for r in range(OUT_SPLIT)]
    for r in range(OUT_SPLIT):
        sl = slice(r * rows, (r + 1) * rows)
        o_ref[sl, :] = _layer_norm(ALPHA * x_ref[sl, :] + ys[r], g_ref[...], b_ref[...])


def _odd_out(x, y, w_out, g, b):
    t = x.shape[0]
    return pl.pallas_call(
        _odd_out_kernel,
        grid=(t // TM,),
        in_specs=[pl.BlockSpec((TM, D_MODEL), lambda i: (i, 0)),
                  pl.BlockSpec((TM, RET_VW), lambda i: (i, 0)),
                  _resident((RET_VW, D_MODEL)),
                  _resident((1, D_MODEL)), _resident((1, D_MODEL))],
        out_specs=pl.BlockSpec((TM, D_MODEL), lambda i: (i, 0)),
        out_shape=jax.ShapeDtypeStruct((t, D_MODEL), F32),
        compiler_params=_params(1),
        name="odd_out",
    )(x, y, w_out, g, b)


def _even_rope_tables(seq):
    half = ROPE_DIMS // 2
    inv = jnp.exp(-math.log(ROPE_THETA) * jnp.arange(half, dtype=F32) * (2.0 / ROPE_DIMS))
    ang = jnp.arange(seq, dtype=F32)[:, None] * inv[None, :]
    cos, sin = jnp.cos(ang), jnp.sin(ang)
    pad = jnp.zeros((seq, DIFF_DH - ROPE_DIMS), F32)
    zeros = jnp.zeros((seq, half), F32)
    c = jnp.concatenate([cos, cos, pad + 1.0], -1)
    sa = jnp.concatenate([-sin, zeros, pad], -1)
    sb = jnp.concatenate([zeros, sin, pad], -1)
    reps = LANES // DIFF_DH
    return tuple(jnp.tile(t, (1, reps)) for t in (c, sa, sb))


def _odd_rope_tables(seq):
    half = RET_DK // 2
    inv = jnp.exp(-math.log(RET_ROPE_THETA) * jnp.arange(half, dtype=F32) * (2.0 / RET_DK))
    ang = jnp.arange(seq, dtype=F32)[:, None] * inv[None, :]
    return jnp.cos(ang), jnp.sin(ang)


def _retention_tables():
    log_g = jnp.log1p(-jnp.exp2(-5.0 - jnp.arange(RET_HEADS, dtype=F32)))
    idx = jnp.arange(RET_CHUNK, dtype=F32)
    rel = idx[:, None] - idx[None, :]
    dmat = jnp.where(rel >= 0, jnp.exp(log_g[:, None, None] * jnp.maximum(rel, 0.0)), 0.0)
    q_decay = jnp.exp(log_g[:, None] * (idx + 1.0))
    k_decay = jnp.exp(log_g[:, None] * (RET_CHUNK - 1.0 - idx))
    cd = jnp.exp(log_g * RET_CHUNK)
    rep = lambda t: jnp.broadcast_to(t[:, :, None], (RET_HEADS, RET_CHUNK, LANES))
    return cd, dmat, rep(q_decay), rep(k_decay)


def kernel(x, ln_g, ln_b, ffn_w_in, ffn_w_out, even_w_in, even_w_out, diff_lambda,
           diff_norm_g, conv_w, odd_w_in, odd_w_out, ret_norm_g):
    batch, seq, _ = x.shape
    assert seq % TM == 0 and seq % ATT_TQ == 0
    h = x.reshape(batch * seq, D_MODEL)
    ffn_w_in = ffn_w_in.astype(BF16)
    ffn_w_out = ffn_w_out.astype(BF16)
    row = lambda v: v.reshape(1, -1)
    for i in range(DEPTH):
        j = i // 2
        h = _ffn_ln(h, ffn_w_in, ffn_w_out, i, 0, row(ln_g[i, 0]), row(ln_b[i, 0]))
        if i % 2 == 0:
            lambda_init = 0.8 - 0.6 * math.exp(-0.3 * i)
            hcat = _even_proj(h, even_w_in[j].astype(BF16), *_even_rope_tables(seq), seq)
            attn = _diff_attention(hcat, diff_lambda[j], row(diff_norm_g[j]), batch, seq,
                                   lambda_init)
            h = _even_out(h, attn, hcat, conv_w[j], even_w_out[j].astype(BF16),
                          row(ln_g[i, 1]), row(ln_b[i, 1]), seq)
        else:
            hcat = _odd_proj(h, odd_w_in[j].astype(BF16), *_odd_rope_tables(seq), seq)
            y = _retention(hcat, _retention_tables(), row(ret_norm_g[j]), batch, seq)
            h = _odd_out(h, y, odd_w_out[j].astype(BF16), row(ln_g[i, 1]), row(ln_b[i, 1]))
        h = _ffn_ln(h, ffn_w_in, ffn_w_out, i, 1, row(ln_g[i, 2]), row(ln_b[i, 2]))
    return h.reshape(batch, seq, D_MODEL)
```

```python
import functools
import math

import jax
import jax.numpy as jnp
from jax import lax
from jax.experimental import pallas as pl
from jax.experimental.pallas import tpu as pltpu

F32 = jnp.float32
BF16 = jnp.bfloat16

D_MODEL = 1024
DEPTH = 2
D_FF = 2816
DIFF_HEADS = 4
DIFF_DH = 64
DIFF_DV = 128
DIFF_QK = 512
DIFF_WIDTH = 512
ROPE_THETA = 500000.0
ROPE_DIMS = 16
CONV_CH = 512
EVEN_IN = 3072
RET_HEADS = 4
RET_DK = 256
RET_DV = 512
RET_QK = 1024
RET_VW = 2048
RET_CHUNK = 128
RET_ROPE_THETA = 10000.0
ODD_IN = 6144
ALPHA = (2.0 * DEPTH) ** 0.25
LN_EPS = 1e-5
NEG_INF = -1e30

LANES = 128
VMEM_LIMIT = 56 * 1024 * 1024

TM = 1024
FF_CHUNK = 256
OUT_SPLIT = 4
ATT_TQ = 256
ATT_STRIP = 16


def _layer_norm(z, g, b):
    mu = jnp.mean(z, -1, keepdims=True)
    d = z - mu
    var = jnp.mean(d * d, -1, keepdims=True)
    return d * lax.rsqrt(var + LN_EPS) * g + b


def _params(n_axes, vmem=VMEM_LIMIT):
    return pltpu.CompilerParams(dimension_semantics=("arbitrary",) * n_axes,
                                vmem_limit_bytes=vmem)


def _resident(shape):
    nd = len(shape)
    return pl.BlockSpec(shape, lambda *_: (0,) * nd, pipeline_mode=pl.Buffered(1))


def _cast_specs(casts, steps):
    in_specs, out_specs, out_shapes = [], [], []
    for arr, prefix in casts:
        r, c = arr.shape[-2:]
        assert arr.ndim == len(prefix) + 2 and r % (16 * steps) == 0
        rows = r // steps
        in_specs.append(pl.BlockSpec((None,) * len(prefix) + (rows, c),
                                     lambda i, p=prefix: p + (i, 0)))
        out_specs.append(pl.BlockSpec((rows, c), lambda i: (i, 0)))
        out_shapes.append(jax.ShapeDtypeStruct((r, c), BF16))
    return in_specs, out_specs, out_shapes


def _do_casts(src_refs, dst_refs):
    for src, dst in zip(src_refs, dst_refs):
        dst[...] = src[...].astype(BF16)


def _ffn_kernel(x_ref, win_ref, wout_ref, g_ref, b_ref, *rest, n_cast):
    cast_src, o_ref, cast_dst, act_ref = (rest[:n_cast], rest[n_cast],
                                          rest[n_cast + 1:2 * n_cast + 1], rest[-1])
    _do_casts(cast_src, cast_dst)
    x = x_ref[...]
    xb = x.astype(BF16)
    for c in range(D_FF // FF_CHUNK):
        lo = c * FF_CHUNK
        gate = jnp.dot(xb, win_ref[:, lo:lo + FF_CHUNK], preferred_element_type=F32)
        up = jnp.dot(xb, win_ref[:, D_FF + lo:D_FF + lo + FF_CHUNK], preferred_element_type=F32)
        act_ref[:, lo:lo + FF_CHUNK] = (gate * jax.nn.sigmoid(gate) * up).astype(BF16)
    rows = TM // OUT_SPLIT
    ys = [jnp.dot(act_ref[r * rows:(r + 1) * rows, :], wout_ref[...],
                  preferred_element_type=F32) for r in range(OUT_SPLIT)]
    for r in range(OUT_SPLIT):
        sl = slice(r * rows, (r + 1) * rows)
        o_ref[sl, :] = _layer_norm(ALPHA * x_ref[sl, :] + 0.5 * ys[r], g_ref[...], b_ref[...])


def _ffn_ln(x, w_in, w_out, g, b, casts=()):
    t = x.shape[0]
    steps = t // TM
    cast_in, cast_out, cast_shapes = _cast_specs(casts, steps)
    out = pl.pallas_call(
        functools.partial(_ffn_kernel, n_cast=len(casts)),
        grid=(steps,),
        in_specs=[pl.BlockSpec((TM, D_MODEL), lambda i: (i, 0)),
                  _resident((D_MODEL, 2 * D_FF)),
                  _resident((D_FF, D_MODEL)),
                  _resident((1, D_MODEL)),
                  _resident((1, D_MODEL))] + cast_in,
        out_specs=[pl.BlockSpec((TM, D_MODEL), lambda i: (i, 0))] + cast_out,
        out_shape=[jax.ShapeDtypeStruct((t, D_MODEL), F32)] + cast_shapes,
        scratch_shapes=[pltpu.VMEM((TM, D_FF), BF16)],
        compiler_params=_params(1),
        name="ffn_ln",
    )(x, w_in, w_out, g, b, *[arr for arr, _ in casts])
    return out[0], out[1:]


def _even_proj_kernel(x_ref, w_ref, c_ref, sa_ref, sb_ref, *rest, n_cast):
    cast_src, o_ref, cast_dst = rest[:n_cast], rest[n_cast], rest[n_cast + 1:]
    _do_casts(cast_src, cast_dst)
    xb = x_ref[...].astype(BF16)
    c = c_ref[...]
    sa = sa_ref[...]
    sb = sb_ref[...]
    for n in range(EVEN_IN // 512):
        r = jnp.dot(xb, w_ref[:, n * 512:(n + 1) * 512], preferred_element_type=F32)
        if n < 2:
            scale = DIFF_DH ** -0.5 if n == 0 else 1.0
            for j in range(512 // LANES):
                blk = r[:, j * LANES:(j + 1) * LANES]
                rot = (blk * c + pltpu.roll(blk, LANES - 8, 1) * sa
                       + pltpu.roll(blk, 8, 1) * sb)
                if n == 0:
                    rot = rot * scale
                o_ref[:, n * 512 + j * LANES:n * 512 + (j + 1) * LANES] = rot.astype(BF16)
        else:
            o_ref[:, n * 512:(n + 1) * 512] = r.astype(BF16)


def _even_proj(x, w, c, sa, sb, seq, casts=()):
    t = x.shape[0]
    steps = t // TM
    per_seq = seq // TM
    tbl = pl.BlockSpec((TM, LANES), lambda i: (i % per_seq, 0))
    cast_in, cast_out, cast_shapes = _cast_specs(casts, steps)
    out = pl.pallas_call(
        functools.partial(_even_proj_kernel, n_cast=len(casts)),
        grid=(steps,),
        in_specs=[pl.BlockSpec((TM, D_MODEL), lambda i: (i, 0)),
                  _resident((D_MODEL, EVEN_IN)), tbl, tbl, tbl] + cast_in,
        out_specs=[pl.BlockSpec((TM, EVEN_IN), lambda i: (i, 0))] + cast_out,
        out_shape=[jax.ShapeDtypeStruct((t, EVEN_IN), BF16)] + cast_shapes,
        compiler_params=_params(1),
        name="even_proj",
    )(x, w, c, sa, sb, *[arr for arr, _ in casts])
    return out[0], out[1:]


def _attn_kernel(q_ref, k_ref, v_ref, lv_ref, g_ref, o_ref, s_ref, p_ref, *, lambda_init, seq):
    lv = lv_ref[...]
    lam = (jnp.exp(jnp.sum(lv[0:1] * lv[1:2], -1, keepdims=True))
           - jnp.exp(jnp.sum(lv[2:3] * lv[3:4], -1, keepdims=True)) + lambda_init)
    lane = lax.broadcasted_iota(jnp.int32, (ATT_TQ, LANES), 1)
    row = lax.broadcasted_iota(jnp.int32, (ATT_TQ, ATT_TQ), 0)
    col = lax.broadcasted_iota(jnp.int32, (ATT_TQ, ATT_TQ), 1)
    causal = col <= row
    gain = g_ref[...]
    nt = (((1,), (1,)), ((), ()))
    n_tiles = seq // ATT_TQ
    order = [t for pair in zip(range(n_tiles - 1, -1, -1), range(n_tiles)) for t in pair][:n_tiles]
    for pos, i in enumerate(order):
        buf = pos % 2
        lo = i * ATT_TQ
        q = q_ref[lo:lo + ATT_TQ, :]
        zero = jnp.zeros_like(q)
        qs = (jnp.where(lane < DIFF_DH, q, zero), jnp.where(lane >= DIFF_DH, q, zero))
        hi = lo + ATT_TQ
        l = []
        for c, qc in enumerate(qs):
            s = lax.dot_general(qc, k_ref[:hi, :], nt, preferred_element_type=F32)
            sc_ref = s_ref.at[buf, c]
            if i > 0:
                sc_ref[:, :lo] = s[:, :lo]
            sc_ref[:, lo:hi] = jnp.where(causal, s[:, lo:], NEG_INF)
            m = jnp.max(sc_ref[:, :hi], -1, keepdims=True)
            lsum = []
            for r in range(ATT_TQ // ATT_STRIP):
                sl = slice(r * ATT_STRIP, (r + 1) * ATT_STRIP)
                p = jnp.exp(sc_ref[sl, :hi] - m[sl])
                lsum.append(jnp.sum(p, -1, keepdims=True))
                p_ref[buf, c * ATT_TQ + r * ATT_STRIP:c * ATT_TQ + (r + 1) * ATT_STRIP, :hi] = (
                    p.astype(BF16))
            l.append(jnp.concatenate(lsum, 0))
        acc = jnp.dot(p_ref[buf, :, :hi], v_ref[:hi, :], preferred_element_type=F32)
        acc = acc[:ATT_TQ] * (1.0 / l[0]) - acc[ATT_TQ:] * (lam / l[1])
        y = acc * lax.rsqrt(jnp.mean(acc * acc, -1, keepdims=True) + LN_EPS)
        o_ref[lo:lo + ATT_TQ, :] = (y * gain * (1.0 - lambda_init)).astype(BF16)


def _diff_attention(hcat, lam_vecs, norm_g, batch, seq, lambda_init):
    return pl.pallas_call(
        functools.partial(_attn_kernel, lambda_init=lambda_init, seq=seq),
        grid=(batch, DIFF_HEADS),
        in_specs=[pl.BlockSpec((seq, LANES), lambda b, h: (b, h)),
                  pl.BlockSpec((seq, LANES), lambda b, h: (b, DIFF_QK // LANES + h)),
                  pl.BlockSpec((seq, LANES), lambda b, h: (b, 2 * DIFF_QK // LANES + h)),
                  _resident((4, DIFF_DH)),
                  _resident((1, DIFF_DV))],
        out_specs=pl.BlockSpec((seq, DIFF_DV), lambda b, h: (b, h)),
        out_shape=jax.ShapeDtypeStruct((batch * seq, DIFF_WIDTH), BF16),
        scratch_shapes=[pltpu.VMEM((2, 2, ATT_TQ, seq), F32),
                        pltpu.VMEM((2, 2 * ATT_TQ, seq), BF16)],
        compiler_params=_params(2),
        name="diff_attn",
    )(hcat, hcat, hcat, lam_vecs, norm_g)


HALO = 16


def _even_out_kernel(x_ref, attn_ref, cb_ref, cc_ref, cx_ref, hc_ref, hx_ref, cw_ref,
                     wout_ref, g_ref, b_ref, o_ref, *, per_seq):
    i = pl.program_id(0)
    u = cc_ref[...].astype(F32) * cx_ref[...].astype(F32)
    halo = hc_ref[...].astype(F32) * hx_ref[...].astype(F32)
    halo = jnp.where(i % per_seq == 0, jnp.zeros_like(halo), halo)
    row = lax.broadcasted_iota(jnp.int32, (8, CONV_CH), 0)
    u1 = pltpu.roll(u, 1, 0)
    u2 = pltpu.roll(u, 2, 0)
    head1 = jnp.where(row == 0, halo[HALO - 1:HALO, :], u1[:8])
    head2 = jnp.where(row == 0, halo[HALO - 2:HALO - 1, :], u2[:8])
    head2 = jnp.where(row == 1, halo[HALO - 1:HALO, :], head2)
    u1 = jnp.concatenate([head1, u1[8:]], 0)
    u2 = jnp.concatenate([head2, u2[8:]], 0)
    cw = cw_ref[...]
    conv = (cb_ref[...].astype(F32) * (cw[0:1] * u2 + cw[1:2] * u1 + cw[2:3] * u)).astype(BF16)
    rows = TM // OUT_SPLIT
    ys = []
    for r in range(OUT_SPLIT):
        sl = slice(r * rows, (r + 1) * rows)
        mix = jnp.concatenate([attn_ref[sl, :], conv[sl]], 1)
        ys.append(jnp.dot(mix, wout_ref[...], preferred_element_type=F32))
    for r in range(OUT_SPLIT):
        sl = slice(r * rows, (r + 1) * rows)
        o_ref[sl, :] = _layer_norm(ALPHA * x_ref[sl, :] + ys[r], g_ref[...], b_ref[...])


def _even_out(x, attn, hcat, conv_w, w_out, g, b, seq):
    t = x.shape[0]
    per_seq = seq // TM
    col0 = (2 * DIFF_QK + DIFF_WIDTH) // CONV_CH
    halo_blocks = TM // HALO

    def halo_spec(col):
        return pl.BlockSpec((HALO, CONV_CH),
                            lambda i: (jnp.maximum(i * halo_blocks - 1, 0), col))

    return pl.pallas_call(
        functools.partial(_even_out_kernel, per_seq=per_seq),
        grid=(t // TM,),
        in_specs=[pl.BlockSpec((TM, D_MODEL), lambda i: (i, 0)),
                  pl.BlockSpec((TM, DIFF_WIDTH), lambda i: (i, 0)),
                  pl.BlockSpec((TM, CONV_CH), lambda i: (i, col0)),
                  pl.BlockSpec((TM, CONV_CH), lambda i: (i, col0 + 1)),
                  pl.BlockSpec((TM, CONV_CH), lambda i: (i, col0 + 2)),
                  halo_spec(col0 + 1), halo_spec(col0 + 2),
                  _resident((3, CONV_CH)),
                  _resident((DIFF_WIDTH + CONV_CH, D_MODEL)),
                  _resident((1, D_MODEL)), _resident((1, D_MODEL))],
        out_specs=pl.BlockSpec((TM, D_MODEL), lambda i: (i, 0)),
        out_shape=jax.ShapeDtypeStruct((t, D_MODEL), F32),
        compiler_params=_params(1),
        name="even_out",
    )(x, attn, hcat, hcat, hcat, hcat, hcat, conv_w, w_out, g, b)


def _odd_proj_kernel(x_ref, w_ref, cos_ref, sin_ref, o_ref):
    xb = x_ref[...].astype(BF16)
    cos = cos_ref[...]
    sin = sin_ref[...]
    half = RET_DK // 2
    for n in range(ODD_IN // 512):
        r = jnp.dot(xb, w_ref[:, n * 512:(n + 1) * 512], preferred_element_type=F32)
        if n < 2 * RET_QK // 512:
            for hh in range(512 // RET_DK):
                x1 = r[:, hh * RET_DK:hh * RET_DK + half]
                x2 = r[:, hh * RET_DK + half:(hh + 1) * RET_DK]
                o1 = x1 * cos - x2 * sin
                o2 = x2 * cos + x1 * sin
                if n >= RET_QK // 512:
                    o1 = o1 * RET_DK ** -0.5
                    o2 = o2 * RET_DK ** -0.5
                base = n * 512 + hh * RET_DK
                o_ref[:, base:base + half] = o1.astype(BF16)
                o_ref[:, base + half:base + RET_DK] = o2.astype(BF16)
        else:
            o_ref[:, n * 512:(n + 1) * 512] = r.astype(BF16)


def _odd_proj(x, w, cos, sin, seq):
    t = x.shape[0]
    per_seq = seq // TM
    tbl = pl.BlockSpec((TM, LANES), lambda i: (i % per_seq, 0))
    return pl.pallas_call(
        _odd_proj_kernel,
        grid=(t // TM,),
        in_specs=[pl.BlockSpec((TM, D_MODEL), lambda i: (i, 0)),
                  _resident((D_MODEL, ODD_IN)), tbl, tbl],
        out_specs=pl.BlockSpec((TM, ODD_IN), lambda i: (i, 0)),
        out_shape=jax.ShapeDtypeStruct((t, ODD_IN), BF16),
        compiler_params=_params(1),
        name="odd_proj",
    )(x, w, cos, sin)


def _retention_kernel(cd_ref, q_ref, k_ref, v_ref, gate_ref, dmat_ref, qd_ref, kd_ref, ng_ref,
                      o_ref, *, seq):
    h = pl.program_id(1)
    cd = cd_ref[h]
    dmat = dmat_ref[...]
    qd = jnp.concatenate([qd_ref[...]] * (RET_DV // LANES), axis=1)
    kd = jnp.concatenate([kd_ref[...]] * (RET_DK // LANES), axis=1)
    ng = ng_ref[...]
    state = jnp.zeros((RET_DK, RET_DV), F32)
    for c in range(seq // RET_CHUNK):
        rows = pl.ds(c * RET_CHUNK, RET_CHUNK)
        qc = q_ref[rows, :]
        kc = k_ref[rows, :]
        vc = v_ref[rows, :]
        sc = lax.dot_general(qc, kc, (((1,), (1,)), ((), ())), preferred_element_type=F32) * dmat
        intra = jnp.dot(sc.astype(BF16), vc, preferred_element_type=F32)
        cross = jnp.dot(qc, state.astype(BF16), preferred_element_type=F32) * qd
        kdec = (kc.astype(F32) * kd).astype(BF16)
        state = state * cd + lax.dot_general(kdec, vc, (((0,), (0,)), ((), ())),
                                             preferred_element_type=F32)
        y = intra + cross
        mu = jnp.mean(y, -1, keepdims=True)
        d = y - mu
        var = jnp.mean(d * d, -1, keepdims=True)
        yn = d * lax.rsqrt(var + LN_EPS) * ng
        gt = gate_ref[rows, :].astype(F32)
        o_ref[rows, :] = (gt * jax.nn.sigmoid(gt) * yn).astype(BF16)


def _retention(hcat, tables, norm_g, batch, seq):
    cd, dmat, qd, kd = tables
    tbl = pl.BlockSpec((None, RET_CHUNK, RET_CHUNK), lambda b, h: (h, 0, 0))
    qk_blocks = RET_QK // RET_DK
    v0 = 2 * RET_QK // RET_DV
    g0 = v0 + RET_VW // RET_DV
    return pl.pallas_call(
        functools.partial(_retention_kernel, seq=seq),
        grid=(batch, RET_HEADS),
        in_specs=[pl.BlockSpec(memory_space=pltpu.SMEM),
                  pl.BlockSpec((seq, RET_DK), lambda b, h: (b, h)),
                  pl.BlockSpec((seq, RET_DK), lambda b, h: (b, qk_blocks + h)),
                  pl.BlockSpec((seq, RET_DV), lambda b, h: (b, v0 + h)),
                  pl.BlockSpec((seq, RET_DV), lambda b, h: (b, g0 + h)),
                  tbl, tbl, tbl,
                  pl.BlockSpec((1, RET_DV), lambda b, h: (0, h))],
        out_specs=pl.BlockSpec((seq, RET_DV), lambda b, h: (b, h)),
        out_shape=jax.ShapeDtypeStruct((batch * seq, RET_VW), BF16),
        compiler_params=_params(2),
        name="retention",
    )(cd, hcat, hcat, hcat, hcat, dmat, qd, kd, norm_g)


def _odd_out_kernel(x_ref, y_ref, wout_ref, g_ref, b_ref, o_ref):
    rows = TM // OUT_SPLIT
    ys = [jnp.dot(y_ref[r * rows:(r + 1) * rows, :], wout_ref[...],
                  preferred_element_type=F32) for r in range(OUT_SPLIT)]
    for r in range(OUT_SPLIT):
        sl = slice(r * rows, (r + 1) * rows)
        o_ref[sl, :] = _layer_norm(ALPHA * x_ref[sl, :] + ys[r], g_ref[...], b_ref[...])


def _odd_out(x, y, w_out, g, b):
    t = x.shape[0]
    return pl.pallas_call(
        _odd_out_kernel,
        grid=(t // TM,),
        in_specs=[pl.BlockSpec((TM, D_MODEL), lambda i: (i, 0)),
                  pl.BlockSpec((TM, RET_VW), lambda i: (i, 0)),
                  _resident((RET_VW, D_MODEL)),
                  _resident((1, D_MODEL)), _resident((1, D_MODEL))],
        out_specs=pl.BlockSpec((TM, D_MODEL), lambda i: (i, 0)),
        out_shape=jax.ShapeDtypeStruct((t, D_MODEL), F32),
        compiler_params=_params(1),
        name="odd_out",
    )(x, y, w_out, g, b)


def _even_rope_tables(seq):
    half = ROPE_DIMS // 2
    inv = jnp.exp(-math.log(ROPE_THETA) * jnp.arange(half, dtype=F32) * (2.0 / ROPE_DIMS))
    ang = jnp.arange(seq, dtype=F32)[:, None] * inv[None, :]
    cos, sin = jnp.cos(ang), jnp.sin(ang)
    pad = jnp.zeros((seq, DIFF_DH - ROPE_DIMS), F32)
    zeros = jnp.zeros((seq, half), F32)
    c = jnp.concatenate([cos, cos, pad + 1.0], -1)
    sa = jnp.concatenate([-sin, zeros, pad], -1)
    sb = jnp.concatenate([zeros, sin, pad], -1)
    reps = LANES // DIFF_DH
    return tuple(jnp.tile(t, (1, reps)) for t in (c, sa, sb))


def _odd_rope_tables(seq):
    half = RET_DK // 2
    inv = jnp.exp(-math.log(RET_ROPE_THETA) * jnp.arange(half, dtype=F32) * (2.0 / RET_DK))
    ang = jnp.arange(seq, dtype=F32)[:, None] * inv[None, :]
    return jnp.cos(ang), jnp.sin(ang)


def _retention_tables():
    log_g = jnp.log1p(-jnp.exp2(-5.0 - jnp.arange(RET_HEADS, dtype=F32)))
    idx = jnp.arange(RET_CHUNK, dtype=F32)
    rel = idx[:, None] - idx[None, :]
    dmat = jnp.where(rel >= 0, jnp.exp(log_g[:, None, None] * jnp.maximum(rel, 0.0)), 0.0)
    q_decay = jnp.exp(log_g[:, None] * (idx + 1.0))
    k_decay = jnp.exp(log_g[:, None] * (RET_CHUNK - 1.0 - idx))
    cd = jnp.exp(log_g * RET_CHUNK)
    rep = lambda t: jnp.broadcast_to(t[:, :, None], (RET_HEADS, RET_CHUNK, LANES))
    return cd, dmat, rep(q_decay), rep(k_decay)


def kernel(x, ln_g, ln_b, ffn_w_in, ffn_w_out, even_w_in, even_w_out, diff_lambda,
           diff_norm_g, conv_w, odd_w_in, odd_w_out, ret_norm_g):
    batch, seq, _ = x.shape
    assert seq % TM == 0 and seq % ATT_TQ == 0 and DEPTH == 2
    h = x.reshape(batch * seq, D_MODEL)
    row = lambda v: v.reshape(1, -1)
    norm = lambda i, k: (row(ln_g[i, k]), row(ln_b[i, k]))
    ffn = lambda i, k: [(ffn_w_in, (i, k)), (ffn_w_out, (i, k))]

    w_in, w_out = ffn_w_in[0, 0].astype(BF16), ffn_w_out[0, 0].astype(BF16)

    h, (w_in, w_out, e_in, e_out) = _ffn_ln(
        h, w_in, w_out, *norm(0, 0),
        casts=ffn(0, 1) + [(even_w_in, (0,)), (even_w_out, (0,))])
    lambda_init = 0.8 - 0.6 * math.exp(-0.3 * 0)
    hcat, next_ffn = _even_proj(h, e_in, *_even_rope_tables(seq), seq, casts=ffn(1, 0))
    attn = _diff_attention(hcat, diff_lambda[0], row(diff_norm_g[0]), batch, seq, lambda_init)
    h = _even_out(h, attn, hcat, conv_w[0], e_out, *norm(0, 1), seq)
    h, (o_in, o_out) = _ffn_ln(h, w_in, w_out, *norm(0, 2),
                               casts=[(odd_w_in, (0,)), (odd_w_out, (0,))])

    h, last_ffn = _ffn_ln(h, *next_ffn, *norm(1, 0), casts=ffn(1, 1))
    hcat = _odd_proj(h, o_in, *_odd_rope_tables(seq), seq)
    y = _retention(hcat, _retention_tables(), row(ret_norm_g[0]), batch, seq)
    h = _odd_out(h, y, o_out, *norm(1, 1))
    h, _ = _ffn_ln(h, *last_ffn, *norm(1, 2))
    return h.reshape(batch, seq, D_MODEL)
```

```python
import functools
import math

import jax
import jax.numpy as jnp
from jax import lax
from jax.experimental import pallas as pl
from jax.experimental.pallas import tpu as pltpu

F32 = jnp.float32
BF16 = jnp.bfloat16

D_MODEL = 1024
DEPTH = 2
D_FF = 2816
DIFF_HEADS = 4
DIFF_DH = 64
DIFF_DV = 128
DIFF_QK = 512
DIFF_WIDTH = 512
ROPE_THETA = 500000.0
ROPE_DIMS = 16
CONV_CH = 512
EVEN_IN = 3072
RET_HEADS = 4
RET_DK = 256
RET_DV = 512
RET_QK = 1024
RET_VW = 2048
RET_ROPE_THETA = 10000.0
ODD_IN = 6144
ALPHA = (2.0 * DEPTH) ** 0.25
LN_EPS = 1e-5
NEG_INF = -1e30

LANES = 128
VMEM_LIMIT = 56 * 1024 * 1024

TM = 1024
FF_CHUNK = 256
OUT_SPLIT = 4
RET_BLOCK = 256
ATT_TQ = 256
ATT_STRIP = 16


def _layer_norm(z, g, b):
    mu = jnp.mean(z, -1, keepdims=True)
    d = z - mu
    var = jnp.mean(d * d, -1, keepdims=True)
    return d * lax.rsqrt(var + LN_EPS) * g + b


def _params(n_axes, vmem=VMEM_LIMIT):
    return pltpu.CompilerParams(dimension_semantics=("arbitrary",) * n_axes,
                                vmem_limit_bytes=vmem)


def _resident(shape):
    nd = len(shape)
    return pl.BlockSpec(shape, lambda *_: (0,) * nd, pipeline_mode=pl.Buffered(1))


def _cast_specs(casts, steps):
    in_specs, out_specs, out_shapes = [], [], []
    for arr, prefix in casts:
        r, c = arr.shape[-2:]
        assert arr.ndim == len(prefix) + 2 and r % (16 * steps) == 0
        rows = r // steps
        in_specs.append(pl.BlockSpec((None,) * len(prefix) + (rows, c),
                                     lambda i, p=prefix: p + (i, 0)))
        out_specs.append(pl.BlockSpec((rows, c), lambda i: (i, 0)))
        out_shapes.append(jax.ShapeDtypeStruct((r, c), BF16))
    return in_specs, out_specs, out_shapes


def _do_casts(src_refs, dst_refs):
    for src, dst in zip(src_refs, dst_refs):
        dst[...] = src[...].astype(BF16)


def _ffn_kernel(x_ref, win_ref, wout_ref, g_ref, b_ref, *rest, n_cast):
    cast_src, o_ref, cast_dst, act_ref = (rest[:n_cast], rest[n_cast],
                                          rest[n_cast + 1:2 * n_cast + 1], rest[-1])
    _do_casts(cast_src, cast_dst)
    x = x_ref[...]
    xb = x.astype(BF16)
    for c in range(D_FF // FF_CHUNK):
        lo = c * FF_CHUNK
        gate = jnp.dot(xb, win_ref[:, lo:lo + FF_CHUNK], preferred_element_type=F32)
        up = jnp.dot(xb, win_ref[:, D_FF + lo:D_FF + lo + FF_CHUNK], preferred_element_type=F32)
        act_ref[:, lo:lo + FF_CHUNK] = (gate * jax.nn.sigmoid(gate) * up).astype(BF16)
    rows = TM // OUT_SPLIT
    ys = [jnp.dot(act_ref[r * rows:(r + 1) * rows, :], wout_ref[...],
                  preferred_element_type=F32) for r in range(OUT_SPLIT)]
    for r in range(OUT_SPLIT):
        sl = slice(r * rows, (r + 1) * rows)
        o_ref[sl, :] = _layer_norm(ALPHA * x_ref[sl, :] + 0.5 * ys[r], g_ref[...], b_ref[...])


def _ffn_ln(x, w_in, w_out, g, b, casts=()):
    t = x.shape[0]
    steps = t // TM
    cast_in, cast_out, cast_shapes = _cast_specs(casts, steps)
    out = pl.pallas_call(
        functools.partial(_ffn_kernel, n_cast=len(casts)),
        grid=(steps,),
        in_specs=[pl.BlockSpec((TM, D_MODEL), lambda i: (i, 0)),
                  _resident((D_MODEL, 2 * D_FF)),
                  _resident((D_FF, D_MODEL)),
                  _resident((1, D_MODEL)),
                  _resident((1, D_MODEL))] + cast_in,
        out_specs=[pl.BlockSpec((TM, D_MODEL), lambda i: (i, 0))] + cast_out,
        out_shape=[jax.ShapeDtypeStruct((t, D_MODEL), F32)] + cast_shapes,
        scratch_shapes=[pltpu.VMEM((TM, D_FF), BF16)],
        compiler_params=_params(1),
        name="ffn_ln",
    )(x, w_in, w_out, g, b, *[arr for arr, _ in casts])
    return out[0], out[1:]


def _even_proj_kernel(x_ref, w_ref, c_ref, sa_ref, sb_ref, *rest, n_cast):
    cast_src, o_ref, cast_dst = rest[:n_cast], rest[n_cast], rest[n_cast + 1:]
    _do_casts(cast_src, cast_dst)
    xb = x_ref[...].astype(BF16)
    c = c_ref[...]
    sa = sa_ref[...]
    sb = sb_ref[...]
    for n in range(EVEN_IN // 512):
        r = jnp.dot(xb, w_ref[:, n * 512:(n + 1) * 512], preferred_element_type=F32)
        if n < 2:
            scale = DIFF_DH ** -0.5 if n == 0 else 1.0
            for j in range(512 // LANES):
                blk = r[:, j * LANES:(j + 1) * LANES]
                rot = (blk * c + pltpu.roll(blk, LANES - 8, 1) * sa
                       + pltpu.roll(blk, 8, 1) * sb)
                if n == 0:
                    rot = rot * scale
                o_ref[:, n * 512 + j * LANES:n * 512 + (j + 1) * LANES] = rot.astype(BF16)
        else:
            o_ref[:, n * 512:(n + 1) * 512] = r.astype(BF16)


def _even_proj(x, w, c, sa, sb, seq, casts=()):
    t = x.shape[0]
    steps = t // TM
    per_seq = seq // TM
    tbl = pl.BlockSpec((TM, LANES), lambda i: (i % per_seq, 0))
    cast_in, cast_out, cast_shapes = _cast_specs(casts, steps)
    out = pl.pallas_call(
        functools.partial(_even_proj_kernel, n_cast=len(casts)),
        grid=(steps,),
        in_specs=[pl.BlockSpec((TM, D_MODEL), lambda i: (i, 0)),
                  _resident((D_MODEL, EVEN_IN)), tbl, tbl, tbl] + cast_in,
        out_specs=[pl.BlockSpec((TM, EVEN_IN), lambda i: (i, 0))] + cast_out,
        out_shape=[jax.ShapeDtypeStruct((t, EVEN_IN), BF16)] + cast_shapes,
        compiler_params=_params(1),
        name="even_proj",
    )(x, w, c, sa, sb, *[arr for arr, _ in casts])
    return out[0], out[1:]


def _attn_kernel(q_ref, k_ref, v_ref, lv_ref, g_ref, o_ref, s_ref, p_ref, *, lambda_init, seq):
    lv = lv_ref[...]
    lam = (jnp.exp(jnp.sum(lv[0:1] * lv[1:2], -1, keepdims=True))
           - jnp.exp(jnp.sum(lv[2:3] * lv[3:4], -1, keepdims=True)) + lambda_init)
    lane = lax.broadcasted_iota(jnp.int32, (ATT_TQ, LANES), 1)
    row = lax.broadcasted_iota(jnp.int32, (ATT_TQ, ATT_TQ), 0)
    col = lax.broadcasted_iota(jnp.int32, (ATT_TQ, ATT_TQ), 1)
    causal = col <= row
    gain = g_ref[...]
    nt = (((1,), (1,)), ((), ()))
    n_tiles = seq // ATT_TQ
    order = [t for pair in zip(range(n_tiles - 1, -1, -1), range(n_tiles)) for t in pair][:n_tiles]
    for pos, i in enumerate(order):
        buf = pos % 2
        lo = i * ATT_TQ
        q = q_ref[lo:lo + ATT_TQ, :]
        zero = jnp.zeros_like(q)
        qs = (jnp.where(lane < DIFF_DH, q, zero), jnp.where(lane >= DIFF_DH, q, zero))
        hi = lo + ATT_TQ
        l = []
        for c, qc in enumerate(qs):
            s = lax.dot_general(qc, k_ref[:hi, :], nt, preferred_element_type=F32)
            sc_ref = s_ref.at[buf, c]
            if i > 0:
                sc_ref[:, :lo] = s[:, :lo]
            sc_ref[:, lo:hi] = jnp.where(causal, s[:, lo:], NEG_INF)
            m = jnp.max(sc_ref[:, :hi], -1, keepdims=True)
            lsum = []
            for r in range(ATT_TQ // ATT_STRIP):
                sl = slice(r * ATT_STRIP, (r + 1) * ATT_STRIP)
                p = jnp.exp(sc_ref[sl, :hi] - m[sl])
                lsum.append(jnp.sum(p, -1, keepdims=True))
                p_ref[buf, c * ATT_TQ + r * ATT_STRIP:c * ATT_TQ + (r + 1) * ATT_STRIP, :hi] = (
                    p.astype(BF16))
            l.append(jnp.concatenate(lsum, 0))
        acc = jnp.dot(p_ref[buf, :, :hi], v_ref[:hi, :], preferred_element_type=F32)
        acc = acc[:ATT_TQ] * (1.0 / l[0]) - acc[ATT_TQ:] * (lam / l[1])
        y = acc * lax.rsqrt(jnp.mean(acc * acc, -1, keepdims=True) + LN_EPS)
        o_ref[lo:lo + ATT_TQ, :] = (y * gain * (1.0 - lambda_init)).astype(BF16)


def _diff_attention(hcat, lam_vecs, norm_g, batch, seq, lambda_init):
    return pl.pallas_call(
        functools.partial(_attn_kernel, lambda_init=lambda_init, seq=seq),
        grid=(batch, DIFF_HEADS),
        in_specs=[pl.BlockSpec((seq, LANES), lambda b, h: (b, h)),
                  pl.BlockSpec((seq, LANES), lambda b, h: (b, DIFF_QK // LANES + h)),
                  pl.BlockSpec((seq, LANES), lambda b, h: (b, 2 * DIFF_QK // LANES + h)),
                  _resident((4, DIFF_DH)),
                  _resident((1, DIFF_DV))],
        out_specs=pl.BlockSpec((seq, DIFF_DV), lambda b, h: (b, h)),
        out_shape=jax.ShapeDtypeStruct((batch * seq, DIFF_WIDTH), BF16),
        scratch_shapes=[pltpu.VMEM((2, 2, ATT_TQ, seq), F32),
                        pltpu.VMEM((2, 2 * ATT_TQ, seq), BF16)],
        compiler_params=_params(2),
        name="diff_attn",
    )(hcat, hcat, hcat, lam_vecs, norm_g)


HALO = 16


def _even_out_kernel(x_ref, attn_ref, cb_ref, cc_ref, cx_ref, hc_ref, hx_ref, cw_ref,
                     wout_ref, g_ref, b_ref, o_ref, *, per_seq):
    i = pl.program_id(0)
    u = cc_ref[...].astype(F32) * cx_ref[...].astype(F32)
    halo = hc_ref[...].astype(F32) * hx_ref[...].astype(F32)
    halo = jnp.where(i % per_seq == 0, jnp.zeros_like(halo), halo)
    row = lax.broadcasted_iota(jnp.int32, (8, CONV_CH), 0)
    u1 = pltpu.roll(u, 1, 0)
    u2 = pltpu.roll(u, 2, 0)
    head1 = jnp.where(row == 0, halo[HALO - 1:HALO, :], u1[:8])
    head2 = jnp.where(row == 0, halo[HALO - 2:HALO - 1, :], u2[:8])
    head2 = jnp.where(row == 1, halo[HALO - 1:HALO, :], head2)
    u1 = jnp.concatenate([head1, u1[8:]], 0)
    u2 = jnp.concatenate([head2, u2[8:]], 0)
    cw = cw_ref[...]
    conv = (cb_ref[...].astype(F32) * (cw[0:1] * u2 + cw[1:2] * u1 + cw[2:3] * u)).astype(BF16)
    rows = TM // OUT_SPLIT
    ys = []
    for r in range(OUT_SPLIT):
        sl = slice(r * rows, (r + 1) * rows)
        mix = jnp.concatenate([attn_ref[sl, :], conv[sl]], 1)
        ys.append(jnp.dot(mix, wout_ref[...], preferred_element_type=F32))
    for r in range(OUT_SPLIT):
        sl = slice(r * rows, (r + 1) * rows)
        o_ref[sl, :] = _layer_norm(ALPHA * x_ref[sl, :] + ys[r], g_ref[...], b_ref[...])


def _even_out(x, attn, hcat, conv_w, w_out, g, b, seq):
    t = x.shape[0]
    per_seq = seq // TM
    col0 = (2 * DIFF_QK + DIFF_WIDTH) // CONV_CH
    halo_blocks = TM // HALO

    def halo_spec(col):
        return pl.BlockSpec((HALO, CONV_CH),
                            lambda i: (jnp.maximum(i * halo_blocks - 1, 0), col))

    return pl.pallas_call(
        functools.partial(_even_out_kernel, per_seq=per_seq),
        grid=(t // TM,),
        in_specs=[pl.BlockSpec((TM, D_MODEL), lambda i: (i, 0)),
                  pl.BlockSpec((TM, DIFF_WIDTH), lambda i: (i, 0)),
                  pl.BlockSpec((TM, CONV_CH), lambda i: (i, col0)),
                  pl.BlockSpec((TM, CONV_CH), lambda i: (i, col0 + 1)),
                  pl.BlockSpec((TM, CONV_CH), lambda i: (i, col0 + 2)),
                  halo_spec(col0 + 1), halo_spec(col0 + 2),
                  _resident((3, CONV_CH)),
                  _resident((DIFF_WIDTH + CONV_CH, D_MODEL)),
                  _resident((1, D_MODEL)), _resident((1, D_MODEL))],
        out_specs=pl.BlockSpec((TM, D_MODEL), lambda i: (i, 0)),
        out_shape=jax.ShapeDtypeStruct((t, D_MODEL), F32),
        compiler_params=_params(1),
        name="even_out",
    )(x, attn, hcat, hcat, hcat, hcat, hcat, conv_w, w_out, g, b)


def _odd_proj_kernel(x_ref, w_ref, cos_ref, sin_ref, ng_ref, o_ref):
    xb = x_ref[...].astype(BF16)
    cos = cos_ref[...]
    sin = sin_ref[...]
    half = RET_DK // 2
    gate0 = (2 * RET_QK + RET_VW) // 512
    for n in list(range(gate0, ODD_IN // 512)) + list(range(gate0)):
        r = jnp.dot(xb, w_ref[:, n * 512:(n + 1) * 512], preferred_element_type=F32)
        if n >= gate0:
            gain = ng_ref[:, (n - gate0) * 512:(n - gate0 + 1) * 512]
            o_ref[:, n * 512:(n + 1) * 512] = (r * jax.nn.sigmoid(r) * gain).astype(BF16)
        elif n < 2 * RET_QK // 512:
            for hh in range(512 // RET_DK):
                x1 = r[:, hh * RET_DK:hh * RET_DK + half]
                x2 = r[:, hh * RET_DK + half:(hh + 1) * RET_DK]
                o1 = x1 * cos - x2 * sin
                o2 = x2 * cos + x1 * sin
                if n >= RET_QK // 512:
                    o1 = o1 * RET_DK ** -0.5
                    o2 = o2 * RET_DK ** -0.5
                base = n * 512 + hh * RET_DK
                o_ref[:, base:base + half] = o1.astype(BF16)
                o_ref[:, base + half:base + RET_DK] = o2.astype(BF16)
        else:
            o_ref[:, n * 512:(n + 1) * 512] = r.astype(BF16)


def _odd_proj(x, w, cos, sin, norm_g, seq):
    t = x.shape[0]
    per_seq = seq // TM
    tbl = pl.BlockSpec((TM, LANES), lambda i: (i % per_seq, 0))
    return pl.pallas_call(
        _odd_proj_kernel,
        grid=(t // TM,),
        in_specs=[pl.BlockSpec((TM, D_MODEL), lambda i: (i, 0)),
                  _resident((D_MODEL, ODD_IN)), tbl, tbl, _resident((1, RET_VW))],
        out_specs=pl.BlockSpec((TM, ODD_IN), lambda i: (i, 0)),
        out_shape=jax.ShapeDtypeStruct((t, ODD_IN), BF16),
        compiler_params=_params(1),
        name="odd_proj",
    )(x, w, cos, sin, norm_g)


def _retention_kernel(cd_ref, q_ref, k_ref, v_ref, gate_ref, dmat_ref, qd_ref, kd_ref,
                      o_ref, *, seq):
    h = pl.program_id(1)
    cd = cd_ref[h]
    dmat = dmat_ref[...]
    qd = jnp.concatenate([qd_ref[...]] * (RET_DV // LANES), axis=1)
    kd = jnp.concatenate([kd_ref[...]] * (RET_DK // LANES), axis=1)
    state = jnp.zeros((RET_DK, RET_DV), F32)
    for c in range(seq // RET_BLOCK):
        rows = pl.ds(c * RET_BLOCK, RET_BLOCK)
        qc = q_ref[rows, :]
        kc = k_ref[rows, :]
        vc = v_ref[rows, :]
        sc = lax.dot_general(qc, kc, (((1,), (1,)), ((), ())), preferred_element_type=F32) * dmat
        y = jnp.dot(sc.astype(BF16), vc, preferred_element_type=F32)
        y = y + jnp.dot(qc, state.astype(BF16), preferred_element_type=F32) * qd
        kdec = (kc.astype(F32) * kd).astype(BF16)
        state = state * cd + lax.dot_general(kdec, vc, (((0,), (0,)), ((), ())),
                                             preferred_element_type=F32)
        mu = jnp.mean(y, -1, keepdims=True)
        d = y - mu
        var = jnp.mean(d * d, -1, keepdims=True)
        yn = d * lax.rsqrt(var + LN_EPS)
        o_ref[rows, :] = (gate_ref[rows, :].astype(F32) * yn).astype(BF16)


def _retention(hcat, tables, batch, seq):
    cd, dmat, qd, kd = tables
    tbl = pl.BlockSpec((None, RET_BLOCK, LANES), lambda b, h: (h, 0, 0))
    qk_blocks = RET_QK // RET_DK
    v0 = 2 * RET_QK // RET_DV
    g0 = v0 + RET_VW // RET_DV
    return pl.pallas_call(
        functools.partial(_retention_kernel, seq=seq),
        grid=(batch, RET_HEADS),
        in_specs=[pl.BlockSpec(memory_space=pltpu.SMEM),
                  pl.BlockSpec((seq, RET_DK), lambda b, h: (b, h)),
                  pl.BlockSpec((seq, RET_DK), lambda b, h: (b, qk_blocks + h)),
                  pl.BlockSpec((seq, RET_DV), lambda b, h: (b, v0 + h)),
                  pl.BlockSpec((seq, RET_DV), lambda b, h: (b, g0 + h)),
                  pl.BlockSpec((None, RET_BLOCK, RET_BLOCK), lambda b, h: (h, 0, 0)),
                  tbl, tbl],
        out_specs=pl.BlockSpec((seq, RET_DV), lambda b, h: (b, h)),
        out_shape=jax.ShapeDtypeStruct((batch * seq, RET_VW), BF16),
        compiler_params=_params(2),
        name="retention",
    )(cd, hcat, hcat, hcat, hcat, dmat, qd, kd)


def _odd_out_kernel(x_ref, y_ref, wout_ref, g_ref, b_ref, o_ref):
    rows = TM // OUT_SPLIT
    ys = [jnp.dot(y_ref[r * rows:(r + 1) * rows, :], wout_ref[...],
                  preferred_element_type=F32) for r in range(OUT_SPLIT)]
    for r in range(OUT_SPLIT):
        sl = slice(r * rows, (r + 1) * rows)
        o_ref[sl, :] = _layer_norm(ALPHA * x_ref[sl, :] + ys[r], g_ref[...], b_ref[...])


def _odd_out(x, y, w_out, g, b):
    t = x.shape[0]
    return pl.pallas_call(
        _odd_out_kernel,
        grid=(t // TM,),
        in_specs=[pl.BlockSpec((TM, D_MODEL), lambda i: (i, 0)),
                  pl.BlockSpec((TM, RET_VW), lambda i: (i, 0)),
                  _resident((RET_VW, D_MODEL)),
                  _resident((1, D_MODEL)), _resident((1, D_MODEL))],
        out_specs=pl.BlockSpec((TM, D_MODEL), lambda i: (i, 0)),
        out_shape=jax.ShapeDtypeStruct((t, D_MODEL), F32),
        compiler_params=_params(1),
        name="odd_out",
    )(x, y, w_out, g, b)


def _even_rope_tables(seq):
    half = ROPE_DIMS // 2
    inv = jnp.exp(-math.log(ROPE_THETA) * jnp.arange(half, dtype=F32) * (2.0 / ROPE_DIMS))
    ang = jnp.arange(seq, dtype=F32)[:, None] * inv[None, :]
    cos, sin = jnp.cos(ang), jnp.sin(ang)
    pad = jnp.zeros((seq, DIFF_DH - ROPE_DIMS), F32)
    zeros = jnp.zeros((seq, half), F32)
    c = jnp.concatenate([cos, cos, pad + 1.0], -1)
    sa = jnp.concatenate([-sin, zeros, pad], -1)
    sb = jnp.concatenate([zeros, sin, pad], -1)
    reps = LANES // DIFF_DH
    return tuple(jnp.tile(t, (1, reps)) for t in (c, sa, sb))


def _odd_rope_tables(seq):
    half = RET_DK // 2
    inv = jnp.exp(-math.log(RET_ROPE_THETA) * jnp.arange(half, dtype=F32) * (2.0 / RET_DK))
    ang = jnp.arange(seq, dtype=F32)[:, None] * inv[None, :]
    return jnp.cos(ang), jnp.sin(ang)


def _retention_tables():
    log_g = jnp.log1p(-jnp.exp2(-5.0 - jnp.arange(RET_HEADS, dtype=F32)))
    idx = jnp.arange(RET_BLOCK, dtype=F32)
    rel = idx[:, None] - idx[None, :]
    dmat = jnp.where(rel >= 0, jnp.exp(log_g[:, None, None] * jnp.maximum(rel, 0.0)), 0.0)
    q_decay = jnp.exp(log_g[:, None] * (idx + 1.0))
    k_decay = jnp.exp(log_g[:, None] * (RET_BLOCK - 1.0 - idx))
    cd = jnp.exp(log_g * RET_BLOCK)
    rep = lambda t: jnp.broadcast_to(t[:, :, None], (RET_HEADS, RET_BLOCK, LANES))
    return cd, dmat, rep(q_decay), rep(k_decay)


def kernel(x, ln_g, ln_b, ffn_w_in, ffn_w_out, even_w_in, even_w_out, diff_lambda,
           diff_norm_g, conv_w, odd_w_in, odd_w_out, ret_norm_g):
    batch, seq, _ = x.shape
    assert seq % TM == 0 and seq % ATT_TQ == 0 and DEPTH == 2
    h = x.reshape(batch * seq, D_MODEL)
    row = lambda v: v.reshape(1, -1)
    norm = lambda i, k: (row(ln_g[i, k]), row(ln_b[i, k]))
    ffn = lambda i, k: [(ffn_w_in, (i, k)), (ffn_w_out, (i, k))]

    w_in, w_out = ffn_w_in[0, 0].astype(BF16), ffn_w_out[0, 0].astype(BF16)

    h, (w_in, w_out, e_in, e_out) = _ffn_ln(
        h, w_in, w_out, *norm(0, 0),
        casts=ffn(0, 1) + [(even_w_in, (0,)), (even_w_out, (0,))])
    lambda_init = 0.8 - 0.6 * math.exp(-0.3 * 0)
    hcat, next_ffn = _even_proj(h, e_in, *_even_rope_tables(seq), seq, casts=ffn(1, 0))
    attn = _diff_attention(hcat, diff_lambda[0], row(diff_norm_g[0]), batch, seq, lambda_init)
    h = _even_out(h, attn, hcat, conv_w[0], e_out, *norm(0, 1), seq)
    h, (o_in, o_out) = _ffn_ln(h, w_in, w_out, *norm(0, 2),
                               casts=[(odd_w_in, (0,)), (odd_w_out, (0,))])

    h, last_ffn = _ffn_ln(h, *next_ffn, *norm(1, 0), casts=ffn(1, 1))
    hcat = _odd_proj(h, o_in, *_odd_rope_tables(seq), row(ret_norm_g[0]), seq)
    y = _retention(hcat, _retention_tables(), batch, seq)
    h = _odd_out(h, y, o_out, *norm(1, 1))
    h, _ = _ffn_ln(h, *last_ffn, *norm(1, 2))
    return h.reshape(batch, seq, D_MODEL)
```

```python
import functools
import math

import jax
import jax.numpy as jnp
import numpy as np
from jax import lax
from jax.experimental import pallas as pl
from jax.experimental.pallas import tpu as pltpu

F32 = jnp.float32
BF16 = jnp.bfloat16

D_MODEL = 1024
DEPTH = 2
D_FF = 2816
DIFF_HEADS = 4
DIFF_DH = 64
DIFF_DV = 128
DIFF_QK = 512
DIFF_WIDTH = 512
ROPE_THETA = 500000.0
ROPE_DIMS = 16
CONV_CH = 512
EVEN_IN = 3072
RET_HEADS = 4
RET_DK = 256
RET_DV = 512
RET_QK = 1024
RET_VW = 2048
RET_ROPE_THETA = 10000.0
ODD_IN = 6144
ALPHA = (2.0 * DEPTH) ** 0.25
LN_EPS = 1e-5
NEG_INF = -1e30
LOG2E = math.log2(math.e)

LANES = 128
VMEM_LIMIT = 56 * 1024 * 1024

TM = 1024
FF_CHUNK = 256
OUT_SPLIT = 4
FFN_ROW_GROUPS = (256, 256, 256, 256)
RET_BLOCK = 256
ATT_TQ = 256
ATT_STRIP = 16
ATT_HEADS_PER_STEP = 2
RET_HEADS_PER_STEP = 2


def _layer_norm(z, g, b):
    mu = jnp.mean(z, -1, keepdims=True)
    d = z - mu
    var = jnp.mean(d * d, -1, keepdims=True)
    return d * lax.rsqrt(var + LN_EPS) * g + b


def _params(n_axes, vmem=VMEM_LIMIT):
    return pltpu.CompilerParams(dimension_semantics=("arbitrary",) * n_axes,
                                vmem_limit_bytes=vmem)


def _resident(shape):
    nd = len(shape)
    return pl.BlockSpec(shape, lambda *_: (0,) * nd, pipeline_mode=pl.Buffered(1))


def _cast_specs(casts, steps):
    in_specs, out_specs, out_shapes = [], [], []
    for arr, prefix in casts:
        r, c = arr.shape[-2:]
        assert arr.ndim == len(prefix) + 2
        rows = min(d for d in range(16, r + 1, 16) if r % d == 0 and d * steps >= r)
        last = r // rows - 1
        in_specs.append(pl.BlockSpec((None,) * len(prefix) + (rows, c),
                                     lambda i, p=prefix, n=last: p + (jnp.minimum(i, n), 0)))
        out_specs.append(pl.BlockSpec((rows, c), lambda i, n=last: (jnp.minimum(i, n), 0)))
        out_shapes.append(jax.ShapeDtypeStruct((r, c), BF16))
    return in_specs, out_specs, out_shapes


def _do_casts(src_refs, dst_refs):
    for src, dst in zip(src_refs, dst_refs):
        dst[...] = src[...].astype(BF16)


def _ffn_kernel(x_ref, win_ref, wout_ref, g_ref, b_ref, *rest, n_cast):
    cast_src, o_ref, cast_dst, act_ref = (rest[:n_cast], rest[n_cast],
                                          rest[n_cast + 1:2 * n_cast + 1], rest[-1])
    _do_casts(cast_src, cast_dst)
    x = x_ref[...]
    xb = x.astype(BF16)
    for c in range(D_FF // FF_CHUNK):
        lo = c * FF_CHUNK
        gate = jnp.dot(xb, win_ref[:, lo:lo + FF_CHUNK], preferred_element_type=F32)
        up = jnp.dot(xb, win_ref[:, D_FF + lo:D_FF + lo + FF_CHUNK], preferred_element_type=F32)
        act_ref[:, lo:lo + FF_CHUNK] = (gate * jax.nn.sigmoid(gate) * up).astype(BF16)
    bounds = [sum(FFN_ROW_GROUPS[:r]) for r in range(len(FFN_ROW_GROUPS) + 1)]
    groups = [slice(lo, hi) for lo, hi in zip(bounds[:-1], bounds[1:])]
    ys = [jnp.dot(act_ref[sl, :], wout_ref[...], preferred_element_type=F32) for sl in groups]
    for sl, y in zip(groups, ys):
        o_ref[sl, :] = _layer_norm(ALPHA * x_ref[sl, :] + 0.5 * y, g_ref[...], b_ref[...])


def _ffn_ln(x, w_in, w_out, g, b, casts=()):
    t = x.shape[0]
    steps = t // TM
    cast_in, cast_out, cast_shapes = _cast_specs(casts, steps)
    out = pl.pallas_call(
        functools.partial(_ffn_kernel, n_cast=len(casts)),
        grid=(steps,),
        in_specs=[pl.BlockSpec((TM, D_MODEL), lambda i: (i, 0)),
                  _resident((D_MODEL, 2 * D_FF)),
                  _resident((D_FF, D_MODEL)),
                  _resident((1, D_MODEL)),
                  _resident((1, D_MODEL))] + cast_in,
        out_specs=[pl.BlockSpec((TM, D_MODEL), lambda i: (i, 0))] + cast_out,
        out_shape=[jax.ShapeDtypeStruct((t, D_MODEL), F32)] + cast_shapes,
        scratch_shapes=[pltpu.VMEM((TM, D_FF), BF16)],
        compiler_params=_params(1),
        name="ffn_ln",
    )(x, w_in, w_out, g, b, *[arr for arr, _ in casts])
    return out[0], out[1:]


def _even_proj_kernel(x_ref, w_ref, c_ref, sa_ref, sb_ref, *rest, n_cast):
    cast_src, o_ref, cast_dst = rest[:n_cast], rest[n_cast], rest[n_cast + 1:]
    _do_casts(cast_src, cast_dst)
    xb = x_ref[...].astype(BF16)
    c = c_ref[...]
    sa = sa_ref[...]
    sb = sb_ref[...]
    for n in range(EVEN_IN // 512):
        r = jnp.dot(xb, w_ref[:, n * 512:(n + 1) * 512], preferred_element_type=F32)
        if n < 2:
            scale = DIFF_DH ** -0.5 * LOG2E if n == 0 else 1.0
            for j in range(512 // LANES):
                blk = r[:, j * LANES:(j + 1) * LANES]
                rot = (blk * c + pltpu.roll(blk, LANES - 8, 1) * sa
                       + pltpu.roll(blk, 8, 1) * sb)
                if n == 0:
                    rot = rot * scale
                o_ref[:, n * 512 + j * LANES:n * 512 + (j + 1) * LANES] = rot.astype(BF16)
        else:
            o_ref[:, n * 512:(n + 1) * 512] = r.astype(BF16)


def _even_proj(x, w, c, sa, sb, seq, casts=()):
    t = x.shape[0]
    steps = t // TM
    per_seq = seq // TM
    tbl = pl.BlockSpec((TM, LANES), lambda i: (i % per_seq, 0))
    cast_in, cast_out, cast_shapes = _cast_specs(casts, steps)
    out = pl.pallas_call(
        functools.partial(_even_proj_kernel, n_cast=len(casts)),
        grid=(steps,),
        in_specs=[pl.BlockSpec((TM, D_MODEL), lambda i: (i, 0)),
                  _resident((D_MODEL, EVEN_IN)), tbl, tbl, tbl] + cast_in,
        out_specs=[pl.BlockSpec((TM, EVEN_IN), lambda i: (i, 0))] + cast_out,
        out_shape=[jax.ShapeDtypeStruct((t, EVEN_IN), BF16)] + cast_shapes,
        compiler_params=_params(1),
        name="even_proj",
    )(x, w, c, sa, sb, *[arr for arr, _ in casts])
    return out[0], out[1:]


def _attn_kernel(q_ref, k_ref, v_ref, lv_ref, g_ref, o_ref, s_ref, p_ref, *, lambda_init, seq):
    lv = lv_ref[...]
    lam = (jnp.exp(jnp.sum(lv[0:1] * lv[1:2], -1, keepdims=True))
           - jnp.exp(jnp.sum(lv[2:3] * lv[3:4], -1, keepdims=True)) + lambda_init)
    lane = lax.broadcasted_iota(jnp.int32, (ATT_TQ, LANES), 1)
    row = lax.broadcasted_iota(jnp.int32, (ATT_TQ, ATT_TQ), 0)
    col = lax.broadcasted_iota(jnp.int32, (ATT_TQ, ATT_TQ), 1)
    causal = col <= row
    gain = g_ref[...]
    nt = (((1,), (1,)), ((), ()))
    n_tiles = seq // ATT_TQ
    order = [t for pair in zip(range(n_tiles - 1, -1, -1), range(n_tiles)) for t in pair][:n_tiles]
    for pos, i in enumerate(order):
        for j in range(ATT_HEADS_PER_STEP):
            head = slice(j * LANES, (j + 1) * LANES)
            buf = pos % 2
            lo = i * ATT_TQ
            q = q_ref[lo:lo + ATT_TQ, head]
            zero = jnp.zeros_like(q)
            qs = (jnp.where(lane < DIFF_DH, q, zero), jnp.where(lane >= DIFF_DH, q, zero))
            hi = lo + ATT_TQ
            l = []
            for c, qc in enumerate(qs):
                s = lax.dot_general(qc, k_ref[:hi, head], nt, preferred_element_type=F32)
                sc_ref = s_ref.at[j, buf, c]
                if i > 0:
                    sc_ref[:, :lo] = s[:, :lo]
                sc_ref[:, lo:hi] = jnp.where(causal, s[:, lo:], NEG_INF)
                m = jnp.max(sc_ref[:, :hi], -1, keepdims=True)
                lsum = []
                for r in range(ATT_TQ // ATT_STRIP):
                    sl = slice(r * ATT_STRIP, (r + 1) * ATT_STRIP)
                    p = jnp.exp2(sc_ref[sl, :hi] - m[sl])
                    lsum.append(jnp.sum(p, -1, keepdims=True))
                    base = c * ATT_TQ + r * ATT_STRIP
                    p_ref[j, buf, base:base + ATT_STRIP, :hi] = p.astype(BF16)
                l.append(jnp.concatenate(lsum, 0))
            acc = jnp.dot(p_ref[j, buf, :, :hi], v_ref[:hi, head], preferred_element_type=F32)
            acc = acc[:ATT_TQ] * (1.0 / l[0]) - acc[ATT_TQ:] * (lam / l[1])
            y = acc * lax.rsqrt(jnp.mean(acc * acc, -1, keepdims=True) + LN_EPS)
            o_ref[lo:lo + ATT_TQ, head] = (y * gain * (1.0 - lambda_init)).astype(BF16)


def _diff_attention(hcat, lam_vecs, norm_g, batch, seq, lambda_init):
    hp = ATT_HEADS_PER_STEP
    width = hp * LANES
    return pl.pallas_call(
        functools.partial(_attn_kernel, lambda_init=lambda_init, seq=seq),
        grid=(batch, DIFF_HEADS // hp),
        in_specs=[pl.BlockSpec((seq, width), lambda b, h: (b, h)),
                  pl.BlockSpec((seq, width), lambda b, h: (b, DIFF_QK // width + h)),
                  pl.BlockSpec((seq, width), lambda b, h: (b, 2 * DIFF_QK // width + h)),
                  _resident((4, DIFF_DH)),
                  _resident((1, DIFF_DV))],
        out_specs=pl.BlockSpec((seq, width), lambda b, h: (b, h)),
        out_shape=jax.ShapeDtypeStruct((batch * seq, DIFF_WIDTH), BF16),
        scratch_shapes=[pltpu.VMEM((hp, 2, 2, ATT_TQ, seq), F32),
                        pltpu.VMEM((hp, 2, 2 * ATT_TQ, seq), BF16)],
        compiler_params=_params(2),
        name="diff_attn",
    )(hcat, hcat, hcat, lam_vecs, norm_g)


HALO = 16


def _even_out_kernel(x_ref, attn_ref, cb_ref, cc_ref, cx_ref, hc_ref, hx_ref, cw_ref,
                     wout_ref, g_ref, b_ref, o_ref, *, per_seq):
    i = pl.program_id(0)
    u = cc_ref[...].astype(F32) * cx_ref[...].astype(F32)
    halo = hc_ref[...].astype(F32) * hx_ref[...].astype(F32)
    halo = jnp.where(i % per_seq == 0, jnp.zeros_like(halo), halo)
    row = lax.broadcasted_iota(jnp.int32, (8, CONV_CH), 0)
    u1 = pltpu.roll(u, 1, 0)
    u2 = pltpu.roll(u, 2, 0)
    head1 = jnp.where(row == 0, halo[HALO - 1:HALO, :], u1[:8])
    head2 = jnp.where(row == 0, halo[HALO - 2:HALO - 1, :], u2[:8])
    head2 = jnp.where(row == 1, halo[HALO - 1:HALO, :], head2)
    u1 = jnp.concatenate([head1, u1[8:]], 0)
    u2 = jnp.concatenate([head2, u2[8:]], 0)
    cw = cw_ref[...]
    conv = (cb_ref[...].astype(F32) * (cw[0:1] * u2 + cw[1:2] * u1 + cw[2:3] * u)).astype(BF16)
    rows = TM // OUT_SPLIT
    ys = []
    for r in range(OUT_SPLIT):
        sl = slice(r * rows, (r + 1) * rows)
        mix = jnp.concatenate([attn_ref[sl, :], conv[sl]], 1)
        ys.append(jnp.dot(mix, wout_ref[...], preferred_element_type=F32))
    for r in range(OUT_SPLIT):
        sl = slice(r * rows, (r + 1) * rows)
        o_ref[sl, :] = _layer_norm(ALPHA * x_ref[sl, :] + ys[r], g_ref[...], b_ref[...])


def _even_out(x, attn, hcat, conv_w, w_out, g, b, seq):
    t = x.shape[0]
    per_seq = seq // TM
    col0 = (2 * DIFF_QK + DIFF_WIDTH) // CONV_CH
    halo_blocks = TM // HALO

    def halo_spec(col):
        return pl.BlockSpec((HALO, CONV_CH),
                            lambda i: (jnp.maximum(i * halo_blocks - 1, 0), col))

    return pl.pallas_call(
        functools.partial(_even_out_kernel, per_seq=per_seq),
        grid=(t // TM,),
        in_specs=[pl.BlockSpec((TM, D_MODEL), lambda i: (i, 0)),
                  pl.BlockSpec((TM, DIFF_WIDTH), lambda i: (i, 0)),
                  pl.BlockSpec((TM, CONV_CH), lambda i: (i, col0)),
                  pl.BlockSpec((TM, CONV_CH), lambda i: (i, col0 + 1)),
                  pl.BlockSpec((TM, CONV_CH), lambda i: (i, col0 + 2)),
                  halo_spec(col0 + 1), halo_spec(col0 + 2),
                  _resident((3, CONV_CH)),
                  _resident((DIFF_WIDTH + CONV_CH, D_MODEL)),
                  _resident((1, D_MODEL)), _resident((1, D_MODEL))],
        out_specs=pl.BlockSpec((TM, D_MODEL), lambda i: (i, 0)),
        out_shape=jax.ShapeDtypeStruct((t, D_MODEL), F32),
        compiler_params=_params(1),
        name="even_out",
    )(x, attn, hcat, hcat, hcat, hcat, hcat, conv_w, w_out, g, b)


def _odd_proj_kernel(x_ref, w_ref, cos_ref, sin_ref, ng_ref, o_ref):
    xb = x_ref[...].astype(BF16)
    cos = cos_ref[...]
    sin = sin_ref[...]
    half = RET_DK // 2
    gate0 = (2 * RET_QK + RET_VW) // 512
    for n in list(range(gate0, ODD_IN // 512)) + list(range(gate0)):
        r = jnp.dot(xb, w_ref[:, n * 512:(n + 1) * 512], preferred_element_type=F32)
        if n >= gate0:
            gain = ng_ref[:, (n - gate0) * 512:(n - gate0 + 1) * 512]
            o_ref[:, n * 512:(n + 1) * 512] = (r * jax.nn.sigmoid(r) * gain).astype(BF16)
        elif n < 2 * RET_QK // 512:
            for hh in range(512 // RET_DK):
                x1 = r[:, hh * RET_DK:hh * RET_DK + half]
                x2 = r[:, hh * RET_DK + half:(hh + 1) * RET_DK]
                o1 = x1 * cos - x2 * sin
                o2 = x2 * cos + x1 * sin
                if n >= RET_QK // 512:
                    o1 = o1 * RET_DK ** -0.5
                    o2 = o2 * RET_DK ** -0.5
                base = n * 512 + hh * RET_DK
                o_ref[:, base:base + half] = o1.astype(BF16)
                o_ref[:, base + half:base + RET_DK] = o2.astype(BF16)
        else:
            o_ref[:, n * 512:(n + 1) * 512] = r.astype(BF16)


def _odd_proj(x, w, cos, sin, norm_g, seq):
    t = x.shape[0]
    per_seq = seq // TM
    tbl = pl.BlockSpec((TM, LANES), lambda i: (i % per_seq, 0))
    return pl.pallas_call(
        _odd_proj_kernel,
        grid=(t // TM,),
        in_specs=[pl.BlockSpec((TM, D_MODEL), lambda i: (i, 0)),
                  _resident((D_MODEL, ODD_IN)), tbl, tbl, _resident((1, RET_VW))],
        out_specs=pl.BlockSpec((TM, ODD_IN), lambda i: (i, 0)),
        out_shape=jax.ShapeDtypeStruct((t, ODD_IN), BF16),
        compiler_params=_params(1),
        name="odd_proj",
    )(x, w, cos, sin, norm_g)


def _retention_kernel(cd_ref, q_ref, k_ref, v_ref, gate_ref, dmat_ref, qd_ref, kd_ref,
                      o_ref, *, seq):
    hp = RET_HEADS_PER_STEP
    first = pl.program_id(1) * hp
    cd = [cd_ref[first + j] for j in range(hp)]
    dmat = [dmat_ref[j] for j in range(hp)]
    qd = [jnp.concatenate([qd_ref[j]] * (RET_DV // LANES), axis=1) for j in range(hp)]
    kd = [jnp.concatenate([kd_ref[j]] * (RET_DK // LANES), axis=1) for j in range(hp)]
    state = [jnp.zeros((RET_DK, RET_DV), F32) for _ in range(hp)]
    for c in range(seq // RET_BLOCK):
        rows = pl.ds(c * RET_BLOCK, RET_BLOCK)
        for j in range(hp):
            dk = slice(j * RET_DK, (j + 1) * RET_DK)
            dv = slice(j * RET_DV, (j + 1) * RET_DV)
            qc = q_ref[rows, dk]
            kc = k_ref[rows, dk]
            vc = v_ref[rows, dv]
            sc = lax.dot_general(qc, kc, (((1,), (1,)), ((), ())),
                                 preferred_element_type=F32) * dmat[j]
            y = jnp.dot(sc.astype(BF16), vc, preferred_element_type=F32)
            y = y + jnp.dot(qc, state[j].astype(BF16), preferred_element_type=F32) * qd[j]
            kdec = (kc.astype(F32) * kd[j]).astype(BF16)
            state[j] = state[j] * cd[j] + lax.dot_general(
                kdec, vc, (((0,), (0,)), ((), ())), preferred_element_type=F32)
            mu = jnp.mean(y, -1, keepdims=True)
            d = y - mu
            var = jnp.mean(d * d, -1, keepdims=True)
            yn = d * lax.rsqrt(var + LN_EPS)
            o_ref[rows, dv] = (gate_ref[rows, dv].astype(F32) * yn).astype(BF16)


def _retention(hcat, tables, batch, seq):
    cd, dmat, qd, kd = tables
    hp = RET_HEADS_PER_STEP
    tbl = pl.BlockSpec((hp, RET_BLOCK, LANES), lambda b, h: (h, 0, 0))
    k0 = RET_QK // (hp * RET_DK)
    v0 = 2 * RET_QK // (hp * RET_DV)
    g0 = v0 + RET_VW // (hp * RET_DV)
    return pl.pallas_call(
        functools.partial(_retention_kernel, seq=seq),
        grid=(batch, RET_HEADS // hp),
        in_specs=[pl.BlockSpec(memory_space=pltpu.SMEM),
                  pl.BlockSpec((seq, hp * RET_DK), lambda b, h: (b, h)),
                  pl.BlockSpec((seq, hp * RET_DK), lambda b, h: (b, k0 + h)),
                  pl.BlockSpec((seq, hp * RET_DV), lambda b, h: (b, v0 + h)),
                  pl.BlockSpec((seq, hp * RET_DV), lambda b, h: (b, g0 + h)),
                  pl.BlockSpec((hp, RET_BLOCK, RET_BLOCK), lambda b, h: (h, 0, 0)),
                  tbl, tbl],
        out_specs=pl.BlockSpec((seq, hp * RET_DV), lambda b, h: (b, h)),
        out_shape=jax.ShapeDtypeStruct((batch * seq, RET_VW), BF16),
        compiler_params=_params(2),
        name="retention",
    )(cd, hcat, hcat, hcat, hcat, dmat, qd, kd)


def _odd_out_kernel(x_ref, y_ref, wout_ref, g_ref, b_ref, o_ref):
    rows = TM // OUT_SPLIT
    ys = [jnp.dot(y_ref[r * rows:(r + 1) * rows, :], wout_ref[...],
                  preferred_element_type=F32) for r in range(OUT_SPLIT)]
    for r in range(OUT_SPLIT):
        sl = slice(r * rows, (r + 1) * rows)
        o_ref[sl, :] = _layer_norm(ALPHA * x_ref[sl, :] + ys[r], g_ref[...], b_ref[...])


def _odd_out(x, y, w_out, g, b):
    t = x.shape[0]
    return pl.pallas_call(
        _odd_out_kernel,
        grid=(t // TM,),
        in_specs=[pl.BlockSpec((TM, D_MODEL), lambda i: (i, 0)),
                  pl.BlockSpec((TM, RET_VW), lambda i: (i, 0)),
                  _resident((RET_VW, D_MODEL)),
                  _resident((1, D_MODEL)), _resident((1, D_MODEL))],
        out_specs=pl.BlockSpec((TM, D_MODEL), lambda i: (i, 0)),
        out_shape=jax.ShapeDtypeStruct((t, D_MODEL), F32),
        compiler_params=_params(1),
        name="odd_out",
    )(x, y, w_out, g, b)


def _rope_angles(seq, n_rot, theta):
    inv = np.exp(-math.log(theta) * np.arange(n_rot // 2, dtype=np.float64) * (2.0 / n_rot))
    return np.arange(seq, dtype=np.float64)[:, None] * inv[None, :]


def _even_rope_tables(seq):
    half = ROPE_DIMS // 2
    ang = _rope_angles(seq, ROPE_DIMS, ROPE_THETA)
    cos, sin = np.cos(ang), np.sin(ang)
    pad = np.zeros((seq, DIFF_DH - ROPE_DIMS))
    zeros = np.zeros((seq, half))
    c = np.concatenate([cos, cos, pad + 1.0], -1)
    sa = np.concatenate([-sin, zeros, pad], -1)
    sb = np.concatenate([zeros, sin, pad], -1)
    reps = LANES // DIFF_DH
    return tuple(jnp.asarray(np.tile(t, (1, reps)), F32) for t in (c, sa, sb))


def _odd_rope_tables(seq):
    ang = _rope_angles(seq, RET_DK, RET_ROPE_THETA)
    return jnp.asarray(np.cos(ang), F32), jnp.asarray(np.sin(ang), F32)


def _retention_tables():
    log_g = np.log1p(-np.exp2(-5.0 - np.arange(RET_HEADS, dtype=np.float64)))
    idx = np.arange(RET_BLOCK, dtype=np.float64)
    rel = idx[:, None] - idx[None, :]
    dmat = np.where(rel >= 0, np.exp(log_g[:, None, None] * np.maximum(rel, 0.0)), 0.0)
    q_decay = np.exp(log_g[:, None] * (idx + 1.0))
    k_decay = np.exp(log_g[:, None] * (RET_BLOCK - 1.0 - idx))
    cd = np.exp(log_g * RET_BLOCK)
    rep = lambda t: np.broadcast_to(t[:, :, None], (RET_HEADS, RET_BLOCK, LANES))
    return tuple(jnp.asarray(t, F32) for t in (cd, dmat, rep(q_decay), rep(k_decay)))


def kernel(x, ln_g, ln_b, ffn_w_in, ffn_w_out, even_w_in, even_w_out, diff_lambda,
           diff_norm_g, conv_w, odd_w_in, odd_w_out, ret_norm_g):
    batch, seq, _ = x.shape
    assert seq % TM == 0 and seq % ATT_TQ == 0 and DEPTH == 2
    h = x.reshape(batch * seq, D_MODEL)
    row = lambda v: v.reshape(1, -1)
    norm = lambda i, k: (row(ln_g[i, k]), row(ln_b[i, k]))
    ffn = lambda i, k: [(ffn_w_in, (i, k)), (ffn_w_out, (i, k))]

    w_in, w_out = ffn_w_in[0, 0].astype(BF16), ffn_w_out[0, 0].astype(BF16)

    h, (w_in, w_out, e_in, e_out) = _ffn_ln(
        h, w_in, w_out, *norm(0, 0),
        casts=ffn(0, 1) + [(even_w_in, (0,)), (even_w_out, (0,))])
    lambda_init = 0.8 - 0.6 * math.exp(-0.3 * 0)
    hcat, next_ffn = _even_proj(h, e_in, *_even_rope_tables(seq), seq, casts=ffn(1, 0))
    attn = _diff_attention(hcat, diff_lambda[0], row(diff_norm_g[0]), batch, seq, lambda_init)
    h = _even_out(h, attn, hcat, conv_w[0], e_out, *norm(0, 1), seq)
    h, (o_in, o_out) = _ffn_ln(h, w_in, w_out, *norm(0, 2),
                               casts=[(odd_w_in, (0,)), (odd_w_out, (0,))])

    h, last_ffn = _ffn_ln(h, *next_ffn, *norm(1, 0), casts=ffn(1, 1))
    hcat = _odd_proj(h, o_in, *_odd_rope_tables(seq), row(ret_norm_g[0]), seq)
    y = _retention(hcat, _retention_tables(), batch, seq)
    h = _odd_out(h, y, o_out, *norm(1, 1))
    h, _ = _ffn_ln(h, *last_ffn, *norm(1, 2))
    return h.reshape(batch, seq, D_MODEL)
```

```python
import functools
import math

import jax
import jax.numpy as jnp
import numpy as np
from jax import lax
from jax.experimental import pallas as pl
from jax.experimental.pallas import tpu as pltpu

F32 = jnp.float32
BF16 = jnp.bfloat16

D_MODEL = 1024
DEPTH = 2
D_FF = 2816
DIFF_HEADS = 4
DIFF_DH = 64
DIFF_DV = 128
DIFF_QK = 512
DIFF_WIDTH = 512
ROPE_THETA = 500000.0
ROPE_DIMS = 16
CONV_CH = 512
EVEN_IN = 3072
RET_HEADS = 4
RET_DK = 256
RET_DV = 512
RET_QK = 1024
RET_VW = 2048
RET_ROPE_THETA = 10000.0
ODD_IN = 6144
ALPHA = (2.0 * DEPTH) ** 0.25
LN_EPS = 1e-5
NEG_INF = -1e30
LOG2E = math.log2(math.e)

LANES = 128
VMEM_LIMIT = 56 * 1024 * 1024

TM = 1024
FF_CHUNK = 256
OUT_SPLIT = 4
FFN_ROW_GROUPS = (256, 256, 256, 256)
RET_BLOCK = 256
ATT_TQ = 256
ATT_STRIP = 16
ATT_HEADS_PER_STEP = 1
RET_HEADS_PER_STEP = 2


def _layer_norm(z, g, b):
    mu = jnp.mean(z, -1, keepdims=True)
    d = z - mu
    var = jnp.mean(d * d, -1, keepdims=True)
    return d * lax.rsqrt(var + LN_EPS) * g + b


def _params(n_axes, vmem=VMEM_LIMIT):
    return pltpu.CompilerParams(dimension_semantics=("arbitrary",) * n_axes,
                                vmem_limit_bytes=vmem)


def _resident(shape):
    nd = len(shape)
    return pl.BlockSpec(shape, lambda *_: (0,) * nd, pipeline_mode=pl.Buffered(1))


def _cast_specs(casts, steps):
    in_specs, out_specs, out_shapes = [], [], []
    for arr, prefix in casts:
        r, c = arr.shape[-2:]
        assert arr.ndim == len(prefix) + 2
        rows = min(d for d in range(16, r + 1, 16) if r % d == 0 and d * steps >= r)
        last = r // rows - 1
        in_specs.append(pl.BlockSpec((None,) * len(prefix) + (rows, c),
                                     lambda i, p=prefix, n=last: p + (jnp.minimum(i, n), 0)))
        out_specs.append(pl.BlockSpec((rows, c), lambda i, n=last: (jnp.minimum(i, n), 0)))
        out_shapes.append(jax.ShapeDtypeStruct((r, c), BF16))
    return in_specs, out_specs, out_shapes


def _do_casts(src_refs, dst_refs):
    for src, dst in zip(src_refs, dst_refs):
        dst[...] = src[...].astype(BF16)


def _ffn_kernel(x_ref, win_ref, wout_ref, g_ref, b_ref, *rest, n_cast):
    cast_src, o_ref, cast_dst, act_ref = (rest[:n_cast], rest[n_cast],
                                          rest[n_cast + 1:2 * n_cast + 1], rest[-1])
    _do_casts(cast_src, cast_dst)
    x = x_ref[...]
    xb = x.astype(BF16)
    for c in range(D_FF // FF_CHUNK):
        lo = c * FF_CHUNK
        gate = jnp.dot(xb, win_ref[:, lo:lo + FF_CHUNK], preferred_element_type=F32)
        up = jnp.dot(xb, win_ref[:, D_FF + lo:D_FF + lo + FF_CHUNK], preferred_element_type=F32)
        act_ref[:, lo:lo + FF_CHUNK] = (gate * jax.nn.sigmoid(gate) * up).astype(BF16)
    bounds = [sum(FFN_ROW_GROUPS[:r]) for r in range(len(FFN_ROW_GROUPS) + 1)]
    groups = [slice(lo, hi) for lo, hi in zip(bounds[:-1], bounds[1:])]
    ys = [jnp.dot(act_ref[sl, :], wout_ref[...], preferred_element_type=F32) for sl in groups]
    for sl, y in zip(groups, ys):
        o_ref[sl, :] = _layer_norm(ALPHA * x_ref[sl, :] + 0.5 * y, g_ref[...], b_ref[...])


def _ffn_ln(x, w_in, w_out, g, b, casts=()):
    t = x.shape[0]
    steps = t // TM
    cast_in, cast_out, cast_shapes = _cast_specs(casts, steps)
    out = pl.pallas_call(
        functools.partial(_ffn_kernel, n_cast=len(casts)),
        grid=(steps,),
        in_specs=[pl.BlockSpec((TM, D_MODEL), lambda i: (i, 0)),
                  _resident((D_MODEL, 2 * D_FF)),
                  _resident((D_FF, D_MODEL)),
                  _resident((1, D_MODEL)),
                  _resident((1, D_MODEL))] + cast_in,
        out_specs=[pl.BlockSpec((TM, D_MODEL), lambda i: (i, 0))] + cast_out,
        out_shape=[jax.ShapeDtypeStruct((t, D_MODEL), F32)] + cast_shapes,
        scratch_shapes=[pltpu.VMEM((TM, D_FF), BF16)],
        compiler_params=_params(1),
        name="ffn_ln",
    )(x, w_in, w_out, g, b, *[arr for arr, _ in casts])
    return out[0], out[1:]


EVEN_OUT_COLS = 2 * DIFF_QK + DIFF_WIDTH + CONV_CH
CONV_TAIL = 8


def _even_proj_kernel(x_ref, w_ref, c_ref, sa_ref, sb_ref, cw_ref, *rest, n_cast, per_seq):
    cast_src, o_ref, cast_dst, tail_ref = (rest[:n_cast], rest[n_cast],
                                           rest[n_cast + 1:2 * n_cast + 1], rest[-1])
    _do_casts(cast_src, cast_dst)
    i = pl.program_id(0)

    @pl.when(i == 0)
    def _():
        tail_ref[...] = jnp.zeros_like(tail_ref)

    xb = x_ref[...].astype(BF16)
    c = c_ref[...]
    sa = sa_ref[...]
    sb = sb_ref[...]
    proj = lambda n: jnp.dot(xb, w_ref[:, n * 512:(n + 1) * 512], preferred_element_type=F32)
    for n in range(2):
        r = proj(n)
        scale = DIFF_DH ** -0.5 * LOG2E if n == 0 else 1.0
        for j in range(512 // LANES):
            blk = r[:, j * LANES:(j + 1) * LANES]
            rot = (blk * c + pltpu.roll(blk, LANES - 8, 1) * sa
                   + pltpu.roll(blk, 8, 1) * sb)
            if n == 0:
                rot = rot * scale
            o_ref[:, n * 512 + j * LANES:n * 512 + (j + 1) * LANES] = rot.astype(BF16)
    u = proj(4) * proj(5)
    prev = jnp.where(i % per_seq == 0, jnp.zeros_like(u[:CONV_TAIL]), tail_ref[...])
    tail_ref[...] = u[TM - CONV_TAIL:]
    row = lax.broadcasted_iota(jnp.int32, (CONV_TAIL, CONV_CH), 0)
    u1 = pltpu.roll(u, 1, 0)
    u2 = pltpu.roll(u, 2, 0)
    head1 = jnp.where(row == 0, prev[CONV_TAIL - 1:], u1[:CONV_TAIL])
    head2 = jnp.where(row == 0, prev[CONV_TAIL - 2:CONV_TAIL - 1], u2[:CONV_TAIL])
    head2 = jnp.where(row == 1, prev[CONV_TAIL - 1:], head2)
    u1 = jnp.concatenate([head1, u1[CONV_TAIL:]], 0)
    u2 = jnp.concatenate([head2, u2[CONV_TAIL:]], 0)
    cw = cw_ref[...]
    conv = proj(3) * (cw[0:1] * u2 + cw[1:2] * u1 + cw[2:3] * u)
    o_ref[:, 3 * 512:] = conv.astype(BF16)
    o_ref[:, 2 * 512:3 * 512] = proj(2).astype(BF16)


def _even_proj(x, w, c, sa, sb, conv_w, seq, casts=()):
    t = x.shape[0]
    steps = t // TM
    per_seq = seq // TM
    tbl = pl.BlockSpec((TM, LANES), lambda i: (i % per_seq, 0))
    cast_in, cast_out, cast_shapes = _cast_specs(casts, steps)
    out = pl.pallas_call(
        functools.partial(_even_proj_kernel, n_cast=len(casts), per_seq=per_seq),
        grid=(steps,),
        in_specs=[pl.BlockSpec((TM, D_MODEL), lambda i: (i, 0)),
                  _resident((D_MODEL, EVEN_IN)), tbl, tbl, tbl,
                  _resident((3, CONV_CH))] + cast_in,
        out_specs=[pl.BlockSpec((TM, EVEN_OUT_COLS), lambda i: (i, 0))] + cast_out,
        out_shape=[jax.ShapeDtypeStruct((t, EVEN_OUT_COLS), BF16)] + cast_shapes,
        scratch_shapes=[pltpu.VMEM((CONV_TAIL, CONV_CH), F32)],
        compiler_params=_params(1),
        name="even_proj",
    )(x, w, c, sa, sb, conv_w, *[arr for arr, _ in casts])
    return out[0], out[1:]


def _attn_kernel(q_ref, k_ref, v_ref, lv_ref, g_ref, o_ref, s_ref, p_ref, *, lambda_init, seq):
    lv = lv_ref[...]
    lam = (jnp.exp(jnp.sum(lv[0:1] * lv[1:2], -1, keepdims=True))
           - jnp.exp(jnp.sum(lv[2:3] * lv[3:4], -1, keepdims=True)) + lambda_init)
    lane = lax.broadcasted_iota(jnp.int32, (ATT_TQ, LANES), 1)
    row = lax.broadcasted_iota(jnp.int32, (ATT_TQ, ATT_TQ), 0)
    col = lax.broadcasted_iota(jnp.int32, (ATT_TQ, ATT_TQ), 1)
    causal = col <= row
    gain = g_ref[...]
    nt = (((1,), (1,)), ((), ()))
    n_tiles = seq // ATT_TQ
    order = [t for pair in zip(range(n_tiles - 1, -1, -1), range(n_tiles)) for t in pair][:n_tiles]
    for pos, i in enumerate(order):
        for j in range(ATT_HEADS_PER_STEP):
            head = slice(j * LANES, (j + 1) * LANES)
            buf = pos % 2
            lo = i * ATT_TQ
            q = q_ref[lo:lo + ATT_TQ, head]
            zero = jnp.zeros_like(q)
            qs = (jnp.where(lane < DIFF_DH, q, zero), jnp.where(lane >= DIFF_DH, q, zero))
            hi = lo + ATT_TQ
            l = []
            for c, qc in enumerate(qs):
                s = lax.dot_general(qc, k_ref[:hi, head], nt, preferred_element_type=F32)
                sc_ref = s_ref.at[j, buf, c]
                if i > 0:
                    sc_ref[:, :lo] = s[:, :lo]
                sc_ref[:, lo:hi] = jnp.where(causal, s[:, lo:], NEG_INF)
                m = jnp.max(sc_ref[:, :hi], -1, keepdims=True)
                lsum = []
                for r in range(ATT_TQ // ATT_STRIP):
                    sl = slice(r * ATT_STRIP, (r + 1) * ATT_STRIP)
                    p = jnp.exp2(sc_ref[sl, :hi] - m[sl])
                    lsum.append(jnp.sum(p, -1, keepdims=True))
                    base = c * ATT_TQ + r * ATT_STRIP
                    p_ref[j, buf, base:base + ATT_STRIP, :hi] = p.astype(BF16)
                l.append(jnp.concatenate(lsum, 0))
            acc = jnp.dot(p_ref[j, buf, :, :hi], v_ref[:hi, head], preferred_element_type=F32)
            acc = acc[:ATT_TQ] * (1.0 / l[0]) - acc[ATT_TQ:] * (lam / l[1])
            y = acc * lax.rsqrt(jnp.mean(acc * acc, -1, keepdims=True) + LN_EPS)
            o_ref[lo:lo + ATT_TQ, head] = (y * gain * (1.0 - lambda_init)).astype(BF16)


def _diff_attention(hcat, lam_vecs, norm_g, batch, seq, lambda_init):
    hp = ATT_HEADS_PER_STEP
    width = hp * LANES
    return pl.pallas_call(
        functools.partial(_attn_kernel, lambda_init=lambda_init, seq=seq),
        grid=(batch, DIFF_HEADS // hp),
        in_specs=[pl.BlockSpec((seq, width), lambda b, h: (b, h)),
                  pl.BlockSpec((seq, width), lambda b, h: (b, DIFF_QK // width + h)),
                  pl.BlockSpec((seq, width), lambda b, h: (b, 2 * DIFF_QK // width + h)),
                  _resident((4, DIFF_DH)),
                  _resident((1, DIFF_DV))],
        out_specs=pl.BlockSpec((seq, width), lambda b, h: (b, h)),
        out_shape=jax.ShapeDtypeStruct((batch * seq, DIFF_WIDTH), BF16),
        scratch_shapes=[pltpu.VMEM((hp, 2, 2, ATT_TQ, seq), F32),
                        pltpu.VMEM((hp, 2, 2 * ATT_TQ, seq), BF16)],
        compiler_params=_params(2),
        name="diff_attn",
    )(hcat, hcat, hcat, lam_vecs, norm_g)


def _even_out_kernel(x_ref, attn_ref, conv_ref, wout_ref, g_ref, b_ref, o_ref):
    rows = TM // OUT_SPLIT
    ys = []
    for r in range(OUT_SPLIT):
        sl = slice(r * rows, (r + 1) * rows)
        mix = jnp.concatenate([attn_ref[sl, :], conv_ref[sl, :]], 1)
        ys.append(jnp.dot(mix, wout_ref[...], preferred_element_type=F32))
    for r in range(OUT_SPLIT):
        sl = slice(r * rows, (r + 1) * rows)
        o_ref[sl, :] = _layer_norm(ALPHA * x_ref[sl, :] + ys[r], g_ref[...], b_ref[...])


def _even_out(x, attn, hcat, w_out, g, b):
    t = x.shape[0]
    conv_col = (2 * DIFF_QK + DIFF_WIDTH) // CONV_CH
    return pl.pallas_call(
        _even_out_kernel,
        grid=(t // TM,),
        in_specs=[pl.BlockSpec((TM, D_MODEL), lambda i: (i, 0)),
                  pl.BlockSpec((TM, DIFF_WIDTH), lambda i: (i, 0)),
                  pl.BlockSpec((TM, CONV_CH), lambda i: (i, conv_col)),
                  _resident((DIFF_WIDTH + CONV_CH, D_MODEL)),
                  _resident((1, D_MODEL)), _resident((1, D_MODEL))],
        out_specs=pl.BlockSpec((TM, D_MODEL), lambda i: (i, 0)),
        out_shape=jax.ShapeDtypeStruct((t, D_MODEL), F32),
        compiler_params=_params(1),
        name="even_out",
    )(x, attn, hcat, w_out, g, b)


def _odd_proj_kernel(x_ref, w_ref, cos_ref, sin_ref, ng_ref, o_ref):
    xb = x_ref[...].astype(BF16)
    cos = cos_ref[...]
    sin = sin_ref[...]
    half = RET_DK // 2
    gate0 = (2 * RET_QK + RET_VW) // 512
    for n in list(range(gate0, ODD_IN // 512)) + list(range(gate0)):
        r = jnp.dot(xb, w_ref[:, n * 512:(n + 1) * 512], preferred_element_type=F32)
        if n >= gate0:
            gain = ng_ref[:, (n - gate0) * 512:(n - gate0 + 1) * 512]
            o_ref[:, n * 512:(n + 1) * 512] = (r * jax.nn.sigmoid(r) * gain).astype(BF16)
        elif n < 2 * RET_QK // 512:
            for hh in range(512 // RET_DK):
                x1 = r[:, hh * RET_DK:hh * RET_DK + half]
                x2 = r[:, hh * RET_DK + half:(hh + 1) * RET_DK]
                o1 = x1 * cos - x2 * sin
                o2 = x2 * cos + x1 * sin
                if n >= RET_QK // 512:
                    o1 = o1 * RET_DK ** -0.5
                    o2 = o2 * RET_DK ** -0.5
                base = n * 512 + hh * RET_DK
                o_ref[:, base:base + half] = o1.astype(BF16)
                o_ref[:, base + half:base + RET_DK] = o2.astype(BF16)
        else:
            o_ref[:, n * 512:(n + 1) * 512] = r.astype(BF16)


def _odd_proj(x, w, cos, sin, norm_g, seq):
    t = x.shape[0]
    per_seq = seq // TM
    tbl = pl.BlockSpec((TM, LANES), lambda i: (i % per_seq, 0))
    return pl.pallas_call(
        _odd_proj_kernel,
        grid=(t // TM,),
        in_specs=[pl.BlockSpec((TM, D_MODEL), lambda i: (i, 0)),
                  _resident((D_MODEL, ODD_IN)), tbl, tbl, _resident((1, RET_VW))],
        out_specs=pl.BlockSpec((TM, ODD_IN), lambda i: (i, 0)),
        out_shape=jax.ShapeDtypeStruct((t, ODD_IN), BF16),
        compiler_params=_params(1),
        name="odd_proj",
    )(x, w, cos, sin, norm_g)


def _retention_kernel(cd_ref, q_ref, k_ref, v_ref, gate_ref, dmat_ref, qd_ref, kd_ref,
                      o_ref, *, seq):
    hp = RET_HEADS_PER_STEP
    first = pl.program_id(1) * hp
    cd = [cd_ref[first + j] for j in range(hp)]
    dmat = [dmat_ref[j] for j in range(hp)]
    qd = [jnp.concatenate([qd_ref[j]] * (RET_DV // LANES), axis=1) for j in range(hp)]
    kd = [jnp.concatenate([kd_ref[j]] * (RET_DK // LANES), axis=1) for j in range(hp)]
    state = [jnp.zeros((RET_DK, RET_DV), F32) for _ in range(hp)]
    for c in range(seq // RET_BLOCK):
        rows = pl.ds(c * RET_BLOCK, RET_BLOCK)
        for j in range(hp):
            dk = slice(j * RET_DK, (j + 1) * RET_DK)
            dv = slice(j * RET_DV, (j + 1) * RET_DV)
            qc = q_ref[rows, dk]
            kc = k_ref[rows, dk]
            vc = v_ref[rows, dv]
            sc = lax.dot_general(qc, kc, (((1,), (1,)), ((), ())),
                                 preferred_element_type=F32) * dmat[j]
            y = jnp.dot(sc.astype(BF16), vc, preferred_element_type=F32)
            y = y + jnp.dot(qc, state[j].astype(BF16), preferred_element_type=F32) * qd[j]
            kdec = (kc.astype(F32) * kd[j]).astype(BF16)
            state[j] = state[j] * cd[j] + lax.dot_general(
                kdec, vc, (((0,), (0,)), ((), ())), preferred_element_type=F32)
            mu = jnp.mean(y, -1, keepdims=True)
            d = y - mu
            var = jnp.mean(d * d, -1, keepdims=True)
            yn = d * lax.rsqrt(var + LN_EPS)
            o_ref[rows, dv] = (gate_ref[rows, dv].astype(F32) * yn).astype(BF16)


def _retention(hcat, tables, batch, seq):
    cd, dmat, qd, kd = tables
    hp = RET_HEADS_PER_STEP
    tbl = pl.BlockSpec((hp, RET_BLOCK, LANES), lambda b, h: (h, 0, 0))
    k0 = RET_QK // (hp * RET_DK)
    v0 = 2 * RET_QK // (hp * RET_DV)
    g0 = v0 + RET_VW // (hp * RET_DV)
    return pl.pallas_call(
        functools.partial(_retention_kernel, seq=seq),
        grid=(batch, RET_HEADS // hp),
        in_specs=[pl.BlockSpec(memory_space=pltpu.SMEM),
                  pl.BlockSpec((seq, hp * RET_DK), lambda b, h: (b, h)),
                  pl.BlockSpec((seq, hp * RET_DK), lambda b, h: (b, k0 + h)),
                  pl.BlockSpec((seq, hp * RET_DV), lambda b, h: (b, v0 + h)),
                  pl.BlockSpec((seq, hp * RET_DV), lambda b, h: (b, g0 + h)),
                  pl.BlockSpec((hp, RET_BLOCK, RET_BLOCK), lambda b, h: (h, 0, 0)),
                  tbl, tbl],
        out_specs=pl.BlockSpec((seq, hp * RET_DV), lambda b, h: (b, h)),
        out_shape=jax.ShapeDtypeStruct((batch * seq, RET_VW), BF16),
        compiler_params=_params(2),
        name="retention",
    )(cd, hcat, hcat, hcat, hcat, dmat, qd, kd)


def _odd_out_kernel(x_ref, y_ref, wout_ref, g_ref, b_ref, o_ref):
    rows = TM // OUT_SPLIT
    ys = [jnp.dot(y_ref[r * rows:(r + 1) * rows, :], wout_ref[...],
                  preferred_element_type=F32) for r in range(OUT_SPLIT)]
    for r in range(OUT_SPLIT):
        sl = slice(r * rows, (r + 1) * rows)
        o_ref[sl, :] = _layer_norm(ALPHA * x_ref[sl, :] + ys[r], g_ref[...], b_ref[...])


def _odd_out(x, y, w_out, g, b):
    t = x.shape[0]
    return pl.pallas_call(
        _odd_out_kernel,
        grid=(t // TM,),
        in_specs=[pl.BlockSpec((TM, D_MODEL), lambda i: (i, 0)),
                  pl.BlockSpec((TM, RET_VW), lambda i: (i, 0)),
                  _resident((RET_VW, D_MODEL)),
                  _resident((1, D_MODEL)), _resident((1, D_MODEL))],
        out_specs=pl.BlockSpec((TM, D_MODEL), lambda i: (i, 0)),
        out_shape=jax.ShapeDtypeStruct((t, D_MODEL), F32),
        compiler_params=_params(1),
        name="odd_out",
    )(x, y, w_out, g, b)


def _rope_angles(seq, n_rot, theta):
    inv = np.exp(-math.log(theta) * np.arange(n_rot // 2, dtype=np.float64) * (2.0 / n_rot))
    return np.arange(seq, dtype=np.float64)[:, None] * inv[None, :]


def _even_rope_tables(seq):
    half = ROPE_DIMS // 2
    ang = _rope_angles(seq, ROPE_DIMS, ROPE_THETA)
    cos, sin = np.cos(ang), np.sin(ang)
    pad = np.zeros((seq, DIFF_DH - ROPE_DIMS))
    zeros = np.zeros((seq, half))
    c = np.concatenate([cos, cos, pad + 1.0], -1)
    sa = np.concatenate([-sin, zeros, pad], -1)
    sb = np.concatenate([zeros, sin, pad], -1)
    reps = LANES // DIFF_DH
    return tuple(jnp.asarray(np.tile(t, (1, reps)), F32) for t in (c, sa, sb))


def _odd_rope_tables(seq):
    ang = _rope_angles(seq, RET_DK, RET_ROPE_THETA)
    return jnp.asarray(np.cos(ang), F32), jnp.asarray(np.sin(ang), F32)


def _retention_tables():
    log_g = np.log1p(-np.exp2(-5.0 - np.arange(RET_HEADS, dtype=np.float64)))
    idx = np.arange(RET_BLOCK, dtype=np.float64)
    rel = idx[:, None] - idx[None, :]
    dmat = np.where(rel >= 0, np.exp(log_g[:, None, None] * np.maximum(rel, 0.0)), 0.0)
    q_decay = np.exp(log_g[:, None] * (idx + 1.0))
    k_decay = np.exp(log_g[:, None] * (RET_BLOCK - 1.0 - idx))
    cd = np.exp(log_g * RET_BLOCK)
    rep = lambda t: np.broadcast_to(t[:, :, None], (RET_HEADS, RET_BLOCK, LANES))
    return tuple(jnp.asarray(t, F32) for t in (cd, dmat, rep(q_decay), rep(k_decay)))


def kernel(x, ln_g, ln_b, ffn_w_in, ffn_w_out, even_w_in, even_w_out, diff_lambda,
           diff_norm_g, conv_w, odd_w_in, odd_w_out, ret_norm_g):
    batch, seq, _ = x.shape
    assert seq % TM == 0 and seq % ATT_TQ == 0 and DEPTH == 2
    h = x.reshape(batch * seq, D_MODEL)
    row = lambda v: v.reshape(1, -1)
    norm = lambda i, k: (row(ln_g[i, k]), row(ln_b[i, k]))
    ffn = lambda i, k: [(ffn_w_in, (i, k)), (ffn_w_out, (i, k))]

    w_in, w_out = ffn_w_in[0, 0].astype(BF16), ffn_w_out[0, 0].astype(BF16)

    h, (w_in, w_out, e_in, e_out) = _ffn_ln(
        h, w_in, w_out, *norm(0, 0),
        casts=ffn(0, 1) + [(even_w_in, (0,)), (even_w_out, (0,))])
    lambda_init = 0.8 - 0.6 * math.exp(-0.3 * 0)
    hcat, next_ffn = _even_proj(h, e_in, *_even_rope_tables(seq), conv_w[0], seq,
                                casts=ffn(1, 0))
    attn = _diff_attention(hcat, diff_lambda[0], row(diff_norm_g[0]), batch, seq, lambda_init)
    h = _even_out(h, attn, hcat, e_out, *norm(0, 1))
    h, (o_in, o_out) = _ffn_ln(h, w_in, w_out, *norm(0, 2),
                               casts=[(odd_w_in, (0,)), (odd_w_out, (0,))])

    h, last_ffn = _ffn_ln(h, *next_ffn, *norm(1, 0), casts=ffn(1, 1))
    hcat = _odd_proj(h, o_in, *_odd_rope_tables(seq), row(ret_norm_g[0]), seq)
    y = _retention(hcat, _retention_tables(), batch, seq)
    h = _odd_out(h, y, o_out, *norm(1, 1))
    h, _ = _ffn_ln(h, *last_ffn, *norm(1, 2))
    return h.reshape(batch, seq, D_MODEL)
```

```python
import functools
import math

import jax
import jax.numpy as jnp
import numpy as np
from jax import lax
from jax.experimental import pallas as pl
from jax.experimental.pallas import tpu as pltpu

F32 = jnp.float32
BF16 = jnp.bfloat16

D_MODEL = 1024
DEPTH = 2
D_FF = 2816
DIFF_HEADS = 4
DIFF_DH = 64
DIFF_DV = 128
DIFF_QK = 512
DIFF_WIDTH = 512
ROPE_THETA = 500000.0
ROPE_DIMS = 16
CONV_CH = 512
EVEN_IN = 3072
RET_HEADS = 4
RET_DK = 256
RET_DV = 512
RET_QK = 1024
RET_VW = 2048
RET_ROPE_THETA = 10000.0
ODD_IN = 6144
ALPHA = (2.0 * DEPTH) ** 0.25
LN_EPS = 1e-5
NEG_INF = -1e30
LOG2E = math.log2(math.e)

LANES = 128
MIB = 1024 * 1024
VMEM_LIMIT = 56 * MIB

TM = 1024
FF_CHUNK = 256
OUT_SPLIT = 4
FFN_ROW_GROUPS = (256, 256, 256, 256)
RET_BLOCK = 256
ATT_TQ = 256
ATT_STRIP = 16
ATT_HEADS_PER_STEP = 1
RET_HEADS_PER_STEP = 2


def _layer_norm(z, g, b):
    mu = jnp.mean(z, -1, keepdims=True)
    d = z - mu
    var = jnp.mean(d * d, -1, keepdims=True)
    return d * lax.rsqrt(var + LN_EPS) * g + b


def _params(n_axes, vmem_mib):
    assert vmem_mib * MIB <= VMEM_LIMIT
    return pltpu.CompilerParams(dimension_semantics=("arbitrary",) * n_axes,
                                vmem_limit_bytes=vmem_mib * MIB)


def _resident(shape):
    nd = len(shape)
    return pl.BlockSpec(shape, lambda *_: (0,) * nd, pipeline_mode=pl.Buffered(1))


def _cast_specs(casts, steps):
    in_specs, out_specs, out_shapes = [], [], []
    for arr, prefix in casts:
        r, c = arr.shape[-2:]
        assert arr.ndim == len(prefix) + 2
        rows = min(d for d in range(16, r + 1, 16) if r % d == 0 and d * steps >= r)
        last = r // rows - 1
        in_specs.append(pl.BlockSpec((None,) * len(prefix) + (rows, c),
                                     lambda i, p=prefix, n=last: p + (jnp.minimum(i, n), 0)))
        out_specs.append(pl.BlockSpec((rows, c), lambda i, n=last: (jnp.minimum(i, n), 0)))
        out_shapes.append(jax.ShapeDtypeStruct((r, c), BF16))
    return in_specs, out_specs, out_shapes


def _do_casts(src_refs, dst_refs):
    for src, dst in zip(src_refs, dst_refs):
        dst[...] = src[...].astype(BF16)


def _ffn_kernel(x_ref, win_ref, wout_ref, g_ref, b_ref, *rest, n_cast):
    cast_src, o_ref, cast_dst, act_ref = (rest[:n_cast], rest[n_cast],
                                          rest[n_cast + 1:2 * n_cast + 1], rest[-1])
    _do_casts(cast_src, cast_dst)
    x = x_ref[...]
    xb = x.astype(BF16)
    for c in range(D_FF // FF_CHUNK):
        lo = c * FF_CHUNK
        gate = jnp.dot(xb, win_ref[:, lo:lo + FF_CHUNK], preferred_element_type=F32)
        up = jnp.dot(xb, win_ref[:, D_FF + lo:D_FF + lo + FF_CHUNK], preferred_element_type=F32)
        act_ref[:, lo:lo + FF_CHUNK] = (gate * jax.nn.sigmoid(gate) * up).astype(BF16)
    bounds = [sum(FFN_ROW_GROUPS[:r]) for r in range(len(FFN_ROW_GROUPS) + 1)]
    groups = [slice(lo, hi) for lo, hi in zip(bounds[:-1], bounds[1:])]
    ys = [jnp.dot(act_ref[sl, :], wout_ref[...], preferred_element_type=F32) for sl in groups]
    for sl, y in zip(groups, ys):
        o_ref[sl, :] = _layer_norm(ALPHA * x_ref[sl, :] + 0.5 * y, g_ref[...], b_ref[...])


def _ffn_ln(x, w_in, w_out, g, b, casts=()):
    t = x.shape[0]
    steps = t // TM
    cast_in, cast_out, cast_shapes = _cast_specs(casts, steps)
    out = pl.pallas_call(
        functools.partial(_ffn_kernel, n_cast=len(casts)),
        grid=(steps,),
        in_specs=[pl.BlockSpec((TM, D_MODEL), lambda i: (i, 0)),
                  _resident((D_MODEL, 2 * D_FF)),
                  _resident((D_FF, D_MODEL)),
                  _resident((1, D_MODEL)),
                  _resident((1, D_MODEL))] + cast_in,
        out_specs=[pl.BlockSpec((TM, D_MODEL), lambda i: (i, 0))] + cast_out,
        out_shape=[jax.ShapeDtypeStruct((t, D_MODEL), F32)] + cast_shapes,
        scratch_shapes=[pltpu.VMEM((TM, D_FF), BF16)],
        compiler_params=_params(1, 56 if casts else 46),
        name="ffn_ln",
    )(x, w_in, w_out, g, b, *[arr for arr, _ in casts])
    return out[0], out[1:]


EVEN_OUT_COLS = 2 * DIFF_QK + DIFF_WIDTH + CONV_CH
CONV_TAIL = 8


def _even_proj_kernel(x_ref, w_ref, c_ref, sa_ref, sb_ref, cw_ref, *rest, n_cast, per_seq):
    cast_src, o_ref, cast_dst, tail_ref = (rest[:n_cast], rest[n_cast],
                                           rest[n_cast + 1:2 * n_cast + 1], rest[-1])
    _do_casts(cast_src, cast_dst)
    i = pl.program_id(0)

    @pl.when(i == 0)
    def _():
        tail_ref[...] = jnp.zeros_like(tail_ref)

    xb = x_ref[...].astype(BF16)
    c = c_ref[...]
    sa = sa_ref[...]
    sb = sb_ref[...]
    proj = lambda n: jnp.dot(xb, w_ref[:, n * 512:(n + 1) * 512], preferred_element_type=F32)
    for n in range(2):
        r = proj(n)
        scale = DIFF_DH ** -0.5 * LOG2E if n == 0 else 1.0
        for j in range(512 // LANES):
            blk = r[:, j * LANES:(j + 1) * LANES]
            rot = (blk * c + pltpu.roll(blk, LANES - 8, 1) * sa
                   + pltpu.roll(blk, 8, 1) * sb)
            if n == 0:
                rot = rot * scale
            o_ref[:, n * 512 + j * LANES:n * 512 + (j + 1) * LANES] = rot.astype(BF16)
    u = proj(4) * proj(5)
    prev = jnp.where(i % per_seq == 0, jnp.zeros_like(u[:CONV_TAIL]), tail_ref[...])
    tail_ref[...] = u[TM - CONV_TAIL:]
    row = lax.broadcasted_iota(jnp.int32, (CONV_TAIL, CONV_CH), 0)
    u1 = pltpu.roll(u, 1, 0)
    u2 = pltpu.roll(u, 2, 0)
    head1 = jnp.where(row == 0, prev[CONV_TAIL - 1:], u1[:CONV_TAIL])
    head2 = jnp.where(row == 0, prev[CONV_TAIL - 2:CONV_TAIL - 1], u2[:CONV_TAIL])
    head2 = jnp.where(row == 1, prev[CONV_TAIL - 1:], head2)
    u1 = jnp.concatenate([head1, u1[CONV_TAIL:]], 0)
    u2 = jnp.concatenate([head2, u2[CONV_TAIL:]], 0)
    cw = cw_ref[...]
    conv = proj(3) * (cw[0:1] * u2 + cw[1:2] * u1 + cw[2:3] * u)
    o_ref[:, 3 * 512:] = conv.astype(BF16)
    o_ref[:, 2 * 512:3 * 512] = proj(2).astype(BF16)


def _even_proj(x, w, c, sa, sb, conv_w, seq, casts=()):
    t = x.shape[0]
    steps = t // TM
    per_seq = seq // TM
    tbl = pl.BlockSpec((TM, LANES), lambda i: (i % per_seq, 0))
    cast_in, cast_out, cast_shapes = _cast_specs(casts, steps)
    out = pl.pallas_call(
        functools.partial(_even_proj_kernel, n_cast=len(casts), per_seq=per_seq),
        grid=(steps,),
        in_specs=[pl.BlockSpec((TM, D_MODEL), lambda i: (i, 0)),
                  _resident((D_MODEL, EVEN_IN)), tbl, tbl, tbl,
                  _resident((3, CONV_CH))] + cast_in,
        out_specs=[pl.BlockSpec((TM, EVEN_OUT_COLS), lambda i: (i, 0))] + cast_out,
        out_shape=[jax.ShapeDtypeStruct((t, EVEN_OUT_COLS), BF16)] + cast_shapes,
        scratch_shapes=[pltpu.VMEM((CONV_TAIL, CONV_CH), F32)],
        compiler_params=_params(1, 40),
        name="even_proj",
    )(x, w, c, sa, sb, conv_w, *[arr for arr, _ in casts])
    return out[0], out[1:]


def _attn_kernel(q_ref, k_ref, v_ref, lv_ref, g_ref, o_ref, s_ref, p_ref, *, lambda_init, seq):
    lv = lv_ref[...]
    lam = (jnp.exp(jnp.sum(lv[0:1] * lv[1:2], -1, keepdims=True))
           - jnp.exp(jnp.sum(lv[2:3] * lv[3:4], -1, keepdims=True)) + lambda_init)
    lane = lax.broadcasted_iota(jnp.int32, (ATT_TQ, LANES), 1)
    row = lax.broadcasted_iota(jnp.int32, (ATT_TQ, ATT_TQ), 0)
    col = lax.broadcasted_iota(jnp.int32, (ATT_TQ, ATT_TQ), 1)
    causal = col <= row
    gain = g_ref[...]
    nt = (((1,), (1,)), ((), ()))
    n_tiles = seq // ATT_TQ
    order = [t for pair in zip(range(n_tiles - 1, -1, -1), range(n_tiles)) for t in pair][:n_tiles]
    for pos, i in enumerate(order):
        for j in range(ATT_HEADS_PER_STEP):
            head = slice(j * LANES, (j + 1) * LANES)
            buf = pos % 2
            lo = i * ATT_TQ
            q = q_ref[lo:lo + ATT_TQ, head]
            zero = jnp.zeros_like(q)
            qs = (jnp.where(lane < DIFF_DH, q, zero), jnp.where(lane >= DIFF_DH, q, zero))
            hi = lo + ATT_TQ
            l = []
            for c, qc in enumerate(qs):
                s = lax.dot_general(qc, k_ref[:hi, head], nt, preferred_element_type=F32)
                sc_ref = s_ref.at[j, buf, c]
                if i > 0:
                    sc_ref[:, :lo] = s[:, :lo]
                sc_ref[:, lo:hi] = jnp.where(causal, s[:, lo:], NEG_INF)
                m = jnp.max(sc_ref[:, :hi], -1, keepdims=True)
                lsum = []
                for r in range(ATT_TQ // ATT_STRIP):
                    sl = slice(r * ATT_STRIP, (r + 1) * ATT_STRIP)
                    p = jnp.exp2(sc_ref[sl, :hi] - m[sl])
                    lsum.append(jnp.sum(p, -1, keepdims=True))
                    base = c * ATT_TQ + r * ATT_STRIP
                    p_ref[j, buf, base:base + ATT_STRIP, :hi] = p.astype(BF16)
                l.append(jnp.concatenate(lsum, 0))
            acc = jnp.dot(p_ref[j, buf, :, :hi], v_ref[:hi, head], preferred_element_type=F32)
            acc = acc[:ATT_TQ] * (1.0 / l[0]) - acc[ATT_TQ:] * (lam / l[1])
            y = acc * lax.rsqrt(jnp.mean(acc * acc, -1, keepdims=True) + LN_EPS)
            o_ref[lo:lo + ATT_TQ, head] = (y * gain * (1.0 - lambda_init)).astype(BF16)


def _diff_attention(hcat, lam_vecs, norm_g, batch, seq, lambda_init):
    hp = ATT_HEADS_PER_STEP
    width = hp * LANES
    return pl.pallas_call(
        functools.partial(_attn_kernel, lambda_init=lambda_init, seq=seq),
        grid=(batch, DIFF_HEADS // hp),
        in_specs=[pl.BlockSpec((seq, width), lambda b, h: (b, h)),
                  pl.BlockSpec((seq, width), lambda b, h: (b, DIFF_QK // width + h)),
                  pl.BlockSpec((seq, width), lambda b, h: (b, 2 * DIFF_QK // width + h)),
                  _resident((4, DIFF_DH)),
                  _resident((1, DIFF_DV))],
        out_specs=pl.BlockSpec((seq, width), lambda b, h: (b, h)),
        out_shape=jax.ShapeDtypeStruct((batch * seq, DIFF_WIDTH), BF16),
        scratch_shapes=[pltpu.VMEM((hp, 2, 2, ATT_TQ, seq), F32),
                        pltpu.VMEM((hp, 2, 2 * ATT_TQ, seq), BF16)],
        compiler_params=_params(2, 24),
        name="diff_attn",
    )(hcat, hcat, hcat, lam_vecs, norm_g)


def _even_out_kernel(x_ref, attn_ref, conv_ref, wout_ref, g_ref, b_ref, o_ref):
    rows = TM // OUT_SPLIT
    ys = []
    for r in range(OUT_SPLIT):
        sl = slice(r * rows, (r + 1) * rows)
        mix = jnp.concatenate([attn_ref[sl, :], conv_ref[sl, :]], 1)
        ys.append(jnp.dot(mix, wout_ref[...], preferred_element_type=F32))
    for r in range(OUT_SPLIT):
        sl = slice(r * rows, (r + 1) * rows)
        o_ref[sl, :] = _layer_norm(ALPHA * x_ref[sl, :] + ys[r], g_ref[...], b_ref[...])


def _even_out(x, attn, hcat, w_out, g, b):
    t = x.shape[0]
    conv_col = (2 * DIFF_QK + DIFF_WIDTH) // CONV_CH
    return pl.pallas_call(
        _even_out_kernel,
        grid=(t // TM,),
        in_specs=[pl.BlockSpec((TM, D_MODEL), lambda i: (i, 0)),
                  pl.BlockSpec((TM, DIFF_WIDTH), lambda i: (i, 0)),
                  pl.BlockSpec((TM, CONV_CH), lambda i: (i, conv_col)),
                  _resident((DIFF_WIDTH + CONV_CH, D_MODEL)),
                  _resident((1, D_MODEL)), _resident((1, D_MODEL))],
        out_specs=pl.BlockSpec((TM, D_MODEL), lambda i: (i, 0)),
        out_shape=jax.ShapeDtypeStruct((t, D_MODEL), F32),
        compiler_params=_params(1, 28),
        name="even_out",
    )(x, attn, hcat, w_out, g, b)


def _odd_proj_kernel(x_ref, w_ref, cos_ref, sin_ref, ng_ref, o_ref):
    xb = x_ref[...].astype(BF16)
    cos = cos_ref[...]
    sin = sin_ref[...]
    half = RET_DK // 2
    gate0 = (2 * RET_QK + RET_VW) // 512
    for n in list(range(gate0, ODD_IN // 512)) + list(range(gate0)):
        r = jnp.dot(xb, w_ref[:, n * 512:(n + 1) * 512], preferred_element_type=F32)
        if n >= gate0:
            gain = ng_ref[:, (n - gate0) * 512:(n - gate0 + 1) * 512]
            o_ref[:, n * 512:(n + 1) * 512] = (r * jax.nn.sigmoid(r) * gain).astype(BF16)
        elif n < 2 * RET_QK // 512:
            for hh in range(512 // RET_DK):
                x1 = r[:, hh * RET_DK:hh * RET_DK + half]
                x2 = r[:, hh * RET_DK + half:(hh + 1) * RET_DK]
                o1 = x1 * cos - x2 * sin
                o2 = x2 * cos + x1 * sin
                if n >= RET_QK // 512:
                    o1 = o1 * RET_DK ** -0.5
                    o2 = o2 * RET_DK ** -0.5
                base = n * 512 + hh * RET_DK
                o_ref[:, base:base + half] = o1.astype(BF16)
                o_ref[:, base + half:base + RET_DK] = o2.astype(BF16)
        else:
            o_ref[:, n * 512:(n + 1) * 512] = r.astype(BF16)


def _odd_proj(x, w, cos, sin, norm_g, seq):
    t = x.shape[0]
    per_seq = seq // TM
    tbl = pl.BlockSpec((TM, LANES), lambda i: (i % per_seq, 0))
    return pl.pallas_call(
        _odd_proj_kernel,
        grid=(t // TM,),
        in_specs=[pl.BlockSpec((TM, D_MODEL), lambda i: (i, 0)),
                  _resident((D_MODEL, ODD_IN)), tbl, tbl, _resident((1, RET_VW))],
        out_specs=pl.BlockSpec((TM, ODD_IN), lambda i: (i, 0)),
        out_shape=jax.ShapeDtypeStruct((t, ODD_IN), BF16),
        compiler_params=_params(1, 52),
        name="odd_proj",
    )(x, w, cos, sin, norm_g)


def _retention_kernel(cd_ref, q_ref, k_ref, v_ref, gate_ref, dmat_ref, qd_ref, kd_ref,
                      o_ref, *, seq):
    hp = RET_HEADS_PER_STEP
    first = pl.program_id(1) * hp
    cd = [cd_ref[first + j] for j in range(hp)]
    dmat = [dmat_ref[j] for j in range(hp)]
    qd = [jnp.concatenate([qd_ref[j]] * (RET_DV // LANES), axis=1) for j in range(hp)]
    kd = [jnp.concatenate([kd_ref[j]] * (RET_DK // LANES), axis=1) for j in range(hp)]
    state = [jnp.zeros((RET_DK, RET_DV), F32) for _ in range(hp)]
    for c in range(seq // RET_BLOCK):
        rows = pl.ds(c * RET_BLOCK, RET_BLOCK)
        for j in range(hp):
            dk = slice(j * RET_DK, (j + 1) * RET_DK)
            dv = slice(j * RET_DV, (j + 1) * RET_DV)
            qc = q_ref[rows, dk]
            kc = k_ref[rows, dk]
            vc = v_ref[rows, dv]
            sc = lax.dot_general(qc, kc, (((1,), (1,)), ((), ())),
                                 preferred_element_type=F32) * dmat[j]
            y = jnp.dot(sc.astype(BF16), vc, preferred_element_type=F32)
            y = y + jnp.dot(qc, state[j].astype(BF16), preferred_element_type=F32) * qd[j]
            kdec = (kc.astype(F32) * kd[j]).astype(BF16)
            state[j] = state[j] * cd[j] + lax.dot_general(
                kdec, vc, (((0,), (0,)), ((), ())), preferred_element_type=F32)
            mu = jnp.mean(y, -1, keepdims=True)
            d = y - mu
            var = jnp.mean(d * d, -1, keepdims=True)
            yn = d * lax.rsqrt(var + LN_EPS)
            o_ref[rows, dv] = (gate_ref[rows, dv].astype(F32) * yn).astype(BF16)


def _retention(hcat, tables, batch, seq):
    cd, dmat, qd, kd = tables
    hp = RET_HEADS_PER_STEP
    tbl = pl.BlockSpec((hp, RET_BLOCK, LANES), lambda b, h: (h, 0, 0))
    k0 = RET_QK // (hp * RET_DK)
    v0 = 2 * RET_QK // (hp * RET_DV)
    g0 = v0 + RET_VW // (hp * RET_DV)
    return pl.pallas_call(
        functools.partial(_retention_kernel, seq=seq),
        grid=(batch, RET_HEADS // hp),
        in_specs=[pl.BlockSpec(memory_space=pltpu.SMEM),
                  pl.BlockSpec((seq, hp * RET_DK), lambda b, h: (b, h)),
                  pl.BlockSpec((seq, hp * RET_DK), lambda b, h: (b, k0 + h)),
                  pl.BlockSpec((seq, hp * RET_DV), lambda b, h: (b, v0 + h)),
                  pl.BlockSpec((seq, hp * RET_DV), lambda b, h: (b, g0 + h)),
                  pl.BlockSpec((hp, RET_BLOCK, RET_BLOCK), lambda b, h: (h, 0, 0)),
                  tbl, tbl],
        out_specs=pl.BlockSpec((seq, hp * RET_DV), lambda b, h: (b, h)),
        out_shape=jax.ShapeDtypeStruct((batch * seq, RET_VW), BF16),
        compiler_params=_params(2, 42),
        name="retention",
    )(cd, hcat, hcat, hcat, hcat, dmat, qd, kd)


def _odd_out_kernel(x_ref, y_ref, wout_ref, g_ref, b_ref, o_ref):
    rows = TM // OUT_SPLIT
    ys = [jnp.dot(y_ref[r * rows:(r + 1) * rows, :], wout_ref[...],
                  preferred_element_type=F32) for r in range(OUT_SPLIT)]
    for r in range(OUT_SPLIT):
        sl = slice(r * rows, (r + 1) * rows)
        o_ref[sl, :] = _layer_norm(ALPHA * x_ref[sl, :] + ys[r], g_ref[...], b_ref[...])


def _odd_out(x, y, w_out, g, b):
    t = x.shape[0]
    return pl.pallas_call(
        _odd_out_kernel,
        grid=(t // TM,),
        in_specs=[pl.BlockSpec((TM, D_MODEL), lambda i: (i, 0)),
                  pl.BlockSpec((TM, RET_VW), lambda i: (i, 0)),
                  _resident((RET_VW, D_MODEL)),
                  _resident((1, D_MODEL)), _resident((1, D_MODEL))],
        out_specs=pl.BlockSpec((TM, D_MODEL), lambda i: (i, 0)),
        out_shape=jax.ShapeDtypeStruct((t, D_MODEL), F32),
        compiler_params=_params(1, 38),
        name="odd_out",
    )(x, y, w_out, g, b)


def _rope_angles(seq, n_rot, theta):
    inv = np.exp(-math.log(theta) * np.arange(n_rot // 2, dtype=np.float64) * (2.0 / n_rot))
    return np.arange(seq, dtype=np.float64)[:, None] * inv[None, :]


def _even_rope_tables(seq):
    half = ROPE_DIMS // 2
    ang = _rope_angles(seq, ROPE_DIMS, ROPE_THETA)
    cos, sin = np.cos(ang), np.sin(ang)
    pad = np.zeros((seq, DIFF_DH - ROPE_DIMS))
    zeros = np.zeros((seq, half))
    c = np.concatenate([cos, cos, pad + 1.0], -1)
    sa = np.concatenate([-sin, zeros, pad], -1)
    sb = np.concatenate([zeros, sin, pad], -1)
    reps = LANES // DIFF_DH
    return tuple(jnp.asarray(np.tile(t, (1, reps)), F32) for t in (c, sa, sb))


def _odd_rope_tables(seq):
    ang = _rope_angles(seq, RET_DK, RET_ROPE_THETA)
    return jnp.asarray(np.cos(ang), F32), jnp.asarray(np.sin(ang), F32)


def _retention_tables():
    log_g = np.log1p(-np.exp2(-5.0 - np.arange(RET_HEADS, dtype=np.float64)))
    idx = np.arange(RET_BLOCK, dtype=np.float64)
    rel = idx[:, None] - idx[None, :]
    dmat = np.where(rel >= 0, np.exp(log_g[:, None, None] * np.maximum(rel, 0.0)), 0.0)
    q_decay = np.exp(log_g[:, None] * (idx + 1.0))
    k_decay = np.exp(log_g[:, None] * (RET_BLOCK - 1.0 - idx))
    cd = np.exp(log_g * RET_BLOCK)
    rep = lambda t: np.broadcast_to(t[:, :, None], (RET_HEADS, RET_BLOCK, LANES))
    return tuple(jnp.asarray(t, F32) for t in (cd, dmat, rep(q_decay), rep(k_decay)))


def kernel(x, ln_g, ln_b, ffn_w_in, ffn_w_out, even_w_in, even_w_out, diff_lambda,
           diff_norm_g, conv_w, odd_w_in, odd_w_out, ret_norm_g):
    batch, seq, _ = x.shape
    assert seq % TM == 0 and seq % ATT_TQ == 0 and DEPTH == 2
    h = x.reshape(batch * seq, D_MODEL)
    row = lambda v: v.reshape(1, -1)
    norm = lambda i, k: (row(ln_g[i, k]), row(ln_b[i, k]))
    ffn = lambda i, k: [(ffn_w_in, (i, k)), (ffn_w_out, (i, k))]

    w_in, w_out = ffn_w_in[0, 0].astype(BF16), ffn_w_out[0, 0].astype(BF16)

    h, (w_in, w_out, e_in, e_out) = _ffn_ln(
        h, w_in, w_out, *norm(0, 0),
        casts=ffn(0, 1) + [(even_w_in, (0,)), (even_w_out, (0,))])
    lambda_init = 0.8 - 0.6 * math.exp(-0.3 * 0)
    hcat, next_ffn = _even_proj(h, e_in, *_even_rope_tables(seq), conv_w[0], seq,
                                casts=ffn(1, 0))
    attn = _diff_attention(hcat, diff_lambda[0], row(diff_norm_g[0]), batch, seq, lambda_init)
    h = _even_out(h, attn, hcat, e_out, *norm(0, 1))
    h, (o_in, o_out) = _ffn_ln(h, w_in, w_out, *norm(0, 2),
                               casts=[(odd_w_in, (0,)), (odd_w_out, (0,))])

    h, last_ffn = _ffn_ln(h, *next_ffn, *norm(1, 0), casts=ffn(1, 1))
    hcat = _odd_proj(h, o_in, *_odd_rope_tables(seq), row(ret_norm_g[0]), seq)
    y = _retention(hcat, _retention_tables(), batch, seq)
    h = _odd_out(h, y, o_out, *norm(1, 1))
    h, _ = _ffn_ln(h, *last_ffn, *norm(1, 2))
    return h.reshape(batch, seq, D_MODEL)
```

```python
import functools
import math

import jax
import jax.numpy as jnp
import numpy as np
from jax import lax
from jax.experimental import pallas as pl
from jax.experimental.pallas import tpu as pltpu

F32 = jnp.float32
BF16 = jnp.bfloat16

D_MODEL = 1024
DEPTH = 2
D_FF = 2816
DIFF_HEADS = 4
DIFF_DH = 64
DIFF_DV = 128
DIFF_QK = 512
DIFF_WIDTH = 512
ROPE_THETA = 500000.0
ROPE_DIMS = 16
CONV_CH = 512
EVEN_IN = 3072
RET_HEADS = 4
RET_DK = 256
RET_DV = 512
RET_QK = 1024
RET_VW = 2048
RET_ROPE_THETA = 10000.0
ODD_IN = 6144
ALPHA = (2.0 * DEPTH) ** 0.25
LN_EPS = 1e-5
NEG_INF = -1e30
LOG2E = math.log2(math.e)

LANES = 128
MIB = 1024 * 1024
VMEM_LIMIT = 56 * MIB

TM = 1024
FF_CHUNK = 256
OUT_SPLIT = 4
FFN_ROW_GROUPS = (256, 256, 256, 256)
RET_BLOCK = 256
ATT_TQ = 256
ATT_STRIP = 16
ATT_HEADS_PER_STEP = 1
RET_HEADS_PER_STEP = 2


def _layer_norm(z, g, b):
    mu = jnp.mean(z, -1, keepdims=True)
    d = z - mu
    var = jnp.mean(d * d, -1, keepdims=True)
    return d * lax.rsqrt(var + LN_EPS) * g + b


def _params(n_axes, vmem_mib):
    assert vmem_mib * MIB <= VMEM_LIMIT
    return pltpu.CompilerParams(dimension_semantics=("arbitrary",) * n_axes,
                                vmem_limit_bytes=vmem_mib * MIB)


def _resident(shape):
    nd = len(shape)
    return pl.BlockSpec(shape, lambda *_: (0,) * nd, pipeline_mode=pl.Buffered(1))


def _cast_specs(casts, grid):
    steps = math.prod(grid)

    def step(*idx):
        lin = idx[0]
        for size, i in zip(grid[1:], idx[1:]):
            lin = lin * size + i
        return lin

    in_specs, out_specs, out_shapes = [], [], []
    for arr, prefix in casts:
        r, c = arr.shape[-2:]
        assert arr.ndim == len(prefix) + 2
        rows = min(d for d in range(16, r + 1, 16) if r % d == 0 and d * steps >= r)
        last = r // rows - 1
        in_specs.append(pl.BlockSpec(
            (None,) * len(prefix) + (rows, c),
            lambda *idx, p=prefix, n=last: p + (jnp.minimum(step(*idx), n), 0)))
        out_specs.append(pl.BlockSpec(
            (rows, c), lambda *idx, n=last: (jnp.minimum(step(*idx), n), 0)))
        out_shapes.append(jax.ShapeDtypeStruct((r, c), BF16))
    return in_specs, out_specs, out_shapes


def _cast_vmem_mib(casts, grid):
    in_specs, _, _ = _cast_specs(casts, grid)
    elems = sum(math.prod(d for d in spec.block_shape if d is not None) for spec in in_specs)
    return -(-elems * (4 + 2) * 2 // MIB)


def _do_casts(src_refs, dst_refs):
    for src, dst in zip(src_refs, dst_refs):
        dst[...] = src[...].astype(BF16)


def _ffn_kernel(x_ref, win_ref, wout_ref, g_ref, b_ref, *rest, n_cast):
    cast_src, o_ref, cast_dst, act_ref = (rest[:n_cast], rest[n_cast],
                                          rest[n_cast + 1:2 * n_cast + 1], rest[-1])
    _do_casts(cast_src, cast_dst)
    x = x_ref[...]
    xb = x.astype(BF16)
    for c in range(D_FF // FF_CHUNK):
        lo = c * FF_CHUNK
        gate = jnp.dot(xb, win_ref[:, lo:lo + FF_CHUNK], preferred_element_type=F32)
        up = jnp.dot(xb, win_ref[:, D_FF + lo:D_FF + lo + FF_CHUNK], preferred_element_type=F32)
        act_ref[:, lo:lo + FF_CHUNK] = (gate * jax.nn.sigmoid(gate) * up).astype(BF16)
    bounds = [sum(FFN_ROW_GROUPS[:r]) for r in range(len(FFN_ROW_GROUPS) + 1)]
    groups = [slice(lo, hi) for lo, hi in zip(bounds[:-1], bounds[1:])]
    ys = [jnp.dot(act_ref[sl, :], wout_ref[...], preferred_element_type=F32) for sl in groups]
    for sl, y in zip(groups, ys):
        o_ref[sl, :] = _layer_norm(ALPHA * x_ref[sl, :] + 0.5 * y, g_ref[...], b_ref[...])


def _ffn_ln(x, w_in, w_out, g, b, casts=()):
    t = x.shape[0]
    steps = t // TM
    cast_in, cast_out, cast_shapes = _cast_specs(casts, (steps,))
    out = pl.pallas_call(
        functools.partial(_ffn_kernel, n_cast=len(casts)),
        grid=(steps,),
        in_specs=[pl.BlockSpec((TM, D_MODEL), lambda i: (i, 0)),
                  _resident((D_MODEL, 2 * D_FF)),
                  _resident((D_FF, D_MODEL)),
                  _resident((1, D_MODEL)),
                  _resident((1, D_MODEL))] + cast_in,
        out_specs=[pl.BlockSpec((TM, D_MODEL), lambda i: (i, 0))] + cast_out,
        out_shape=[jax.ShapeDtypeStruct((t, D_MODEL), F32)] + cast_shapes,
        scratch_shapes=[pltpu.VMEM((TM, D_FF), BF16)],
        compiler_params=_params(1, 45 + _cast_vmem_mib(casts, (steps,))),
        name="ffn_ln",
    )(x, w_in, w_out, g, b, *[arr for arr, _ in casts])
    return out[0], out[1:]


EVEN_OUT_COLS = 2 * DIFF_QK + DIFF_WIDTH + CONV_CH
CONV_TAIL = 8


def _even_proj_kernel(x_ref, w_ref, c_ref, sa_ref, sb_ref, cw_ref, *rest, n_cast, per_seq):
    cast_src, o_ref, cast_dst, tail_ref = (rest[:n_cast], rest[n_cast],
                                           rest[n_cast + 1:2 * n_cast + 1], rest[-1])
    _do_casts(cast_src, cast_dst)
    i = pl.program_id(0)

    @pl.when(i == 0)
    def _():
        tail_ref[...] = jnp.zeros_like(tail_ref)

    xb = x_ref[...].astype(BF16)
    c = c_ref[...]
    sa = sa_ref[...]
    sb = sb_ref[...]
    proj = lambda n: jnp.dot(xb, w_ref[:, n * 512:(n + 1) * 512], preferred_element_type=F32)
    for n in range(2):
        r = proj(n)
        scale = DIFF_DH ** -0.5 * LOG2E if n == 0 else 1.0
        for j in range(512 // LANES):
            blk = r[:, j * LANES:(j + 1) * LANES]
            rot = (blk * c + pltpu.roll(blk, LANES - 8, 1) * sa
                   + pltpu.roll(blk, 8, 1) * sb)
            if n == 0:
                rot = rot * scale
            o_ref[:, n * 512 + j * LANES:n * 512 + (j + 1) * LANES] = rot.astype(BF16)
    u = proj(4) * proj(5)
    prev = jnp.where(i % per_seq == 0, jnp.zeros_like(u[:CONV_TAIL]), tail_ref[...])
    tail_ref[...] = u[TM - CONV_TAIL:]
    row = lax.broadcasted_iota(jnp.int32, (CONV_TAIL, CONV_CH), 0)
    u1 = pltpu.roll(u, 1, 0)
    u2 = pltpu.roll(u, 2, 0)
    head1 = jnp.where(row == 0, prev[CONV_TAIL - 1:], u1[:CONV_TAIL])
    head2 = jnp.where(row == 0, prev[CONV_TAIL - 2:CONV_TAIL - 1], u2[:CONV_TAIL])
    head2 = jnp.where(row == 1, prev[CONV_TAIL - 1:], head2)
    u1 = jnp.concatenate([head1, u1[CONV_TAIL:]], 0)
    u2 = jnp.concatenate([head2, u2[CONV_TAIL:]], 0)
    cw = cw_ref[...]
    conv = proj(3) * (cw[0:1] * u2 + cw[1:2] * u1 + cw[2:3] * u)
    o_ref[:, 3 * 512:] = conv.astype(BF16)
    o_ref[:, 2 * 512:3 * 512] = proj(2).astype(BF16)


def _even_proj(x, w, c, sa, sb, conv_w, seq, casts=()):
    t = x.shape[0]
    steps = t // TM
    per_seq = seq // TM
    tbl = pl.BlockSpec((TM, LANES), lambda i: (i % per_seq, 0))
    cast_in, cast_out, cast_shapes = _cast_specs(casts, (steps,))
    out = pl.pallas_call(
        functools.partial(_even_proj_kernel, n_cast=len(casts), per_seq=per_seq),
        grid=(steps,),
        in_specs=[pl.BlockSpec((TM, D_MODEL), lambda i: (i, 0)),
                  _resident((D_MODEL, EVEN_IN)), tbl, tbl, tbl,
                  _resident((3, CONV_CH))] + cast_in,
        out_specs=[pl.BlockSpec((TM, EVEN_OUT_COLS), lambda i: (i, 0))] + cast_out,
        out_shape=[jax.ShapeDtypeStruct((t, EVEN_OUT_COLS), BF16)] + cast_shapes,
        scratch_shapes=[pltpu.VMEM((CONV_TAIL, CONV_CH), F32)],
        compiler_params=_params(1, 28 + _cast_vmem_mib(casts, (steps,))),
        name="even_proj",
    )(x, w, c, sa, sb, conv_w, *[arr for arr, _ in casts])
    return out[0], out[1:]


def _attn_kernel(q_ref, k_ref, v_ref, lv_ref, g_ref, *rest, lambda_init, seq, n_cast):
    cast_src, o_ref, cast_dst, s_ref, p_ref = (rest[:n_cast], rest[n_cast],
                                               rest[n_cast + 1:2 * n_cast + 1], rest[-2], rest[-1])
    _do_casts(cast_src, cast_dst)
    lv = lv_ref[...]
    lam = (jnp.exp(jnp.sum(lv[0:1] * lv[1:2], -1, keepdims=True))
           - jnp.exp(jnp.sum(lv[2:3] * lv[3:4], -1, keepdims=True)) + lambda_init)
    lane = lax.broadcasted_iota(jnp.int32, (ATT_TQ, LANES), 1)
    row = lax.broadcasted_iota(jnp.int32, (ATT_TQ, ATT_TQ), 0)
    col = lax.broadcasted_iota(jnp.int32, (ATT_TQ, ATT_TQ), 1)
    causal = col <= row
    gain = g_ref[...]
    nt = (((1,), (1,)), ((), ()))
    n_tiles = seq // ATT_TQ
    order = [t for pair in zip(range(n_tiles - 1, -1, -1), range(n_tiles)) for t in pair][:n_tiles]
    for pos, i in enumerate(order):
        for j in range(ATT_HEADS_PER_STEP):
            head = slice(j * LANES, (j + 1) * LANES)
            buf = pos % 2
            lo = i * ATT_TQ
            q = q_ref[lo:lo + ATT_TQ, head]
            zero = jnp.zeros_like(q)
            qs = (jnp.where(lane < DIFF_DH, q, zero), jnp.where(lane >= DIFF_DH, q, zero))
            hi = lo + ATT_TQ
            l = []
            for c, qc in enumerate(qs):
                s = lax.dot_general(qc, k_ref[:hi, head], nt, preferred_element_type=F32)
                sc_ref = s_ref.at[j, buf, c]
                if i > 0:
                    sc_ref[:, :lo] = s[:, :lo]
                sc_ref[:, lo:hi] = jnp.where(causal, s[:, lo:], NEG_INF)
                m = jnp.max(sc_ref[:, :hi], -1, keepdims=True)
                lsum = []
                for r in range(ATT_TQ // ATT_STRIP):
                    sl = slice(r * ATT_STRIP, (r + 1) * ATT_STRIP)
                    p = jnp.exp2(sc_ref[sl, :hi] - m[sl])
                    lsum.append(jnp.sum(p, -1, keepdims=True))
                    base = c * ATT_TQ + r * ATT_STRIP
                    p_ref[j, buf, base:base + ATT_STRIP, :hi] = p.astype(BF16)
                l.append(jnp.concatenate(lsum, 0))
            acc = jnp.dot(p_ref[j, buf, :, :hi], v_ref[:hi, head], preferred_element_type=F32)
            acc = acc[:ATT_TQ] * (1.0 / l[0]) - acc[ATT_TQ:] * (lam / l[1])
            y = acc * lax.rsqrt(jnp.mean(acc * acc, -1, keepdims=True) + LN_EPS)
            o_ref[lo:lo + ATT_TQ, head] = (y * gain * (1.0 - lambda_init)).astype(BF16)


def _diff_attention(hcat, lam_vecs, norm_g, batch, seq, lambda_init, casts=()):
    hp = ATT_HEADS_PER_STEP
    width = hp * LANES
    grid = (batch, DIFF_HEADS // hp)
    cast_in, cast_out, cast_shapes = _cast_specs(casts, grid)
    out = pl.pallas_call(
        functools.partial(_attn_kernel, lambda_init=lambda_init, seq=seq, n_cast=len(casts)),
        grid=grid,
        in_specs=[pl.BlockSpec((seq, width), lambda b, h: (b, h)),
                  pl.BlockSpec((seq, width), lambda b, h: (b, DIFF_QK // width + h)),
                  pl.BlockSpec((seq, width), lambda b, h: (b, 2 * DIFF_QK // width + h)),
                  _resident((4, DIFF_DH)),
                  _resident((1, DIFF_DV))] + cast_in,
        out_specs=[pl.BlockSpec((seq, width), lambda b, h: (b, h))] + cast_out,
        out_shape=[jax.ShapeDtypeStruct((batch * seq, DIFF_WIDTH), BF16)] + cast_shapes,
        scratch_shapes=[pltpu.VMEM((hp, 2, 2, ATT_TQ, seq), F32),
                        pltpu.VMEM((hp, 2, 2 * ATT_TQ, seq), BF16)],
        compiler_params=_params(2, 24 + _cast_vmem_mib(casts, grid)),
        name="diff_attn",
    )(hcat, hcat, hcat, lam_vecs, norm_g, *[arr for arr, _ in casts])
    return out[0], out[1:]


def _even_out_kernel(x_ref, attn_ref, conv_ref, wout_ref, g_ref, b_ref, o_ref):
    rows = TM // OUT_SPLIT
    ys = []
    for r in range(OUT_SPLIT):
        sl = slice(r * rows, (r + 1) * rows)
        mix = jnp.concatenate([attn_ref[sl, :], conv_ref[sl, :]], 1)
        ys.append(jnp.dot(mix, wout_ref[...], preferred_element_type=F32))
    for r in range(OUT_SPLIT):
        sl = slice(r * rows, (r + 1) * rows)
        o_ref[sl, :] = _layer_norm(ALPHA * x_ref[sl, :] + ys[r], g_ref[...], b_ref[...])


def _even_out(x, attn, hcat, w_out, g, b):
    t = x.shape[0]
    conv_col = (2 * DIFF_QK + DIFF_WIDTH) // CONV_CH
    return pl.pallas_call(
        _even_out_kernel,
        grid=(t // TM,),
        in_specs=[pl.BlockSpec((TM, D_MODEL), lambda i: (i, 0)),
                  pl.BlockSpec((TM, DIFF_WIDTH), lambda i: (i, 0)),
                  pl.BlockSpec((TM, CONV_CH), lambda i: (i, conv_col)),
                  _resident((DIFF_WIDTH + CONV_CH, D_MODEL)),
                  _resident((1, D_MODEL)), _resident((1, D_MODEL))],
        out_specs=pl.BlockSpec((TM, D_MODEL), lambda i: (i, 0)),
        out_shape=jax.ShapeDtypeStruct((t, D_MODEL), F32),
        compiler_params=_params(1, 24),
        name="even_out",
    )(x, attn, hcat, w_out, g, b)


def _odd_proj_kernel(x_ref, w_ref, cos_ref, sin_ref, ng_ref, o_ref):
    xb = x_ref[...].astype(BF16)
    cos = cos_ref[...]
    sin = sin_ref[...]
    half = RET_DK // 2
    gate0 = (2 * RET_QK + RET_VW) // 512
    for n in list(range(gate0, ODD_IN // 512)) + list(range(gate0)):
        r = jnp.dot(xb, w_ref[:, n * 512:(n + 1) * 512], preferred_element_type=F32)
        if n >= gate0:
            gain = ng_ref[:, (n - gate0) * 512:(n - gate0 + 1) * 512]
            o_ref[:, n * 512:(n + 1) * 512] = (r * jax.nn.sigmoid(r) * gain).astype(BF16)
        elif n < 2 * RET_QK // 512:
            for hh in range(512 // RET_DK):
                x1 = r[:, hh * RET_DK:hh * RET_DK + half]
                x2 = r[:, hh * RET_DK + half:(hh + 1) * RET_DK]
                o1 = x1 * cos - x2 * sin
                o2 = x2 * cos + x1 * sin
                if n >= RET_QK // 512:
                    o1 = o1 * RET_DK ** -0.5
                    o2 = o2 * RET_DK ** -0.5
                base = n * 512 + hh * RET_DK
                o_ref[:, base:base + half] = o1.astype(BF16)
                o_ref[:, base + half:base + RET_DK] = o2.astype(BF16)
        else:
            o_ref[:, n * 512:(n + 1) * 512] = r.astype(BF16)


def _odd_proj(x, w, cos, sin, norm_g, seq):
    t = x.shape[0]
    per_seq = seq // TM
    tbl = pl.BlockSpec((TM, LANES), lambda i: (i % per_seq, 0))
    return pl.pallas_call(
        _odd_proj_kernel,
        grid=(t // TM,),
        in_specs=[pl.BlockSpec((TM, D_MODEL), lambda i: (i, 0)),
                  _resident((D_MODEL, ODD_IN)), tbl, tbl, _resident((1, RET_VW))],
        out_specs=pl.BlockSpec((TM, ODD_IN), lambda i: (i, 0)),
        out_shape=jax.ShapeDtypeStruct((t, ODD_IN), BF16),
        compiler_params=_params(1, 50),
        name="odd_proj",
    )(x, w, cos, sin, norm_g)


def _retention_kernel(cd_ref, q_ref, k_ref, v_ref, gate_ref, dmat_ref, qd_ref, kd_ref,
                      o_ref, *, seq):
    hp = RET_HEADS_PER_STEP
    first = pl.program_id(1) * hp
    cd = [cd_ref[first + j] for j in range(hp)]
    dmat = [dmat_ref[j] for j in range(hp)]
    qd = [jnp.concatenate([qd_ref[j]] * (RET_DV // LANES), axis=1) for j in range(hp)]
    kd = [jnp.concatenate([kd_ref[j]] * (RET_DK // LANES), axis=1) for j in range(hp)]
    state = [jnp.zeros((RET_DK, RET_DV), F32) for _ in range(hp)]
    for c in range(seq // RET_BLOCK):
        rows = pl.ds(c * RET_BLOCK, RET_BLOCK)
        for j in range(hp):
            dk = slice(j * RET_DK, (j + 1) * RET_DK)
            dv = slice(j * RET_DV, (j + 1) * RET_DV)
            qc = q_ref[rows, dk]
            kc = k_ref[rows, dk]
            vc = v_ref[rows, dv]
            sc = lax.dot_general(qc, kc, (((1,), (1,)), ((), ())),
                                 preferred_element_type=F32) * dmat[j]
            y = jnp.dot(sc.astype(BF16), vc, preferred_element_type=F32)
            y = y + jnp.dot(qc, state[j].astype(BF16), preferred_element_type=F32) * qd[j]
            kdec = (kc.astype(F32) * kd[j]).astype(BF16)
            state[j] = state[j] * cd[j] + lax.dot_general(
                kdec, vc, (((0,), (0,)), ((), ())), preferred_element_type=F32)
            mu = jnp.mean(y, -1, keepdims=True)
            d = y - mu
            var = jnp.mean(d * d, -1, keepdims=True)
            yn = d * lax.rsqrt(var + LN_EPS)
            o_ref[rows, dv] = (gate_ref[rows, dv].astype(F32) * yn).astype(BF16)


def _retention(hcat, tables, batch, seq):
    cd, dmat, qd, kd = tables
    hp = RET_HEADS_PER_STEP
    tbl = pl.BlockSpec((hp, RET_BLOCK, LANES), lambda b, h: (h, 0, 0))
    k0 = RET_QK // (hp * RET_DK)
    v0 = 2 * RET_QK // (hp * RET_DV)
    g0 = v0 + RET_VW // (hp * RET_DV)
    return pl.pallas_call(
        functools.partial(_retention_kernel, seq=seq),
        grid=(batch, RET_HEADS // hp),
        in_specs=[pl.BlockSpec(memory_space=pltpu.SMEM),
                  pl.BlockSpec((seq, hp * RET_DK), lambda b, h: (b, h)),
                  pl.BlockSpec((seq, hp * RET_DK), lambda b, h: (b, k0 + h)),
                  pl.BlockSpec((seq, hp * RET_DV), lambda b, h: (b, v0 + h)),
                  pl.BlockSpec((seq, hp * RET_DV), lambda b, h: (b, g0 + h)),
                  pl.BlockSpec((hp, RET_BLOCK, RET_BLOCK), lambda b, h: (h, 0, 0)),
                  tbl, tbl],
        out_specs=pl.BlockSpec((seq, hp * RET_DV), lambda b, h: (b, h)),
        out_shape=jax.ShapeDtypeStruct((batch * seq, RET_VW), BF16),
        compiler_params=_params(2, 38),
        name="retention",
    )(cd, hcat, hcat, hcat, hcat, dmat, qd, kd)


def _odd_out_kernel(x_ref, y_ref, wout_ref, g_ref, b_ref, o_ref):
    rows = TM // OUT_SPLIT
    ys = [jnp.dot(y_ref[r * rows:(r + 1) * rows, :], wout_ref[...],
                  preferred_element_type=F32) for r in range(OUT_SPLIT)]
    for r in range(OUT_SPLIT):
        sl = slice(r * rows, (r + 1) * rows)
        o_ref[sl, :] = _layer_norm(ALPHA * x_ref[sl, :] + ys[r], g_ref[...], b_ref[...])


def _odd_out(x, y, w_out, g, b):
    t = x.shape[0]
    return pl.pallas_call(
        _odd_out_kernel,
        grid=(t // TM,),
        in_specs=[pl.BlockSpec((TM, D_MODEL), lambda i: (i, 0)),
                  pl.BlockSpec((TM, RET_VW), lambda i: (i, 0)),
                  _resident((RET_VW, D_MODEL)),
                  _resident((1, D_MODEL)), _resident((1, D_MODEL))],
        out_specs=pl.BlockSpec((TM, D_MODEL), lambda i: (i, 0)),
        out_shape=jax.ShapeDtypeStruct((t, D_MODEL), F32),
        compiler_params=_params(1, 32),
        name="odd_out",
    )(x, y, w_out, g, b)


def _rope_angles(seq, n_rot, theta):
    inv = np.exp(-math.log(theta) * np.arange(n_rot // 2, dtype=np.float64) * (2.0 / n_rot))
    return np.arange(seq, dtype=np.float64)[:, None] * inv[None, :]


def _even_rope_tables(seq):
    half = ROPE_DIMS // 2
    ang = _rope_angles(seq, ROPE_DIMS, ROPE_THETA)
    cos, sin = np.cos(ang), np.sin(ang)
    pad = np.zeros((seq, DIFF_DH - ROPE_DIMS))
    zeros = np.zeros((seq, half))
    c = np.concatenate([cos, cos, pad + 1.0], -1)
    sa = np.concatenate([-sin, zeros, pad], -1)
    sb = np.concatenate([zeros, sin, pad], -1)
    reps = LANES // DIFF_DH
    return tuple(jnp.asarray(np.tile(t, (1, reps)), F32) for t in (c, sa, sb))


def _odd_rope_tables(seq):
    ang = _rope_angles(seq, RET_DK, RET_ROPE_THETA)
    return jnp.asarray(np.cos(ang), F32), jnp.asarray(np.sin(ang), F32)


def _retention_tables():
    log_g = np.log1p(-np.exp2(-5.0 - np.arange(RET_HEADS, dtype=np.float64)))
    idx = np.arange(RET_BLOCK, dtype=np.float64)
    rel = idx[:, None] - idx[None, :]
    dmat = np.where(rel >= 0, np.exp(log_g[:, None, None] * np.maximum(rel, 0.0)), 0.0)
    q_decay = np.exp(log_g[:, None] * (idx + 1.0))
    k_decay = np.exp(log_g[:, None] * (RET_BLOCK - 1.0 - idx))
    cd = np.exp(log_g * RET_BLOCK)
    rep = lambda t: np.broadcast_to(t[:, :, None], (RET_HEADS, RET_BLOCK, LANES))
    return tuple(jnp.asarray(t, F32) for t in (cd, dmat, rep(q_decay), rep(k_decay)))


def kernel(x, ln_g, ln_b, ffn_w_in, ffn_w_out, even_w_in, even_w_out, diff_lambda,
           diff_norm_g, conv_w, odd_w_in, odd_w_out, ret_norm_g):
    batch, seq, _ = x.shape
    assert seq % TM == 0 and seq % ATT_TQ == 0 and DEPTH == 2
    h = x.reshape(batch * seq, D_MODEL)
    row = lambda v: v.reshape(1, -1)
    norm = lambda i, k: (row(ln_g[i, k]), row(ln_b[i, k]))
    ffn = lambda i, k: [(ffn_w_in, (i, k)), (ffn_w_out, (i, k))]

    w_in, w_out = ffn_w_in[0, 0].astype(BF16), ffn_w_out[0, 0].astype(BF16)

    h, (e_in, e_out) = _ffn_ln(h, w_in, w_out, *norm(0, 0),
                               casts=[(even_w_in, (0,)), (even_w_out, (0,))])
    lambda_init = 0.8 - 0.6 * math.exp(-0.3 * 0)
    hcat, ffn01 = _even_proj(h, e_in, *_even_rope_tables(seq), conv_w[0], seq, casts=ffn(0, 1))
    attn, later = _diff_attention(
        hcat, diff_lambda[0], row(diff_norm_g[0]), batch, seq, lambda_init,
        casts=ffn(1, 0) + [(odd_w_in, (0,)), (odd_w_out, (0,))] + ffn(1, 1))
    ffn10, (o_in, o_out), ffn11 = later[:2], later[2:4], later[4:]
    h = _even_out(h, attn, hcat, e_out, *norm(0, 1))
    h, _ = _ffn_ln(h, *ffn01, *norm(0, 2))

    h, _ = _ffn_ln(h, *ffn10, *norm(1, 0))
    hcat = _odd_proj(h, o_in, *_odd_rope_tables(seq), row(ret_norm_g[0]), seq)
    y = _retention(hcat, _retention_tables(), batch, seq)
    h = _odd_out(h, y, o_out, *norm(1, 1))
    h, _ = _ffn_ln(h, *ffn11, *norm(1, 2))
    return h.reshape(batch, seq, D_MODEL)
```

```python
import functools
import math

import jax
import jax.numpy as jnp
import numpy as np
from jax import lax
from jax.experimental import pallas as pl
from jax.experimental.pallas import tpu as pltpu

F32 = jnp.float32
BF16 = jnp.bfloat16

D_MODEL = 1024
DEPTH = 2
D_FF = 2816
DIFF_HEADS = 4
DIFF_DH = 64
DIFF_DV = 128
DIFF_QK = 512
DIFF_WIDTH = 512
ROPE_THETA = 500000.0
ROPE_DIMS = 16
CONV_CH = 512
EVEN_IN = 3072
RET_HEADS = 4
RET_DK = 256
RET_DV = 512
RET_QK = 1024
RET_VW = 2048
RET_ROPE_THETA = 10000.0
ODD_IN = 6144
ALPHA = (2.0 * DEPTH) ** 0.25
LN_EPS = 1e-5
NEG_INF = -1e30
LOG2E = math.log2(math.e)

LANES = 128
MIB = 1024 * 1024
VMEM_LIMIT = 56 * MIB

TM = 1024
FF_CHUNK = 256
OUT_SPLIT = 4
FFN_ROW_GROUPS = (256, 256, 256, 256)
RET_BLOCK = 256
ATT_TQ = 256
ATT_STRIP = 16
ATT_HEADS_PER_STEP = 1
RET_HEADS_PER_STEP = 2


def _layer_norm(z, g, b):
    mu = jnp.mean(z, -1, keepdims=True)
    d = z - mu
    var = jnp.mean(d * d, -1, keepdims=True)
    return d * lax.rsqrt(var + LN_EPS) * g + b


def _params(n_axes, vmem_mib):
    assert vmem_mib * MIB <= VMEM_LIMIT
    return pltpu.CompilerParams(dimension_semantics=("arbitrary",) * n_axes,
                                vmem_limit_bytes=vmem_mib * MIB)


def _resident(shape):
    nd = len(shape)
    return pl.BlockSpec(shape, lambda *_: (0,) * nd, pipeline_mode=pl.Buffered(1))


def _cast_specs(casts, grid):
    steps = math.prod(grid)

    def step(*idx):
        lin = idx[0]
        for size, i in zip(grid[1:], idx[1:]):
            lin = lin * size + i
        return lin

    in_specs, out_specs, out_shapes = [], [], []
    for arr, prefix in casts:
        r, c = arr.shape[-2:]
        assert arr.ndim == len(prefix) + 2
        rows = min(d for d in range(16, r + 1, 16) if r % d == 0 and d * steps >= r)
        last = r // rows - 1
        in_specs.append(pl.BlockSpec(
            (None,) * len(prefix) + (rows, c),
            lambda *idx, p=prefix, n=last: p + (jnp.minimum(step(*idx), n), 0)))
        out_specs.append(pl.BlockSpec(
            (rows, c), lambda *idx, n=last: (jnp.minimum(step(*idx), n), 0)))
        out_shapes.append(jax.ShapeDtypeStruct((r, c), BF16))
    return in_specs, out_specs, out_shapes


def _cast_vmem_mib(casts, grid):
    in_specs, _, _ = _cast_specs(casts, grid)
    elems = sum(math.prod(d for d in spec.block_shape if d is not None) for spec in in_specs)
    return -(-elems * (4 + 2) * 2 // MIB)


def _do_casts(src_refs, dst_refs):
    for src, dst in zip(src_refs, dst_refs):
        dst[...] = src[...].astype(BF16)


W_STREAM_STEPS = 16


def _stream_cast(src_hbm, dst_ref, stage_ref, sem_ref):
    rows = stage_ref.shape[1]
    n = src_hbm.shape[0] // rows
    copy = lambda c: pltpu.make_async_copy(src_hbm.at[pl.ds(c * rows, rows), :],
                                           stage_ref.at[c % 2], sem_ref.at[c % 2])
    copy(0).start()
    for c in range(n):
        if c + 1 < n:
            copy(c + 1).start()
        copy(c).wait()
        dst_ref[c * rows:(c + 1) * rows, :] = stage_ref[c % 2].astype(BF16)


def _ffn_kernel(x_ref, win_ref, wout_ref, g_ref, b_ref, *rest, n_cast, stream):
    cast_src, o_ref, cast_dst = rest[:n_cast], rest[n_cast], rest[n_cast + 1:2 * n_cast + 1]
    act_ref = rest[2 * n_cast + 1]
    if stream is not None:
        win_bf, wout_bf, stage_in, stage_out, sems = rest[2 * n_cast + 2:]

        @pl.when(pl.program_id(0) == 0)
        def _():
            _stream_cast(win_ref.at[stream], win_bf, stage_in, sems.at[0])
            _stream_cast(wout_ref.at[stream], wout_bf, stage_out, sems.at[1])

        win_ref, wout_ref = win_bf, wout_bf
    _do_casts(cast_src, cast_dst)
    x = x_ref[...]
    xb = x.astype(BF16)
    for c in range(D_FF // FF_CHUNK):
        lo = c * FF_CHUNK
        gate = jnp.dot(xb, win_ref[:, lo:lo + FF_CHUNK], preferred_element_type=F32)
        up = jnp.dot(xb, win_ref[:, D_FF + lo:D_FF + lo + FF_CHUNK], preferred_element_type=F32)
        act_ref[:, lo:lo + FF_CHUNK] = (gate * jax.nn.sigmoid(gate) * up).astype(BF16)
    bounds = [sum(FFN_ROW_GROUPS[:r]) for r in range(len(FFN_ROW_GROUPS) + 1)]
    groups = [slice(lo, hi) for lo, hi in zip(bounds[:-1], bounds[1:])]
    ys = [jnp.dot(act_ref[sl, :], wout_ref[...], preferred_element_type=F32) for sl in groups]
    for sl, y in zip(groups, ys):
        o_ref[sl, :] = _layer_norm(ALPHA * x_ref[sl, :] + 0.5 * y, g_ref[...], b_ref[...])


def _ffn_ln(x, w_in, w_out, g, b, casts=(), stream=None):
    t = x.shape[0]
    steps = t // TM
    cast_in, cast_out, cast_shapes = _cast_specs(casts, (steps,))
    scratch = [pltpu.VMEM((TM, D_FF), BF16)]
    if stream is None:
        weights = [_resident((D_MODEL, 2 * D_FF)), _resident((D_FF, D_MODEL))]
        stage_mib = 0
    else:
        weights = [pl.BlockSpec(memory_space=pl.ANY)] * 2
        rows_in, rows_out = D_MODEL // W_STREAM_STEPS, D_FF // W_STREAM_STEPS
        scratch += [pltpu.VMEM((D_MODEL, 2 * D_FF), BF16), pltpu.VMEM((D_FF, D_MODEL), BF16),
                    pltpu.VMEM((2, rows_in, 2 * D_FF), F32), pltpu.VMEM((2, rows_out, D_MODEL), F32),
                    pltpu.SemaphoreType.DMA((2, 2))]
        stage_mib = -(-2 * 4 * (rows_in * 2 * D_FF + rows_out * D_MODEL) // MIB)
    out = pl.pallas_call(
        functools.partial(_ffn_kernel, n_cast=len(casts), stream=stream),
        grid=(steps,),
        in_specs=[pl.BlockSpec((TM, D_MODEL), lambda i: (i, 0))] + weights
                 + [_resident((1, D_MODEL)), _resident((1, D_MODEL))] + cast_in,
        out_specs=[pl.BlockSpec((TM, D_MODEL), lambda i: (i, 0))] + cast_out,
        out_shape=[jax.ShapeDtypeStruct((t, D_MODEL), F32)] + cast_shapes,
        scratch_shapes=scratch,
        compiler_params=_params(1, 45 + stage_mib + _cast_vmem_mib(casts, (steps,))),
        name="ffn_ln",
    )(x, w_in, w_out, g, b, *[arr for arr, _ in casts])
    return out[0], out[1:]


EVEN_OUT_COLS = 2 * DIFF_QK + DIFF_WIDTH + CONV_CH
CONV_TAIL = 8


def _even_proj_kernel(x_ref, w_ref, c_ref, sa_ref, sb_ref, cw_ref, *rest, n_cast, per_seq):
    cast_src, o_ref, cast_dst, tail_ref = (rest[:n_cast], rest[n_cast],
                                           rest[n_cast + 1:2 * n_cast + 1], rest[-1])
    _do_casts(cast_src, cast_dst)
    i = pl.program_id(0)

    @pl.when(i == 0)
    def _():
        tail_ref[...] = jnp.zeros_like(tail_ref)

    xb = x_ref[...].astype(BF16)
    c = c_ref[...]
    sa = sa_ref[...]
    sb = sb_ref[...]
    proj = lambda n: jnp.dot(xb, w_ref[:, n * 512:(n + 1) * 512], preferred_element_type=F32)
    for n in range(2):
        r = proj(n)
        scale = DIFF_DH ** -0.5 * LOG2E if n == 0 else 1.0
        for j in range(512 // LANES):
            blk = r[:, j * LANES:(j + 1) * LANES]
            rot = (blk * c + pltpu.roll(blk, LANES - 8, 1) * sa
                   + pltpu.roll(blk, 8, 1) * sb)
            if n == 0:
                rot = rot * scale
            o_ref[:, n * 512 + j * LANES:n * 512 + (j + 1) * LANES] = rot.astype(BF16)
    u = proj(4) * proj(5)
    prev = jnp.where(i % per_seq == 0, jnp.zeros_like(u[:CONV_TAIL]), tail_ref[...])
    tail_ref[...] = u[TM - CONV_TAIL:]
    row = lax.broadcasted_iota(jnp.int32, (CONV_TAIL, CONV_CH), 0)
    u1 = pltpu.roll(u, 1, 0)
    u2 = pltpu.roll(u, 2, 0)
    head1 = jnp.where(row == 0, prev[CONV_TAIL - 1:], u1[:CONV_TAIL])
    head2 = jnp.where(row == 0, prev[CONV_TAIL - 2:CONV_TAIL - 1], u2[:CONV_TAIL])
    head2 = jnp.where(row == 1, prev[CONV_TAIL - 1:], head2)
    u1 = jnp.concatenate([head1, u1[CONV_TAIL:]], 0)
    u2 = jnp.concatenate([head2, u2[CONV_TAIL:]], 0)
    cw = cw_ref[...]
    conv = proj(3) * (cw[0:1] * u2 + cw[1:2] * u1 + cw[2:3] * u)
    o_ref[:, 3 * 512:] = conv.astype(BF16)
    o_ref[:, 2 * 512:3 * 512] = proj(2).astype(BF16)


def _even_proj(x, w, c, sa, sb, conv_w, seq, casts=()):
    t = x.shape[0]
    steps = t // TM
    per_seq = seq // TM
    tbl = pl.BlockSpec((TM, LANES), lambda i: (i % per_seq, 0))
    cast_in, cast_out, cast_shapes = _cast_specs(casts, (steps,))
    out = pl.pallas_call(
        functools.partial(_even_proj_kernel, n_cast=len(casts), per_seq=per_seq),
        grid=(steps,),
        in_specs=[pl.BlockSpec((TM, D_MODEL), lambda i: (i, 0)),
                  _resident((D_MODEL, EVEN_IN)), tbl, tbl, tbl,
                  _resident((3, CONV_CH))] + cast_in,
        out_specs=[pl.BlockSpec((TM, EVEN_OUT_COLS), lambda i: (i, 0))] + cast_out,
        out_shape=[jax.ShapeDtypeStruct((t, EVEN_OUT_COLS), BF16)] + cast_shapes,
        scratch_shapes=[pltpu.VMEM((CONV_TAIL, CONV_CH), F32)],
        compiler_params=_params(1, 28 + _cast_vmem_mib(casts, (steps,))),
        name="even_proj",
    )(x, w, c, sa, sb, conv_w, *[arr for arr, _ in casts])
    return out[0], out[1:]


def _attn_kernel(q_ref, k_ref, v_ref, lv_ref, g_ref, *rest, lambda_init, seq, n_cast):
    cast_src, o_ref, cast_dst, s_ref, p_ref = (rest[:n_cast], rest[n_cast],
                                               rest[n_cast + 1:2 * n_cast + 1], rest[-2], rest[-1])
    _do_casts(cast_src, cast_dst)
    lv = lv_ref[...]
    lam = (jnp.exp(jnp.sum(lv[0:1] * lv[1:2], -1, keepdims=True))
           - jnp.exp(jnp.sum(lv[2:3] * lv[3:4], -1, keepdims=True)) + lambda_init)
    lane = lax.broadcasted_iota(jnp.int32, (ATT_TQ, LANES), 1)
    row = lax.broadcasted_iota(jnp.int32, (ATT_TQ, ATT_TQ), 0)
    col = lax.broadcasted_iota(jnp.int32, (ATT_TQ, ATT_TQ), 1)
    causal = col <= row
    gain = g_ref[...]
    nt = (((1,), (1,)), ((), ()))
    n_tiles = seq // ATT_TQ
    order = [t for pair in zip(range(n_tiles - 1, -1, -1), range(n_tiles)) for t in pair][:n_tiles]
    for pos, i in enumerate(order):
        for j in range(ATT_HEADS_PER_STEP):
            head = slice(j * LANES, (j + 1) * LANES)
            buf = pos % 2
            lo = i * ATT_TQ
            q = q_ref[lo:lo + ATT_TQ, head]
            zero = jnp.zeros_like(q)
            qs = (jnp.where(lane < DIFF_DH, q, zero), jnp.where(lane >= DIFF_DH, q, zero))
            hi = lo + ATT_TQ
            l = []
            for c, qc in enumerate(qs):
                s = lax.dot_general(qc, k_ref[:hi, head], nt, preferred_element_type=F32)
                sc_ref = s_ref.at[j, buf, c]
                if i > 0:
                    sc_ref[:, :lo] = s[:, :lo]
                sc_ref[:, lo:hi] = jnp.where(causal, s[:, lo:], NEG_INF)
                m = jnp.max(sc_ref[:, :hi], -1, keepdims=True)
                lsum = []
                for r in range(ATT_TQ // ATT_STRIP):
                    sl = slice(r * ATT_STRIP, (r + 1) * ATT_STRIP)
                    p = jnp.exp2(sc_ref[sl, :hi] - m[sl])
                    lsum.append(jnp.sum(p, -1, keepdims=True))
                    base = c * ATT_TQ + r * ATT_STRIP
                    p_ref[j, buf, base:base + ATT_STRIP, :hi] = p.astype(BF16)
                l.append(jnp.concatenate(lsum, 0))
            acc = jnp.dot(p_ref[j, buf, :, :hi], v_ref[:hi, head], preferred_element_type=F32)
            acc = acc[:ATT_TQ] * (1.0 / l[0]) - acc[ATT_TQ:] * (lam / l[1])
            y = acc * lax.rsqrt(jnp.mean(acc * acc, -1, keepdims=True) + LN_EPS)
            o_ref[lo:lo + ATT_TQ, head] = (y * gain * (1.0 - lambda_init)).astype(BF16)


def _diff_attention(hcat, lam_vecs, norm_g, batch, seq, lambda_init, casts=()):
    hp = ATT_HEADS_PER_STEP
    width = hp * LANES
    grid = (batch, DIFF_HEADS // hp)
    cast_in, cast_out, cast_shapes = _cast_specs(casts, grid)
    out = pl.pallas_call(
        functools.partial(_attn_kernel, lambda_init=lambda_init, seq=seq, n_cast=len(casts)),
        grid=grid,
        in_specs=[pl.BlockSpec((seq, width), lambda b, h: (b, h)),
                  pl.BlockSpec((seq, width), lambda b, h: (b, DIFF_QK // width + h)),
                  pl.BlockSpec((seq, width), lambda b, h: (b, 2 * DIFF_QK // width + h)),
                  _resident((4, DIFF_DH)),
                  _resident((1, DIFF_DV))] + cast_in,
        out_specs=[pl.BlockSpec((seq, width), lambda b, h: (b, h))] + cast_out,
        out_shape=[jax.ShapeDtypeStruct((batch * seq, DIFF_WIDTH), BF16)] + cast_shapes,
        scratch_shapes=[pltpu.VMEM((hp, 2, 2, ATT_TQ, seq), F32),
                        pltpu.VMEM((hp, 2, 2 * ATT_TQ, seq), BF16)],
        compiler_params=_params(2, 24 + _cast_vmem_mib(casts, grid)),
        name="diff_attn",
    )(hcat, hcat, hcat, lam_vecs, norm_g, *[arr for arr, _ in casts])
    return out[0], out[1:]


def _even_out_kernel(x_ref, attn_ref, conv_ref, wout_ref, g_ref, b_ref, o_ref):
    rows = TM // OUT_SPLIT
    ys = []
    for r in range(OUT_SPLIT):
        sl = slice(r * rows, (r + 1) * rows)
        mix = jnp.concatenate([attn_ref[sl, :], conv_ref[sl, :]], 1)
        ys.append(jnp.dot(mix, wout_ref[...], preferred_element_type=F32))
    for r in range(OUT_SPLIT):
        sl = slice(r * rows, (r + 1) * rows)
        o_ref[sl, :] = _layer_norm(ALPHA * x_ref[sl, :] + ys[r], g_ref[...], b_ref[...])


def _even_out(x, attn, hcat, w_out, g, b):
    t = x.shape[0]
    conv_col = (2 * DIFF_QK + DIFF_WIDTH) // CONV_CH
    return pl.pallas_call(
        _even_out_kernel,
        grid=(t // TM,),
        in_specs=[pl.BlockSpec((TM, D_MODEL), lambda i: (i, 0)),
                  pl.BlockSpec((TM, DIFF_WIDTH), lambda i: (i, 0)),
                  pl.BlockSpec((TM, CONV_CH), lambda i: (i, conv_col)),
                  _resident((DIFF_WIDTH + CONV_CH, D_MODEL)),
                  _resident((1, D_MODEL)), _resident((1, D_MODEL))],
        out_specs=pl.BlockSpec((TM, D_MODEL), lambda i: (i, 0)),
        out_shape=jax.ShapeDtypeStruct((t, D_MODEL), F32),
        compiler_params=_params(1, 24),
        name="even_out",
    )(x, attn, hcat, w_out, g, b)


def _odd_proj_kernel(x_ref, w_ref, cos_ref, sin_ref, ng_ref, o_ref):
    xb = x_ref[...].astype(BF16)
    cos = cos_ref[...]
    sin = sin_ref[...]
    half = RET_DK // 2
    gate0 = (2 * RET_QK + RET_VW) // 512
    for n in list(range(gate0, ODD_IN // 512)) + list(range(gate0)):
        r = jnp.dot(xb, w_ref[:, n * 512:(n + 1) * 512], preferred_element_type=F32)
        if n >= gate0:
            gain = ng_ref[:, (n - gate0) * 512:(n - gate0 + 1) * 512]
            o_ref[:, n * 512:(n + 1) * 512] = (r * jax.nn.sigmoid(r) * gain).astype(BF16)
        elif n < 2 * RET_QK // 512:
            for hh in range(512 // RET_DK):
                x1 = r[:, hh * RET_DK:hh * RET_DK + half]
                x2 = r[:, hh * RET_DK + half:(hh + 1) * RET_DK]
                o1 = x1 * cos - x2 * sin
                o2 = x2 * cos + x1 * sin
                if n >= RET_QK // 512:
                    o1 = o1 * RET_DK ** -0.5
                    o2 = o2 * RET_DK ** -0.5
                base = n * 512 + hh * RET_DK
                o_ref[:, base:base + half] = o1.astype(BF16)
                o_ref[:, base + half:base + RET_DK] = o2.astype(BF16)
        else:
            o_ref[:, n * 512:(n + 1) * 512] = r.astype(BF16)


def _odd_proj(x, w, cos, sin, norm_g, seq):
    t = x.shape[0]
    per_seq = seq // TM
    tbl = pl.BlockSpec((TM, LANES), lambda i: (i % per_seq, 0))
    return pl.pallas_call(
        _odd_proj_kernel,
        grid=(t // TM,),
        in_specs=[pl.BlockSpec((TM, D_MODEL), lambda i: (i, 0)),
                  _resident((D_MODEL, ODD_IN)), tbl, tbl, _resident((1, RET_VW))],
        out_specs=pl.BlockSpec((TM, ODD_IN), lambda i: (i, 0)),
        out_shape=jax.ShapeDtypeStruct((t, ODD_IN), BF16),
        compiler_params=_params(1, 50),
        name="odd_proj",
    )(x, w, cos, sin, norm_g)


def _retention_kernel(cd_ref, q_ref, k_ref, v_ref, gate_ref, dmat_ref, qd_ref, kd_ref,
                      o_ref, *, seq):
    hp = RET_HEADS_PER_STEP
    first = pl.program_id(1) * hp
    cd = [cd_ref[first + j] for j in range(hp)]
    dmat = [dmat_ref[j] for j in range(hp)]
    qd = [jnp.concatenate([qd_ref[j]] * (RET_DV // LANES), axis=1) for j in range(hp)]
    kd = [jnp.concatenate([kd_ref[j]] * (RET_DK // LANES), axis=1) for j in range(hp)]
    state = [jnp.zeros((RET_DK, RET_DV), F32) for _ in range(hp)]
    for c in range(seq // RET_BLOCK):
        rows = pl.ds(c * RET_BLOCK, RET_BLOCK)
        for j in range(hp):
            dk = slice(j * RET_DK, (j + 1) * RET_DK)
            dv = slice(j * RET_DV, (j + 1) * RET_DV)
            qc = q_ref[rows, dk]
            kc = k_ref[rows, dk]
            vc = v_ref[rows, dv]
            sc = lax.dot_general(qc, kc, (((1,), (1,)), ((), ())),
                                 preferred_element_type=F32) * dmat[j]
            y = jnp.dot(sc.astype(BF16), vc, preferred_element_type=F32)
            y = y + jnp.dot(qc, state[j].astype(BF16), preferred_element_type=F32) * qd[j]
            kdec = (kc.astype(F32) * kd[j]).astype(BF16)
            state[j] = state[j] * cd[j] + lax.dot_general(
                kdec, vc, (((0,), (0,)), ((), ())), preferred_element_type=F32)
            mu = jnp.mean(y, -1, keepdims=True)
            d = y - mu
            var = jnp.mean(d * d, -1, keepdims=True)
            yn = d * lax.rsqrt(var + LN_EPS)
            o_ref[rows, dv] = (gate_ref[rows, dv].astype(F32) * yn).astype(BF16)


def _retention(hcat, tables, batch, seq):
    cd, dmat, qd, kd = tables
    hp = RET_HEADS_PER_STEP
    tbl = pl.BlockSpec((hp, RET_BLOCK, LANES), lambda b, h: (h, 0, 0))
    k0 = RET_QK // (hp * RET_DK)
    v0 = 2 * RET_QK // (hp * RET_DV)
    g0 = v0 + RET_VW // (hp * RET_DV)
    return pl.pallas_call(
        functools.partial(_retention_kernel, seq=seq),
        grid=(batch, RET_HEADS // hp),
        in_specs=[pl.BlockSpec(memory_space=pltpu.SMEM),
                  pl.BlockSpec((seq, hp * RET_DK), lambda b, h: (b, h)),
                  pl.BlockSpec((seq, hp * RET_DK), lambda b, h: (b, k0 + h)),
                  pl.BlockSpec((seq, hp * RET_DV), lambda b, h: (b, v0 + h)),
                  pl.BlockSpec((seq, hp * RET_DV), lambda b, h: (b, g0 + h)),
                  pl.BlockSpec((hp, RET_BLOCK, RET_BLOCK), lambda b, h: (h, 0, 0)),
                  tbl, tbl],
        out_specs=pl.BlockSpec((seq, hp * RET_DV), lambda b, h: (b, h)),
        out_shape=jax.ShapeDtypeStruct((batch * seq, RET_VW), BF16),
        compiler_params=_params(2, 38),
        name="retention",
    )(cd, hcat, hcat, hcat, hcat, dmat, qd, kd)


def _odd_out_kernel(x_ref, y_ref, wout_ref, g_ref, b_ref, o_ref):
    rows = TM // OUT_SPLIT
    ys = [jnp.dot(y_ref[r * rows:(r + 1) * rows, :], wout_ref[...],
                  preferred_element_type=F32) for r in range(OUT_SPLIT)]
    for r in range(OUT_SPLIT):
        sl = slice(r * rows, (r + 1) * rows)
        o_ref[sl, :] = _layer_norm(ALPHA * x_ref[sl, :] + ys[r], g_ref[...], b_ref[...])


def _odd_out(x, y, w_out, g, b):
    t = x.shape[0]
    return pl.pallas_call(
        _odd_out_kernel,
        grid=(t // TM,),
        in_specs=[pl.BlockSpec((TM, D_MODEL), lambda i: (i, 0)),
                  pl.BlockSpec((TM, RET_VW), lambda i: (i, 0)),
                  _resident((RET_VW, D_MODEL)),
                  _resident((1, D_MODEL)), _resident((1, D_MODEL))],
        out_specs=pl.BlockSpec((TM, D_MODEL), lambda i: (i, 0)),
        out_shape=jax.ShapeDtypeStruct((t, D_MODEL), F32),
        compiler_params=_params(1, 32),
        name="odd_out",
    )(x, y, w_out, g, b)


def _rope_angles(seq, n_rot, theta):
    inv = np.exp(-math.log(theta) * np.arange(n_rot // 2, dtype=np.float64) * (2.0 / n_rot))
    return np.arange(seq, dtype=np.float64)[:, None] * inv[None, :]


def _even_rope_tables(seq):
    half = ROPE_DIMS // 2
    ang = _rope_angles(seq, ROPE_DIMS, ROPE_THETA)
    cos, sin = np.cos(ang), np.sin(ang)
    pad = np.zeros((seq, DIFF_DH - ROPE_DIMS))
    zeros = np.zeros((seq, half))
    c = np.concatenate([cos, cos, pad + 1.0], -1)
    sa = np.concatenate([-sin, zeros, pad], -1)
    sb = np.concatenate([zeros, sin, pad], -1)
    reps = LANES // DIFF_DH
    return tuple(jnp.asarray(np.tile(t, (1, reps)), F32) for t in (c, sa, sb))


def _odd_rope_tables(seq):
    ang = _rope_angles(seq, RET_DK, RET_ROPE_THETA)
    return jnp.asarray(np.cos(ang), F32), jnp.asarray(np.sin(ang), F32)


def _retention_tables():
    log_g = np.log1p(-np.exp2(-5.0 - np.arange(RET_HEADS, dtype=np.float64)))
    idx = np.arange(RET_BLOCK, dtype=np.float64)
    rel = idx[:, None] - idx[None, :]
    dmat = np.where(rel >= 0, np.exp(log_g[:, None, None] * np.maximum(rel, 0.0)), 0.0)
    q_decay = np.exp(log_g[:, None] * (idx + 1.0))
    k_decay = np.exp(log_g[:, None] * (RET_BLOCK - 1.0 - idx))
    cd = np.exp(log_g * RET_BLOCK)
    rep = lambda t: np.broadcast_to(t[:, :, None], (RET_HEADS, RET_BLOCK, LANES))
    return tuple(jnp.asarray(t, F32) for t in (cd, dmat, rep(q_decay), rep(k_decay)))


def kernel(x, ln_g, ln_b, ffn_w_in, ffn_w_out, even_w_in, even_w_out, diff_lambda,
           diff_norm_g, conv_w, odd_w_in, odd_w_out, ret_norm_g):
    batch, seq, _ = x.shape
    assert seq % TM == 0 and seq % ATT_TQ == 0 and DEPTH == 2
    h = x.reshape(batch * seq, D_MODEL)
    row = lambda v: v.reshape(1, -1)
    norm = lambda i, k: (row(ln_g[i, k]), row(ln_b[i, k]))
    ffn = lambda i, k: [(ffn_w_in, (i, k)), (ffn_w_out, (i, k))]


    h, (e_in, e_out) = _ffn_ln(h, ffn_w_in, ffn_w_out, *norm(0, 0), stream=(0, 0),
                               casts=[(even_w_in, (0,)), (even_w_out, (0,))])
    lambda_init = 0.8 - 0.6 * math.exp(-0.3 * 0)
    hcat, (w_in, w_out, *next_ffn) = _even_proj(
        h, e_in, *_even_rope_tables(seq), conv_w[0], seq, casts=ffn(0, 1) + ffn(1, 0))
    attn, _ = _diff_attention(hcat, diff_lambda[0], row(diff_norm_g[0]), batch, seq,
                              lambda_init)
    h = _even_out(h, attn, hcat, e_out, *norm(0, 1))
    h, (o_in, o_out) = _ffn_ln(h, w_in, w_out, *norm(0, 2),
                               casts=[(odd_w_in, (0,)), (odd_w_out, (0,))])

    h, last_ffn = _ffn_ln(h, *next_ffn, *norm(1, 0), casts=ffn(1, 1))
    hcat = _odd_proj(h, o_in, *_odd_rope_tables(seq), row(ret_norm_g[0]), seq)
    y = _retention(hcat, _retention_tables(), batch, seq)
    h = _odd_out(h, y, o_out, *norm(1, 1))
    h, _ = _ffn_ln(h, *last_ffn, *norm(1, 2))
    return h.reshape(batch, seq, D_MODEL)
```

```python
import functools
import math

import jax
import jax.numpy as jnp
import numpy as np
from jax import lax
from jax.experimental import pallas as pl
from jax.experimental.pallas import tpu as pltpu

F32 = jnp.float32
BF16 = jnp.bfloat16

D_MODEL = 1024
DEPTH = 2
D_FF = 2816
DIFF_HEADS = 4
DIFF_DH = 64
DIFF_DV = 128
DIFF_QK = 512
DIFF_WIDTH = 512
ROPE_THETA = 500000.0
ROPE_DIMS = 16
CONV_CH = 512
EVEN_IN = 3072
RET_HEADS = 4
RET_DK = 256
RET_DV = 512
RET_QK = 1024
RET_VW = 2048
RET_ROPE_THETA = 10000.0
ODD_IN = 6144
ALPHA = (2.0 * DEPTH) ** 0.25
LN_EPS = 1e-5
NEG_INF = -1e30
LOG2E = math.log2(math.e)

LANES = 128
MIB = 1024 * 1024
VMEM_LIMIT = 56 * MIB

TM = 1024
FF_CHUNK = 256
OUT_SPLIT = 4
FFN_ROW_GROUPS = (256, 256, 256, 256)
EVEN_OUT_TM = 2048
ODD_PROJ_CHUNK = 256
RET_BLOCK = 256
ATT_TQ = 256
ATT_STRIP = 16
ATT_HEADS_PER_STEP = 1
RET_HEADS_PER_STEP = 2


def _layer_norm(z, g, b):
    mu = jnp.mean(z, -1, keepdims=True)
    d = z - mu
    var = jnp.mean(d * d, -1, keepdims=True)
    return d * lax.rsqrt(var + LN_EPS) * g + b


def _params(n_axes, vmem_mib):
    assert vmem_mib * MIB <= VMEM_LIMIT
    return pltpu.CompilerParams(dimension_semantics=("arbitrary",) * n_axes,
                                vmem_limit_bytes=vmem_mib * MIB)


def _resident(shape):
    nd = len(shape)
    return pl.BlockSpec(shape, lambda *_: (0,) * nd, pipeline_mode=pl.Buffered(1))


def _cast_specs(casts, grid):
    steps = math.prod(grid)

    def step(*idx):
        lin = idx[0]
        for size, i in zip(grid[1:], idx[1:]):
            lin = lin * size + i
        return lin

    in_specs, out_specs, out_shapes = [], [], []
    for arr, prefix in casts:
        r, c = arr.shape[-2:]
        assert arr.ndim == len(prefix) + 2
        rows = min(d for d in range(16, r + 1, 16) if r % d == 0 and d * steps >= r)
        last = r // rows - 1
        in_specs.append(pl.BlockSpec(
            (None,) * len(prefix) + (rows, c),
            lambda *idx, p=prefix, n=last: p + (jnp.minimum(step(*idx), n), 0)))
        out_specs.append(pl.BlockSpec(
            (rows, c), lambda *idx, n=last: (jnp.minimum(step(*idx), n), 0)))
        out_shapes.append(jax.ShapeDtypeStruct((r, c), BF16))
    return in_specs, out_specs, out_shapes


def _cast_vmem_mib(casts, grid):
    in_specs, _, _ = _cast_specs(casts, grid)
    elems = sum(math.prod(d for d in spec.block_shape if d is not None) for spec in in_specs)
    return -(-elems * (4 + 2) * 2 // MIB)


def _do_casts(src_refs, dst_refs):
    for src, dst in zip(src_refs, dst_refs):
        dst[...] = src[...].astype(BF16)


W_STREAM_STEPS = 16
W_STREAM_SLOTS = 3


def _stream_cast(src_hbm, pick, dst_ref, stage_ref, sem_ref, sem_row):
    slots, rows = stage_ref.shape[:2]
    n = dst_ref.shape[0] // rows
    copy = lambda c: pltpu.make_async_copy(
        src_hbm.at[pick[0], pick[1], pl.ds(c * rows, rows), :],
        stage_ref.at[c % slots], sem_ref.at[sem_row, c % slots])
    for c in range(min(slots - 1, n)):
        copy(c).start()
    for c in range(n):
        if c + slots - 1 < n:
            copy(c + slots - 1).start()
        copy(c).wait()
        dst_ref[c * rows:(c + 1) * rows, :] = stage_ref[c % slots].astype(BF16)


def _ffn_kernel(x_ref, win_ref, wout_ref, g_ref, b_ref, *rest, n_cast, stream):
    cast_src, o_ref, cast_dst = rest[:n_cast], rest[n_cast], rest[n_cast + 1:2 * n_cast + 1]
    act_ref = rest[2 * n_cast + 1]
    if stream is not None:
        win_bf, wout_bf, stage_in, stage_out, sems = rest[2 * n_cast + 2:]

        @pl.when(pl.program_id(0) == 0)
        def _():
            _stream_cast(win_ref, stream, win_bf, stage_in, sems, 0)
            _stream_cast(wout_ref, stream, wout_bf, stage_out, sems, 1)

        win_ref, wout_ref = win_bf, wout_bf
    _do_casts(cast_src, cast_dst)
    x = x_ref[...]
    xb = x.astype(BF16)
    for c in range(D_FF // FF_CHUNK):
        lo = c * FF_CHUNK
        gate = jnp.dot(xb, win_ref[:, lo:lo + FF_CHUNK], preferred_element_type=F32)
        up = jnp.dot(xb, win_ref[:, D_FF + lo:D_FF + lo + FF_CHUNK], preferred_element_type=F32)
        act_ref[:, lo:lo + FF_CHUNK] = (gate * jax.nn.sigmoid(gate) * up).astype(BF16)
    bounds = [sum(FFN_ROW_GROUPS[:r]) for r in range(len(FFN_ROW_GROUPS) + 1)]
    groups = [slice(lo, hi) for lo, hi in zip(bounds[:-1], bounds[1:])]
    ys = [jnp.dot(act_ref[sl, :], wout_ref[...], preferred_element_type=F32) for sl in groups]
    for sl, y in zip(groups, ys):
        o_ref[sl, :] = _layer_norm(ALPHA * x_ref[sl, :] + 0.5 * y, g_ref[...], b_ref[...])


def _ffn_ln(x, w_in, w_out, g, b, casts=(), stream=None):
    t = x.shape[0]
    steps = t // TM
    cast_in, cast_out, cast_shapes = _cast_specs(casts, (steps,))
    scratch = [pltpu.VMEM((TM, D_FF), BF16)]
    if stream is None:
        weights = [_resident((D_MODEL, 2 * D_FF)), _resident((D_FF, D_MODEL))]
        stage_mib = 0
    else:
        weights = [pl.BlockSpec(memory_space=pl.ANY)] * 2
        rows_in, rows_out = D_MODEL // W_STREAM_STEPS, D_FF // W_STREAM_STEPS
        scratch += [pltpu.VMEM((D_MODEL, 2 * D_FF), BF16), pltpu.VMEM((D_FF, D_MODEL), BF16),
                    pltpu.VMEM((W_STREAM_SLOTS, rows_in, 2 * D_FF), F32),
                    pltpu.VMEM((W_STREAM_SLOTS, rows_out, D_MODEL), F32),
                    pltpu.SemaphoreType.DMA((2, W_STREAM_SLOTS))]
        stage_mib = -(-W_STREAM_SLOTS * 4 * (rows_in * 2 * D_FF + rows_out * D_MODEL) // MIB)
    out = pl.pallas_call(
        functools.partial(_ffn_kernel, n_cast=len(casts), stream=stream),
        grid=(steps,),
        in_specs=[pl.BlockSpec((TM, D_MODEL), lambda i: (i, 0))] + weights
                 + [_resident((1, D_MODEL)), _resident((1, D_MODEL))] + cast_in,
        out_specs=[pl.BlockSpec((TM, D_MODEL), lambda i: (i, 0))] + cast_out,
        out_shape=[jax.ShapeDtypeStruct((t, D_MODEL), F32)] + cast_shapes,
        scratch_shapes=scratch,
        compiler_params=_params(1, 45 + stage_mib + _cast_vmem_mib(casts, (steps,))),
        name="ffn_ln",
    )(x, w_in, w_out, g, b, *[arr for arr, _ in casts])
    return out[0], out[1:]


EVEN_OUT_COLS = 2 * DIFF_QK + DIFF_WIDTH + CONV_CH
CONV_TAIL = 8


def _even_proj_kernel(x_ref, w_ref, c_ref, sa_ref, sb_ref, cw_ref, *rest, n_cast, per_seq):
    cast_src, o_ref, cast_dst, tail_ref = (rest[:n_cast], rest[n_cast],
                                           rest[n_cast + 1:2 * n_cast + 1], rest[-1])
    _do_casts(cast_src, cast_dst)
    i = pl.program_id(0)

    @pl.when(i == 0)
    def _():
        tail_ref[...] = jnp.zeros_like(tail_ref)

    xb = x_ref[...].astype(BF16)
    c = c_ref[...]
    sa = sa_ref[...]
    sb = sb_ref[...]
    proj = lambda n: jnp.dot(xb, w_ref[:, n * 512:(n + 1) * 512], preferred_element_type=F32)
    for n in range(2):
        r = proj(n)
        scale = DIFF_DH ** -0.5 * LOG2E if n == 0 else 1.0
        for j in range(512 // LANES):
            blk = r[:, j * LANES:(j + 1) * LANES]
            rot = (blk * c + pltpu.roll(blk, LANES - 8, 1) * sa
                   + pltpu.roll(blk, 8, 1) * sb)
            if n == 0:
                rot = rot * scale
            o_ref[:, n * 512 + j * LANES:n * 512 + (j + 1) * LANES] = rot.astype(BF16)
    u = proj(4) * proj(5)
    prev = jnp.where(i % per_seq == 0, jnp.zeros_like(u[:CONV_TAIL]), tail_ref[...])
    tail_ref[...] = u[TM - CONV_TAIL:]
    row = lax.broadcasted_iota(jnp.int32, (CONV_TAIL, CONV_CH), 0)
    u1 = pltpu.roll(u, 1, 0)
    u2 = pltpu.roll(u, 2, 0)
    head1 = jnp.where(row == 0, prev[CONV_TAIL - 1:], u1[:CONV_TAIL])
    head2 = jnp.where(row == 0, prev[CONV_TAIL - 2:CONV_TAIL - 1], u2[:CONV_TAIL])
    head2 = jnp.where(row == 1, prev[CONV_TAIL - 1:], head2)
    u1 = jnp.concatenate([head1, u1[CONV_TAIL:]], 0)
    u2 = jnp.concatenate([head2, u2[CONV_TAIL:]], 0)
    cw = cw_ref[...]
    conv = proj(3) * (cw[0:1] * u2 + cw[1:2] * u1 + cw[2:3] * u)
    o_ref[:, 3 * 512:] = conv.astype(BF16)
    o_ref[:, 2 * 512:3 * 512] = proj(2).astype(BF16)


def _even_proj(x, w, c, sa, sb, conv_w, seq, casts=()):
    t = x.shape[0]
    steps = t // TM
    per_seq = seq // TM
    tbl = pl.BlockSpec((TM, LANES), lambda i: (i % per_seq, 0))
    cast_in, cast_out, cast_shapes = _cast_specs(casts, (steps,))
    out = pl.pallas_call(
        functools.partial(_even_proj_kernel, n_cast=len(casts), per_seq=per_seq),
        grid=(steps,),
        in_specs=[pl.BlockSpec((TM, D_MODEL), lambda i: (i, 0)),
                  _resident((D_MODEL, EVEN_IN)), tbl, tbl, tbl,
                  _resident((3, CONV_CH))] + cast_in,
        out_specs=[pl.BlockSpec((TM, EVEN_OUT_COLS), lambda i: (i, 0))] + cast_out,
        out_shape=[jax.ShapeDtypeStruct((t, EVEN_OUT_COLS), BF16)] + cast_shapes,
        scratch_shapes=[pltpu.VMEM((CONV_TAIL, CONV_CH), F32)],
        compiler_params=_params(1, 28 + _cast_vmem_mib(casts, (steps,))),
        name="even_proj",
    )(x, w, c, sa, sb, conv_w, *[arr for arr, _ in casts])
    return out[0], out[1:]


def _attn_kernel(q_ref, k_ref, v_ref, lv_ref, g_ref, *rest, lambda_init, seq, n_cast):
    cast_src, o_ref, cast_dst, s_ref, p_ref = (rest[:n_cast], rest[n_cast],
                                               rest[n_cast + 1:2 * n_cast + 1], rest[-2], rest[-1])
    _do_casts(cast_src, cast_dst)
    lv = lv_ref[...]
    lam = (jnp.exp(jnp.sum(lv[0:1] * lv[1:2], -1, keepdims=True))
           - jnp.exp(jnp.sum(lv[2:3] * lv[3:4], -1, keepdims=True)) + lambda_init)
    lane = lax.broadcasted_iota(jnp.int32, (ATT_TQ, LANES), 1)
    row = lax.broadcasted_iota(jnp.int32, (ATT_TQ, ATT_TQ), 0)
    col = lax.broadcasted_iota(jnp.int32, (ATT_TQ, ATT_TQ), 1)
    causal = col <= row
    gain = g_ref[...]
    nt = (((1,), (1,)), ((), ()))
    n_tiles = seq // ATT_TQ
    order = [t for pair in zip(range(n_tiles - 1, -1, -1), range(n_tiles)) for t in pair][:n_tiles]
    for pos, i in enumerate(order):
        for j in range(ATT_HEADS_PER_STEP):
            head = slice(j * LANES, (j + 1) * LANES)
            buf = pos % 2
            lo = i * ATT_TQ
            q = q_ref[lo:lo + ATT_TQ, head]
            zero = jnp.zeros_like(q)
            qs = (jnp.where(lane < DIFF_DH, q, zero), jnp.where(lane >= DIFF_DH, q, zero))
            hi = lo + ATT_TQ
            l = []
            for c, qc in enumerate(qs):
                s = lax.dot_general(qc, k_ref[:hi, head], nt, preferred_element_type=F32)
                sc_ref = s_ref.at[j, buf, c]
                if i > 0:
                    sc_ref[:, :lo] = s[:, :lo]
                sc_ref[:, lo:hi] = jnp.where(causal, s[:, lo:], NEG_INF)
                m = jnp.max(sc_ref[:, :hi], -1, keepdims=True)
                lsum = []
                for r in range(ATT_TQ // ATT_STRIP):
                    sl = slice(r * ATT_STRIP, (r + 1) * ATT_STRIP)
                    p = jnp.exp2(sc_ref[sl, :hi] - m[sl])
                    lsum.append(jnp.sum(p, -1, keepdims=True))
                    base = c * ATT_TQ + r * ATT_STRIP
                    p_ref[j, buf, base:base + ATT_STRIP, :hi] = p.astype(BF16)
                l.append(jnp.concatenate(lsum, 0))
            acc = jnp.dot(p_ref[j, buf, :, :hi], v_ref[:hi, head], preferred_element_type=F32)
            acc = acc[:ATT_TQ] * (1.0 / l[0]) - acc[ATT_TQ:] * (lam / l[1])
            y = acc * lax.rsqrt(jnp.mean(acc * acc, -1, keepdims=True) + LN_EPS)
            o_ref[lo:lo + ATT_TQ, head] = (y * gain * (1.0 - lambda_init)).astype(BF16)


def _diff_attention(hcat, lam_vecs, norm_g, batch, seq, lambda_init, casts=()):
    hp = ATT_HEADS_PER_STEP
    width = hp * LANES
    grid = (batch, DIFF_HEADS // hp)
    cast_in, cast_out, cast_shapes = _cast_specs(casts, grid)
    out = pl.pallas_call(
        functools.partial(_attn_kernel, lambda_init=lambda_init, seq=seq, n_cast=len(casts)),
        grid=grid,
        in_specs=[pl.BlockSpec((seq, width), lambda b, h: (b, h)),
                  pl.BlockSpec((seq, width), lambda b, h: (b, DIFF_QK // width + h)),
                  pl.BlockSpec((seq, width), lambda b, h: (b, 2 * DIFF_QK // width + h)),
                  _resident((4, DIFF_DH)),
                  _resident((1, DIFF_DV))] + cast_in,
        out_specs=[pl.BlockSpec((seq, width), lambda b, h: (b, h))] + cast_out,
        out_shape=[jax.ShapeDtypeStruct((batch * seq, DIFF_WIDTH), BF16)] + cast_shapes,
        scratch_shapes=[pltpu.VMEM((hp, 2, 2, ATT_TQ, seq), F32),
                        pltpu.VMEM((hp, 2, 2 * ATT_TQ, seq), BF16)],
        compiler_params=_params(2, 24 + _cast_vmem_mib(casts, grid)),
        name="diff_attn",
    )(hcat, hcat, hcat, lam_vecs, norm_g, *[arr for arr, _ in casts])
    return out[0], out[1:]


def _even_out_kernel(x_ref, attn_ref, conv_ref, wout_ref, g_ref, b_ref, o_ref):
    rows = TM // OUT_SPLIT
    groups = [slice(lo, lo + rows) for lo in range(0, EVEN_OUT_TM, rows)]
    ys = []
    for sl in groups:
        mix = jnp.concatenate([attn_ref[sl, :], conv_ref[sl, :]], 1)
        ys.append(jnp.dot(mix, wout_ref[...], preferred_element_type=F32))
    for sl, y in zip(groups, ys):
        o_ref[sl, :] = _layer_norm(ALPHA * x_ref[sl, :] + y, g_ref[...], b_ref[...])


def _even_out(x, attn, hcat, w_out, g, b):
    t = x.shape[0]
    tm = EVEN_OUT_TM
    conv_col = (2 * DIFF_QK + DIFF_WIDTH) // CONV_CH
    return pl.pallas_call(
        _even_out_kernel,
        grid=(t // tm,),
        in_specs=[pl.BlockSpec((tm, D_MODEL), lambda i: (i, 0)),
                  pl.BlockSpec((tm, DIFF_WIDTH), lambda i: (i, 0)),
                  pl.BlockSpec((tm, CONV_CH), lambda i: (i, conv_col)),
                  _resident((DIFF_WIDTH + CONV_CH, D_MODEL)),
                  _resident((1, D_MODEL)), _resident((1, D_MODEL))],
        out_specs=pl.BlockSpec((tm, D_MODEL), lambda i: (i, 0)),
        out_shape=jax.ShapeDtypeStruct((t, D_MODEL), F32),
        compiler_params=_params(1, 46),
        name="even_out",
    )(x, attn, hcat, w_out, g, b)


def _odd_proj_kernel(x_ref, w_ref, cos_ref, sin_ref, ng_ref, o_ref):
    xb = x_ref[...].astype(BF16)
    cos = cos_ref[...]
    sin = sin_ref[...]
    half = RET_DK // 2
    width = ODD_PROJ_CHUNK
    gate0 = (2 * RET_QK + RET_VW) // width
    for n in list(range(gate0, ODD_IN // width)) + list(range(gate0)):
        cols = slice(n * width, (n + 1) * width)
        r = jnp.dot(xb, w_ref[:, cols], preferred_element_type=F32)
        if n >= gate0:
            gain = ng_ref[:, (n - gate0) * width:(n - gate0 + 1) * width]
            o_ref[:, cols] = (r * jax.nn.sigmoid(r) * gain).astype(BF16)
        elif n < 2 * RET_QK // width:
            for hh in range(width // RET_DK):
                x1 = r[:, hh * RET_DK:hh * RET_DK + half]
                x2 = r[:, hh * RET_DK + half:(hh + 1) * RET_DK]
                o1 = x1 * cos - x2 * sin
                o2 = x2 * cos + x1 * sin
                if n >= RET_QK // width:
                    o1 = o1 * RET_DK ** -0.5
                    o2 = o2 * RET_DK ** -0.5
                base = n * width + hh * RET_DK
                o_ref[:, base:base + half] = o1.astype(BF16)
                o_ref[:, base + half:base + RET_DK] = o2.astype(BF16)
        else:
            o_ref[:, cols] = r.astype(BF16)


def _odd_proj(x, w, cos, sin, norm_g, seq):
    t = x.shape[0]
    per_seq = seq // TM
    tbl = pl.BlockSpec((TM, LANES), lambda i: (i % per_seq, 0))
    return pl.pallas_call(
        _odd_proj_kernel,
        grid=(t // TM,),
        in_specs=[pl.BlockSpec((TM, D_MODEL), lambda i: (i, 0)),
                  _resident((D_MODEL, ODD_IN)), tbl, tbl, _resident((1, RET_VW))],
        out_specs=pl.BlockSpec((TM, ODD_IN), lambda i: (i, 0)),
        out_shape=jax.ShapeDtypeStruct((t, ODD_IN), BF16),
        compiler_params=_params(1, 50),
        name="odd_proj",
    )(x, w, cos, sin, norm_g)


def _retention_kernel(cd_ref, q_ref, k_ref, v_ref, gate_ref, dmat_ref, qd_ref, kd_ref,
                      o_ref, *, seq):
    hp = RET_HEADS_PER_STEP
    first = pl.program_id(1) * hp
    cd = [cd_ref[first + j] for j in range(hp)]
    dmat = [dmat_ref[j] for j in range(hp)]
    qd = [jnp.concatenate([qd_ref[j]] * (RET_DV // LANES), axis=1) for j in range(hp)]
    kd = [jnp.concatenate([kd_ref[j]] * (RET_DK // LANES), axis=1) for j in range(hp)]
    state = [jnp.zeros((RET_DK, RET_DV), F32) for _ in range(hp)]
    for c in range(seq // RET_BLOCK):
        rows = pl.ds(c * RET_BLOCK, RET_BLOCK)
        for j in range(hp):
            dk = slice(j * RET_DK, (j + 1) * RET_DK)
            dv = slice(j * RET_DV, (j + 1) * RET_DV)
            qc = q_ref[rows, dk]
            kc = k_ref[rows, dk]
            vc = v_ref[rows, dv]
            sc = lax.dot_general(qc, kc, (((1,), (1,)), ((), ())),
                                 preferred_element_type=F32) * dmat[j]
            y = jnp.dot(sc.astype(BF16), vc, preferred_element_type=F32)
            y = y + jnp.dot(qc, state[j].astype(BF16), preferred_element_type=F32) * qd[j]
            kdec = (kc.astype(F32) * kd[j]).astype(BF16)
            state[j] = state[j] * cd[j] + lax.dot_general(
                kdec, vc, (((0,), (0,)), ((), ())), preferred_element_type=F32)
            mu = jnp.mean(y, -1, keepdims=True)
            d = y - mu
            var = jnp.mean(d * d, -1, keepdims=True)
            yn = d * lax.rsqrt(var + LN_EPS)
            o_ref[rows, dv] = (gate_ref[rows, dv].astype(F32) * yn).astype(BF16)


def _retention(hcat, tables, batch, seq):
    cd, dmat, qd, kd = tables
    hp = RET_HEADS_PER_STEP
    tbl = pl.BlockSpec((hp, RET_BLOCK, LANES), lambda b, h: (h, 0, 0))
    k0 = RET_QK // (hp * RET_DK)
    v0 = 2 * RET_QK // (hp * RET_DV)
    g0 = v0 + RET_VW // (hp * RET_DV)
    return pl.pallas_call(
        functools.partial(_retention_kernel, seq=seq),
        grid=(batch, RET_HEADS // hp),
        in_specs=[pl.BlockSpec(memory_space=pltpu.SMEM),
                  pl.BlockSpec((seq, hp * RET_DK), lambda b, h: (b, h)),
                  pl.BlockSpec((seq, hp * RET_DK), lambda b, h: (b, k0 + h)),
                  pl.BlockSpec((seq, hp * RET_DV), lambda b, h: (b, v0 + h)),
                  pl.BlockSpec((seq, hp * RET_DV), lambda b, h: (b, g0 + h)),
                  pl.BlockSpec((hp, RET_BLOCK, RET_BLOCK), lambda b, h: (h, 0, 0)),
                  tbl, tbl],
        out_specs=pl.BlockSpec((seq, hp * RET_DV), lambda b, h: (b, h)),
        out_shape=jax.ShapeDtypeStruct((batch * seq, RET_VW), BF16),
        compiler_params=_params(2, 38),
        name="retention",
    )(cd, hcat, hcat, hcat, hcat, dmat, qd, kd)


def _odd_out_kernel(x_ref, y_ref, wout_ref, g_ref, b_ref, o_ref):
    rows = TM // OUT_SPLIT
    ys = [jnp.dot(y_ref[r * rows:(r + 1) * rows, :], wout_ref[...],
                  preferred_element_type=F32) for r in range(OUT_SPLIT)]
    for r in range(OUT_SPLIT):
        sl = slice(r * rows, (r + 1) * rows)
        o_ref[sl, :] = _layer_norm(ALPHA * x_ref[sl, :] + ys[r], g_ref[...], b_ref[...])


def _odd_out(x, y, w_out, g, b):
    t = x.shape[0]
    return pl.pallas_call(
        _odd_out_kernel,
        grid=(t // TM,),
        in_specs=[pl.BlockSpec((TM, D_MODEL), lambda i: (i, 0)),
                  pl.BlockSpec((TM, RET_VW), lambda i: (i, 0)),
                  _resident((RET_VW, D_MODEL)),
                  _resident((1, D_MODEL)), _resident((1, D_MODEL))],
        out_specs=pl.BlockSpec((TM, D_MODEL), lambda i: (i, 0)),
        out_shape=jax.ShapeDtypeStruct((t, D_MODEL), F32),
        compiler_params=_params(1, 32),
        name="odd_out",
    )(x, y, w_out, g, b)


def _rope_angles(seq, n_rot, theta):
    inv = np.exp(-math.log(theta) * np.arange(n_rot // 2, dtype=np.float64) * (2.0 / n_rot))
    return np.arange(seq, dtype=np.float64)[:, None] * inv[None, :]


def _even_rope_tables(seq):
    half = ROPE_DIMS // 2
    ang = _rope_angles(seq, ROPE_DIMS, ROPE_THETA)
    cos, sin = np.cos(ang), np.sin(ang)
    pad = np.zeros((seq, DIFF_DH - ROPE_DIMS))
    zeros = np.zeros((seq, half))
    c = np.concatenate([cos, cos, pad + 1.0], -1)
    sa = np.concatenate([-sin, zeros, pad], -1)
    sb = np.concatenate([zeros, sin, pad], -1)
    reps = LANES // DIFF_DH
    return tuple(jnp.asarray(np.tile(t, (1, reps)), F32) for t in (c, sa, sb))


def _odd_rope_tables(seq):
    ang = _rope_angles(seq, RET_DK, RET_ROPE_THETA)
    return jnp.asarray(np.cos(ang), F32), jnp.asarray(np.sin(ang), F32)


def _retention_tables():
    log_g = np.log1p(-np.exp2(-5.0 - np.arange(RET_HEADS, dtype=np.float64)))
    idx = np.arange(RET_BLOCK, dtype=np.float64)
    rel = idx[:, None] - idx[None, :]
    dmat = np.where(rel >= 0, np.exp(log_g[:, None, None] * np.maximum(rel, 0.0)), 0.0)
    q_decay = np.exp(log_g[:, None] * (idx + 1.0))
    k_decay = np.exp(log_g[:, None] * (RET_BLOCK - 1.0 - idx))
    cd = np.exp(log_g * RET_BLOCK)
    rep = lambda t: np.broadcast_to(t[:, :, None], (RET_HEADS, RET_BLOCK, LANES))
    return tuple(jnp.asarray(t, F32) for t in (cd, dmat, rep(q_decay), rep(k_decay)))


def kernel(x, ln_g, ln_b, ffn_w_in, ffn_w_out, even_w_in, even_w_out, diff_lambda,
           diff_norm_g, conv_w, odd_w_in, odd_w_out, ret_norm_g):
    batch, seq, _ = x.shape
    assert seq % TM == 0 and seq % ATT_TQ == 0 and DEPTH == 2
    h = x.reshape(batch * seq, D_MODEL)
    row = lambda v: v.reshape(1, -1)
    norm = lambda i, k: (row(ln_g[i, k]), row(ln_b[i, k]))
    ffn = lambda i, k: [(ffn_w_in, (i, k)), (ffn_w_out, (i, k))]


    h, (e_in, e_out) = _ffn_ln(h, ffn_w_in, ffn_w_out, *norm(0, 0), stream=(0, 0),
                               casts=[(even_w_in, (0,)), (even_w_out, (0,))])
    lambda_init = 0.8 - 0.6 * math.exp(-0.3 * 0)
    hcat, (w_in, w_out, *next_ffn) = _even_proj(
        h, e_in, *_even_rope_tables(seq), conv_w[0], seq, casts=ffn(0, 1) + ffn(1, 0))
    attn, _ = _diff_attention(hcat, diff_lambda[0], row(diff_norm_g[0]), batch, seq,
                              lambda_init)
    h = _even_out(h, attn, hcat, e_out, *norm(0, 1))
    h, (o_in, o_out) = _ffn_ln(h, w_in, w_out, *norm(0, 2),
                               casts=[(odd_w_in, (0,)), (odd_w_out, (0,))])

    h, last_ffn = _ffn_ln(h, *next_ffn, *norm(1, 0), casts=ffn(1, 1))
    hcat = _odd_proj(h, o_in, *_odd_rope_tables(seq), row(ret_norm_g[0]), seq)
    y = _retention(hcat, _retention_tables(), batch, seq)
    h = _odd_out(h, y, o_out, *norm(1, 1))
    h, _ = _ffn_ln(h, *last_ffn, *norm(1, 2))
    return h.reshape(batch, seq, D_MODEL)
```

```python
import functools
import math

import jax
import jax.numpy as jnp
import numpy as np
from jax import lax
from jax.experimental import pallas as pl
from jax.experimental.pallas import tpu as pltpu

F32 = jnp.float32
BF16 = jnp.bfloat16

D_MODEL = 1024
DEPTH = 2
D_FF = 2816
DIFF_HEADS = 4
DIFF_DH = 64
DIFF_DV = 128
DIFF_QK = 512
DIFF_WIDTH = 512
ROPE_THETA = 500000.0
ROPE_DIMS = 16
CONV_CH = 512
EVEN_IN = 3072
RET_HEADS = 4
RET_DK = 256
RET_DV = 512
RET_QK = 1024
RET_VW = 2048
RET_ROPE_THETA = 10000.0
ODD_IN = 6144
ALPHA = (2.0 * DEPTH) ** 0.25
LN_EPS = 1e-5
NEG_INF = -1e30
LOG2E = math.log2(math.e)

LANES = 128
MIB = 1024 * 1024
VMEM_LIMIT = 56 * MIB

TM = 1024
FF_CHUNK = 256
OUT_SPLIT = 4
FFN_ROW_GROUPS = (256, 256, 256, 256)
EVEN_OUT_TM = 1024
ODD_TM = 1024
RET_BLOCK = 256
ATT_TQ = 256
ATT_STRIP = 16
ATT_HEADS_PER_STEP = 1


def _layer_norm(z, g, b):
    mu = jnp.mean(z, -1, keepdims=True)
    d = z - mu
    var = jnp.mean(d * d, -1, keepdims=True)
    return d * lax.rsqrt(var + LN_EPS) * g + b


def _params(n_axes, vmem_mib):
    assert vmem_mib * MIB <= VMEM_LIMIT
    return pltpu.CompilerParams(dimension_semantics=("arbitrary",) * n_axes,
                                vmem_limit_bytes=vmem_mib * MIB)


def _resident(shape):
    nd = len(shape)
    return pl.BlockSpec(shape, lambda *_: (0,) * nd, pipeline_mode=pl.Buffered(1))


def _cast_specs(casts, grid):
    steps = math.prod(grid)

    def step(*idx):
        lin = idx[0]
        for size, i in zip(grid[1:], idx[1:]):
            lin = lin * size + i
        return lin

    in_specs, out_specs, out_shapes = [], [], []
    for arr, prefix in casts:
        r, c = arr.shape[-2:]
        assert arr.ndim == len(prefix) + 2
        rows = min(d for d in range(16, r + 1, 16) if r % d == 0 and d * steps >= r)
        last = r // rows - 1
        in_specs.append(pl.BlockSpec(
            (None,) * len(prefix) + (rows, c),
            lambda *idx, p=prefix, n=last: p + (jnp.minimum(step(*idx), n), 0)))
        out_specs.append(pl.BlockSpec(
            (rows, c), lambda *idx, n=last: (jnp.minimum(step(*idx), n), 0)))
        out_shapes.append(jax.ShapeDtypeStruct((r, c), BF16))
    return in_specs, out_specs, out_shapes


def _cast_vmem_mib(casts, grid):
    in_specs, _, _ = _cast_specs(casts, grid)
    elems = sum(math.prod(d for d in spec.block_shape if d is not None) for spec in in_specs)
    return -(-elems * (4 + 2) * 2 // MIB)


def _do_casts(src_refs, dst_refs):
    for src, dst in zip(src_refs, dst_refs):
        dst[...] = src[...].astype(BF16)


W_STREAM_STEPS = 16
W_STREAM_SLOTS = 3


def _stream_cast(src_hbm, pick, dst_ref, stage_ref, sem_ref, sem_row):
    slots, rows = stage_ref.shape[:2]
    n = dst_ref.shape[0] // rows
    copy = lambda c: pltpu.make_async_copy(
        src_hbm.at[pick[0], pick[1], pl.ds(c * rows, rows), :],
        stage_ref.at[c % slots], sem_ref.at[sem_row, c % slots])
    for c in range(min(slots - 1, n)):
        copy(c).start()
    for c in range(n):
        if c + slots - 1 < n:
            copy(c + slots - 1).start()
        copy(c).wait()
        dst_ref[c * rows:(c + 1) * rows, :] = stage_ref[c % slots].astype(BF16)


def _ffn_kernel(x_ref, win_ref, wout_ref, g_ref, b_ref, *rest, n_cast, stream):
    cast_src, o_ref, cast_dst = rest[:n_cast], rest[n_cast], rest[n_cast + 1:2 * n_cast + 1]
    act_ref = rest[2 * n_cast + 1]
    if stream is not None:
        win_bf, wout_bf, stage_in, stage_out, sems = rest[2 * n_cast + 2:]

        @pl.when(pl.program_id(0) == 0)
        def _():
            _stream_cast(win_ref, stream, win_bf, stage_in, sems, 0)
            _stream_cast(wout_ref, stream, wout_bf, stage_out, sems, 1)

        win_ref, wout_ref = win_bf, wout_bf
    _do_casts(cast_src, cast_dst)
    x = x_ref[...]
    xb = x.astype(BF16)
    for c in range(D_FF // FF_CHUNK):
        lo = c * FF_CHUNK
        gate = jnp.dot(xb, win_ref[:, lo:lo + FF_CHUNK], preferred_element_type=F32)
        up = jnp.dot(xb, win_ref[:, D_FF + lo:D_FF + lo + FF_CHUNK], preferred_element_type=F32)
        act_ref[:, lo:lo + FF_CHUNK] = (gate * jax.nn.sigmoid(gate) * up).astype(BF16)
    bounds = [sum(FFN_ROW_GROUPS[:r]) for r in range(len(FFN_ROW_GROUPS) + 1)]
    groups = [slice(lo, hi) for lo, hi in zip(bounds[:-1], bounds[1:])]
    ys = [jnp.dot(act_ref[sl, :], wout_ref[...], preferred_element_type=F32) for sl in groups]
    for sl, y in zip(groups, ys):
        o_ref[sl, :] = _layer_norm(ALPHA * x_ref[sl, :] + 0.5 * y, g_ref[...], b_ref[...])


def _ffn_ln(x, w_in, w_out, g, b, casts=(), stream=None):
    t = x.shape[0]
    steps = t // TM
    cast_in, cast_out, cast_shapes = _cast_specs(casts, (steps,))
    scratch = [pltpu.VMEM((TM, D_FF), BF16)]
    if stream is None:
        weights = [_resident((D_MODEL, 2 * D_FF)), _resident((D_FF, D_MODEL))]
        stage_mib = 0
    else:
        weights = [pl.BlockSpec(memory_space=pl.ANY)] * 2
        rows_in, rows_out = D_MODEL // W_STREAM_STEPS, D_FF // W_STREAM_STEPS
        scratch += [pltpu.VMEM((D_MODEL, 2 * D_FF), BF16), pltpu.VMEM((D_FF, D_MODEL), BF16),
                    pltpu.VMEM((W_STREAM_SLOTS, rows_in, 2 * D_FF), F32),
                    pltpu.VMEM((W_STREAM_SLOTS, rows_out, D_MODEL), F32),
                    pltpu.SemaphoreType.DMA((2, W_STREAM_SLOTS))]
        stage_mib = -(-W_STREAM_SLOTS * 4 * (rows_in * 2 * D_FF + rows_out * D_MODEL) // MIB)
    out = pl.pallas_call(
        functools.partial(_ffn_kernel, n_cast=len(casts), stream=stream),
        grid=(steps,),
        in_specs=[pl.BlockSpec((TM, D_MODEL), lambda i: (i, 0))] + weights
                 + [_resident((1, D_MODEL)), _resident((1, D_MODEL))] + cast_in,
        out_specs=[pl.BlockSpec((TM, D_MODEL), lambda i: (i, 0))] + cast_out,
        out_shape=[jax.ShapeDtypeStruct((t, D_MODEL), F32)] + cast_shapes,
        scratch_shapes=scratch,
        compiler_params=_params(1, 45 + stage_mib + _cast_vmem_mib(casts, (steps,))),
        name="ffn_ln",
    )(x, w_in, w_out, g, b, *[arr for arr, _ in casts])
    return out[0], out[1:]


EVEN_OUT_COLS = 2 * DIFF_QK + DIFF_WIDTH + CONV_CH
CONV_TAIL = 8


def _even_proj_kernel(x_ref, w_ref, c_ref, sa_ref, sb_ref, cw_ref, *rest, n_cast, per_seq):
    cast_src, o_ref, cast_dst, tail_ref = (rest[:n_cast], rest[n_cast],
                                           rest[n_cast + 1:2 * n_cast + 1], rest[-1])
    _do_casts(cast_src, cast_dst)
    i = pl.program_id(0)

    @pl.when(i == 0)
    def _():
        tail_ref[...] = jnp.zeros_like(tail_ref)

    xb = x_ref[...].astype(BF16)
    c = c_ref[...]
    sa = sa_ref[...]
    sb = sb_ref[...]
    proj = lambda n: jnp.dot(xb, w_ref[:, n * 512:(n + 1) * 512], preferred_element_type=F32)
    for n in range(2):
        r = proj(n)
        scale = DIFF_DH ** -0.5 * LOG2E if n == 0 else 1.0
        for j in range(512 // LANES):
            blk = r[:, j * LANES:(j + 1) * LANES]
            rot = (blk * c + pltpu.roll(blk, LANES - 8, 1) * sa
                   + pltpu.roll(blk, 8, 1) * sb)
            if n == 0:
                rot = rot * scale
            o_ref[:, n * 512 + j * LANES:n * 512 + (j + 1) * LANES] = rot.astype(BF16)
    u = proj(4) * proj(5)
    prev = jnp.where(i % per_seq == 0, jnp.zeros_like(u[:CONV_TAIL]), tail_ref[...])
    tail_ref[...] = u[TM - CONV_TAIL:]
    row = lax.broadcasted_iota(jnp.int32, (CONV_TAIL, CONV_CH), 0)
    u1 = pltpu.roll(u, 1, 0)
    u2 = pltpu.roll(u, 2, 0)
    head1 = jnp.where(row == 0, prev[CONV_TAIL - 1:], u1[:CONV_TAIL])
    head2 = jnp.where(row == 0, prev[CONV_TAIL - 2:CONV_TAIL - 1], u2[:CONV_TAIL])
    head2 = jnp.where(row == 1, prev[CONV_TAIL - 1:], head2)
    u1 = jnp.concatenate([head1, u1[CONV_TAIL:]], 0)
    u2 = jnp.concatenate([head2, u2[CONV_TAIL:]], 0)
    cw = cw_ref[...]
    conv = proj(3) * (cw[0:1] * u2 + cw[1:2] * u1 + cw[2:3] * u)
    o_ref[:, 3 * 512:] = conv.astype(BF16)
    o_ref[:, 2 * 512:3 * 512] = proj(2).astype(BF16)


def _even_proj(x, w, c, sa, sb, conv_w, seq, casts=()):
    t = x.shape[0]
    steps = t // TM
    per_seq = seq // TM
    tbl = pl.BlockSpec((TM, LANES), lambda i: (i % per_seq, 0))
    cast_in, cast_out, cast_shapes = _cast_specs(casts, (steps,))
    out = pl.pallas_call(
        functools.partial(_even_proj_kernel, n_cast=len(casts), per_seq=per_seq),
        grid=(steps,),
        in_specs=[pl.BlockSpec((TM, D_MODEL), lambda i: (i, 0)),
                  _resident((D_MODEL, EVEN_IN)), tbl, tbl, tbl,
                  _resident((3, CONV_CH))] + cast_in,
        out_specs=[pl.BlockSpec((TM, EVEN_OUT_COLS), lambda i: (i, 0))] + cast_out,
        out_shape=[jax.ShapeDtypeStruct((t, EVEN_OUT_COLS), BF16)] + cast_shapes,
        scratch_shapes=[pltpu.VMEM((CONV_TAIL, CONV_CH), F32)],
        compiler_params=_params(1, 28 + _cast_vmem_mib(casts, (steps,))),
        name="even_proj",
    )(x, w, c, sa, sb, conv_w, *[arr for arr, _ in casts])
    return out[0], out[1:]


def _attn_kernel(q_ref, k_ref, v_ref, lv_ref, g_ref, *rest, lambda_init, seq, n_cast):
    cast_src, o_ref, cast_dst, s_ref, p_ref = (rest[:n_cast], rest[n_cast],
                                               rest[n_cast + 1:2 * n_cast + 1], rest[-2], rest[-1])
    _do_casts(cast_src, cast_dst)
    lv = lv_ref[...]
    lam = (jnp.exp(jnp.sum(lv[0:1] * lv[1:2], -1, keepdims=True))
           - jnp.exp(jnp.sum(lv[2:3] * lv[3:4], -1, keepdims=True)) + lambda_init)
    lane = lax.broadcasted_iota(jnp.int32, (ATT_TQ, LANES), 1)
    row = lax.broadcasted_iota(jnp.int32, (ATT_TQ, ATT_TQ), 0)
    col = lax.broadcasted_iota(jnp.int32, (ATT_TQ, ATT_TQ), 1)
    causal = col <= row
    gain = g_ref[...]
    nt = (((1,), (1,)), ((), ()))
    n_tiles = seq // ATT_TQ
    order = [t for pair in zip(range(n_tiles - 1, -1, -1), range(n_tiles)) for t in pair][:n_tiles]
    for pos, i in enumerate(order):
        for j in range(ATT_HEADS_PER_STEP):
            head = slice(j * LANES, (j + 1) * LANES)
            buf = pos % 2
            lo = i * ATT_TQ
            q = q_ref[lo:lo + ATT_TQ, head]
            zero = jnp.zeros_like(q)
            qs = (jnp.where(lane < DIFF_DH, q, zero), jnp.where(lane >= DIFF_DH, q, zero))
            hi = lo + ATT_TQ
            l = []
            for c, qc in enumerate(qs):
                s = lax.dot_general(qc, k_ref[:hi, head], nt, preferred_element_type=F32)
                sc_ref = s_ref.at[j, buf, c]
                if i > 0:
                    sc_ref[:, :lo] = s[:, :lo]
                sc_ref[:, lo:hi] = jnp.where(causal, s[:, lo:], NEG_INF)
                m = jnp.max(sc_ref[:, :hi], -1, keepdims=True)
                lsum = []
                for r in range(ATT_TQ // ATT_STRIP):
                    sl = slice(r * ATT_STRIP, (r + 1) * ATT_STRIP)
                    p = jnp.exp2(sc_ref[sl, :hi] - m[sl])
                    lsum.append(jnp.sum(p, -1, keepdims=True))
                    base = c * ATT_TQ + r * ATT_STRIP
                    p_ref[j, buf, base:base + ATT_STRIP, :hi] = p.astype(BF16)
                l.append(jnp.concatenate(lsum, 0))
            acc = jnp.dot(p_ref[j, buf, :, :hi], v_ref[:hi, head], preferred_element_type=F32)
            acc = acc[:ATT_TQ] * (1.0 / l[0]) - acc[ATT_TQ:] * (lam / l[1])
            y = acc * lax.rsqrt(jnp.mean(acc * acc, -1, keepdims=True) + LN_EPS)
            o_ref[lo:lo + ATT_TQ, head] = (y * gain * (1.0 - lambda_init)).astype(BF16)


def _diff_attention(hcat, lam_vecs, norm_g, batch, seq, lambda_init, casts=()):
    hp = ATT_HEADS_PER_STEP
    width = hp * LANES
    grid = (batch, DIFF_HEADS // hp)
    cast_in, cast_out, cast_shapes = _cast_specs(casts, grid)
    out = pl.pallas_call(
        functools.partial(_attn_kernel, lambda_init=lambda_init, seq=seq, n_cast=len(casts)),
        grid=grid,
        in_specs=[pl.BlockSpec((seq, width), lambda b, h: (b, h)),
                  pl.BlockSpec((seq, width), lambda b, h: (b, DIFF_QK // width + h)),
                  pl.BlockSpec((seq, width), lambda b, h: (b, 2 * DIFF_QK // width + h)),
                  _resident((4, DIFF_DH)),
                  _resident((1, DIFF_DV))] + cast_in,
        out_specs=[pl.BlockSpec((seq, width), lambda b, h: (b, h))] + cast_out,
        out_shape=[jax.ShapeDtypeStruct((batch * seq, DIFF_WIDTH), BF16)] + cast_shapes,
        scratch_shapes=[pltpu.VMEM((hp, 2, 2, ATT_TQ, seq), F32),
                        pltpu.VMEM((hp, 2, 2 * ATT_TQ, seq), BF16)],
        compiler_params=_params(2, 24 + _cast_vmem_mib(casts, grid)),
        name="diff_attn",
    )(hcat, hcat, hcat, lam_vecs, norm_g, *[arr for arr, _ in casts])
    return out[0], out[1:]


def _even_out_kernel(x_ref, attn_ref, conv_ref, wout_ref, g_ref, b_ref, o_ref):
    rows = TM // OUT_SPLIT
    groups = [slice(lo, lo + rows) for lo in range(0, EVEN_OUT_TM, rows)]
    ys = []
    for sl in groups:
        mix = jnp.concatenate([attn_ref[sl, :], conv_ref[sl, :]], 1)
        ys.append(jnp.dot(mix, wout_ref[...], preferred_element_type=F32))
    for sl, y in zip(groups, ys):
        o_ref[sl, :] = _layer_norm(ALPHA * x_ref[sl, :] + y, g_ref[...], b_ref[...])


def _even_out(x, attn, hcat, w_out, g, b):
    t = x.shape[0]
    tm = EVEN_OUT_TM
    conv_col = (2 * DIFF_QK + DIFF_WIDTH) // CONV_CH
    return pl.pallas_call(
        _even_out_kernel,
        grid=(t // tm,),
        in_specs=[pl.BlockSpec((tm, D_MODEL), lambda i: (i, 0)),
                  pl.BlockSpec((tm, DIFF_WIDTH), lambda i: (i, 0)),
                  pl.BlockSpec((tm, CONV_CH), lambda i: (i, conv_col)),
                  _resident((DIFF_WIDTH + CONV_CH, D_MODEL)),
                  _resident((1, D_MODEL)), _resident((1, D_MODEL))],
        out_specs=pl.BlockSpec((tm, D_MODEL), lambda i: (i, 0)),
        out_shape=jax.ShapeDtypeStruct((t, D_MODEL), F32),
        compiler_params=_params(1, 24),
        name="even_out",
    )(x, attn, hcat, w_out, g, b)


def _odd_mix_kernel(cd_ref, x_ref, w_ref, cos_ref, sin_ref, ng_ref, dmat_ref, qd_ref, kd_ref,
                    o_ref, state_ref):
    @pl.when(pl.program_id(1) == 0)
    def _():
        state_ref[...] = jnp.zeros_like(state_ref)

    xb = x_ref[...].astype(BF16)
    cos = cos_ref[...]
    sin = sin_ref[...]
    half = RET_DK // 2
    k0, v0, g0 = RET_QK, 2 * RET_QK, 2 * RET_QK + RET_VW

    def proj(lo, n):
        return jnp.dot(xb, w_ref[:, lo:lo + n], preferred_element_type=F32)

    def rotary(r):
        x1, x2 = r[:, :half], r[:, half:]
        return jnp.concatenate([x1 * cos - x2 * sin, x2 * cos + x1 * sin], 1)

    def project(j):
        q = rotary(proj(j * RET_DK, RET_DK)).astype(BF16)
        k = rotary(proj(k0 + j * RET_DK, RET_DK)) * RET_DK ** -0.5
        v = proj(v0 + j * RET_DV, RET_DV).astype(BF16)
        g = proj(g0 + j * RET_DV, RET_DV)
        gate = g * jax.nn.sigmoid(g) * ng_ref[:, j * RET_DV:(j + 1) * RET_DV]
        return q, k, k.astype(BF16), v, gate

    ahead = project(0)
    for j in range(RET_HEADS):
        q, k, kb, v, gate = ahead
        if j + 1 < RET_HEADS:
            ahead = project(j + 1)
        cd = cd_ref[j]
        dmat = dmat_ref[j]
        qd = jnp.concatenate([qd_ref[j]] * (RET_DV // LANES), axis=1)
        kd = jnp.concatenate([kd_ref[j]] * (RET_DK // LANES), axis=1)
        state = state_ref[j]
        for c in range(ODD_TM // RET_BLOCK):
            rows = slice(c * RET_BLOCK, (c + 1) * RET_BLOCK)
            qc, kc, vc = q[rows], kb[rows], v[rows]
            sc = lax.dot_general(qc, kc, (((1,), (1,)), ((), ())),
                                 preferred_element_type=F32) * dmat
            y = jnp.dot(sc.astype(BF16), vc, preferred_element_type=F32)
            y = y + jnp.dot(qc, state.astype(BF16), preferred_element_type=F32) * qd
            kdec = (k[rows] * kd).astype(BF16)
            state = state * cd + lax.dot_general(kdec, vc, (((0,), (0,)), ((), ())),
                                                 preferred_element_type=F32)
            mu = jnp.mean(y, -1, keepdims=True)
            d = y - mu
            var = jnp.mean(d * d, -1, keepdims=True)
            yn = d * lax.rsqrt(var + LN_EPS)
            o_ref[rows, j * RET_DV:(j + 1) * RET_DV] = (gate[rows] * yn).astype(BF16)
        state_ref[j] = state


def _odd_mix(x, w, cos, sin, norm_g, tables, batch, seq):
    cd, dmat, qd, kd = tables
    per_seq = seq // ODD_TM
    tile = lambda b, p: (b * per_seq + p, 0)
    tbl = pl.BlockSpec((ODD_TM, LANES), lambda b, p: (p, 0))
    return pl.pallas_call(
        _odd_mix_kernel,
        grid=(batch, per_seq),
        in_specs=[pl.BlockSpec(memory_space=pltpu.SMEM),
                  pl.BlockSpec((ODD_TM, D_MODEL), tile),
                  _resident((D_MODEL, ODD_IN)), tbl, tbl, _resident((1, RET_VW)),
                  _resident((RET_HEADS, RET_BLOCK, RET_BLOCK)),
                  _resident((RET_HEADS, RET_BLOCK, LANES)),
                  _resident((RET_HEADS, RET_BLOCK, LANES))],
        out_specs=pl.BlockSpec((ODD_TM, RET_VW), tile),
        out_shape=jax.ShapeDtypeStruct((batch * seq, RET_VW), BF16),
        scratch_shapes=[pltpu.VMEM((RET_HEADS, RET_DK, RET_DV), F32)],
        compiler_params=_params(2, 52),
        name="odd_mix",
    )(cd, x, w, cos, sin, norm_g, dmat, qd, kd)


def _odd_out_kernel(x_ref, y_ref, wout_ref, g_ref, b_ref, o_ref):
    rows = TM // OUT_SPLIT
    ys = [jnp.dot(y_ref[r * rows:(r + 1) * rows, :], wout_ref[...],
                  preferred_element_type=F32) for r in range(OUT_SPLIT)]
    for r in range(OUT_SPLIT):
        sl = slice(r * rows, (r + 1) * rows)
        o_ref[sl, :] = _layer_norm(ALPHA * x_ref[sl, :] + ys[r], g_ref[...], b_ref[...])


def _odd_out(x, y, w_out, g, b):
    t = x.shape[0]
    return pl.pallas_call(
        _odd_out_kernel,
        grid=(t // TM,),
        in_specs=[pl.BlockSpec((TM, D_MODEL), lambda i: (i, 0)),
                  pl.BlockSpec((TM, RET_VW), lambda i: (i, 0)),
                  _resident((RET_VW, D_MODEL)),
                  _resident((1, D_MODEL)), _resident((1, D_MODEL))],
        out_specs=pl.BlockSpec((TM, D_MODEL), lambda i: (i, 0)),
        out_shape=jax.ShapeDtypeStruct((t, D_MODEL), F32),
        compiler_params=_params(1, 32),
        name="odd_out",
    )(x, y, w_out, g, b)


def _rope_angles(seq, n_rot, theta):
    inv = np.exp(-math.log(theta) * np.arange(n_rot // 2, dtype=np.float64) * (2.0 / n_rot))
    return np.arange(seq, dtype=np.float64)[:, None] * inv[None, :]


def _even_rope_tables(seq):
    half = ROPE_DIMS // 2
    ang = _rope_angles(seq, ROPE_DIMS, ROPE_THETA)
    cos, sin = np.cos(ang), np.sin(ang)
    pad = np.zeros((seq, DIFF_DH - ROPE_DIMS))
    zeros = np.zeros((seq, half))
    c = np.concatenate([cos, cos, pad + 1.0], -1)
    sa = np.concatenate([-sin, zeros, pad], -1)
    sb = np.concatenate([zeros, sin, pad], -1)
    reps = LANES // DIFF_DH
    return tuple(jnp.asarray(np.tile(t, (1, reps)), F32) for t in (c, sa, sb))


def _odd_rope_tables(seq):
    ang = _rope_angles(seq, RET_DK, RET_ROPE_THETA)
    return jnp.asarray(np.cos(ang), F32), jnp.asarray(np.sin(ang), F32)


def _retention_tables():
    log_g = np.log1p(-np.exp2(-5.0 - np.arange(RET_HEADS, dtype=np.float64)))
    idx = np.arange(RET_BLOCK, dtype=np.float64)
    rel = idx[:, None] - idx[None, :]
    dmat = np.where(rel >= 0, np.exp(log_g[:, None, None] * np.maximum(rel, 0.0)), 0.0)
    q_decay = np.exp(log_g[:, None] * (idx + 1.0))
    k_decay = np.exp(log_g[:, None] * (RET_BLOCK - 1.0 - idx))
    cd = np.exp(log_g * RET_BLOCK)
    rep = lambda t: np.broadcast_to(t[:, :, None], (RET_HEADS, RET_BLOCK, LANES))
    return tuple(jnp.asarray(t, F32) for t in (cd, dmat, rep(q_decay), rep(k_decay)))


def kernel(x, ln_g, ln_b, ffn_w_in, ffn_w_out, even_w_in, even_w_out, diff_lambda,
           diff_norm_g, conv_w, odd_w_in, odd_w_out, ret_norm_g):
    batch, seq, _ = x.shape
    assert seq % TM == 0 and seq % ATT_TQ == 0 and DEPTH == 2
    h = x.reshape(batch * seq, D_MODEL)
    row = lambda v: v.reshape(1, -1)
    norm = lambda i, k: (row(ln_g[i, k]), row(ln_b[i, k]))
    ffn = lambda i, k: [(ffn_w_in, (i, k)), (ffn_w_out, (i, k))]


    h, (e_in, e_out) = _ffn_ln(h, ffn_w_in, ffn_w_out, *norm(0, 0), stream=(0, 0),
                               casts=[(even_w_in, (0,)), (even_w_out, (0,))])
    lambda_init = 0.8 - 0.6 * math.exp(-0.3 * 0)
    hcat, (w_in, w_out, *next_ffn) = _even_proj(
        h, e_in, *_even_rope_tables(seq), conv_w[0], seq, casts=ffn(0, 1) + ffn(1, 0))
    attn, _ = _diff_attention(hcat, diff_lambda[0], row(diff_norm_g[0]), batch, seq,
                              lambda_init)
    h = _even_out(h, attn, hcat, e_out, *norm(0, 1))
    h, (o_in, o_out) = _ffn_ln(h, w_in, w_out, *norm(0, 2),
                               casts=[(odd_w_in, (0,)), (odd_w_out, (0,))])

    h, last_ffn = _ffn_ln(h, *next_ffn, *norm(1, 0), casts=ffn(1, 1))
    y = _odd_mix(h, o_in, *_odd_rope_tables(seq), row(ret_norm_g[0]), _retention_tables(),
                 batch, seq)
    h = _odd_out(h, y, o_out, *norm(1, 1))
    h, _ = _ffn_ln(h, *last_ffn, *norm(1, 2))
    return h.reshape(batch, seq, D_MODEL)
```

```python
import functools
import math

import jax
import jax.numpy as jnp
import numpy as np
from jax import lax
from jax.experimental import pallas as pl
from jax.experimental.pallas import tpu as pltpu

F32 = jnp.float32
BF16 = jnp.bfloat16

D_MODEL = 1024
DEPTH = 2
D_FF = 2816
DIFF_HEADS = 4
DIFF_DH = 64
DIFF_DV = 128
DIFF_QK = 512
DIFF_WIDTH = 512
ROPE_THETA = 500000.0
ROPE_DIMS = 16
CONV_CH = 512
EVEN_IN = 3072
RET_HEADS = 4
RET_DK = 256
RET_DV = 512
RET_QK = 1024
RET_VW = 2048
RET_ROPE_THETA = 10000.0
ODD_IN = 6144
ALPHA = (2.0 * DEPTH) ** 0.25
LN_EPS = 1e-5
NEG_INF = -1e30
LOG2E = math.log2(math.e)

LANES = 128
MIB = 1024 * 1024
VMEM_LIMIT = 56 * MIB

TM = 1024
FF_CHUNK = 256
OUT_SPLIT = 4
FFN_ROW_GROUPS = (256, 256, 256, 256)
EVEN_OUT_TM = 1024
ODD_TM = 1024
RET_BLOCK = 256
ATT_TQ = 256
ATT_STRIP = 16
ATT_HEADS_PER_STEP = 1


def _layer_norm(z, g, b):
    mu = jnp.mean(z, -1, keepdims=True)
    d = z - mu
    var = jnp.mean(d * d, -1, keepdims=True)
    return d * lax.rsqrt(var + LN_EPS) * g + b


def _params(n_axes, vmem_mib):
    assert vmem_mib * MIB <= VMEM_LIMIT
    return pltpu.CompilerParams(dimension_semantics=("arbitrary",) * n_axes,
                                vmem_limit_bytes=vmem_mib * MIB)


def _resident(shape):
    nd = len(shape)
    return pl.BlockSpec(shape, lambda *_: (0,) * nd, pipeline_mode=pl.Buffered(1))


def _cast_specs(casts, grid):
    steps = math.prod(grid)

    def step(*idx):
        lin = idx[0]
        for size, i in zip(grid[1:], idx[1:]):
            lin = lin * size + i
        return lin

    in_specs, out_specs, out_shapes = [], [], []
    for arr, prefix in casts:
        r, c = arr.shape[-2:]
        assert arr.ndim == len(prefix) + 2
        rows = min(d for d in range(16, r + 1, 16) if r % d == 0 and d * steps >= r)
        last = r // rows - 1
        in_specs.append(pl.BlockSpec(
            (None,) * len(prefix) + (rows, c),
            lambda *idx, p=prefix, n=last: p + (jnp.minimum(step(*idx), n), 0)))
        out_specs.append(pl.BlockSpec(
            (rows, c), lambda *idx, n=last: (jnp.minimum(step(*idx), n), 0)))
        out_shapes.append(jax.ShapeDtypeStruct((r, c), BF16))
    return in_specs, out_specs, out_shapes


def _cast_vmem_mib(casts, grid):
    in_specs, _, _ = _cast_specs(casts, grid)
    elems = sum(math.prod(d for d in spec.block_shape if d is not None) for spec in in_specs)
    return -(-elems * (4 + 2) * 2 // MIB)


def _do_casts(src_refs, dst_refs):
    for src, dst in zip(src_refs, dst_refs):
        dst[...] = src[...].astype(BF16)


W_STREAM_STEPS = 16
W_STREAM_SLOTS = 3


def _stream_cast(pick, sem_ref, streams):
    def copy(s, c):
        src, _, stage = streams[s]
        slots, rows = stage.shape[:2]
        return pltpu.make_async_copy(src.at[pick[0], pick[1], pl.ds(c * rows, rows), :],
                                     stage.at[c % slots], sem_ref.at[s, c % slots])

    for s, (_, _, stage) in enumerate(streams):
        for c in range(stage.shape[0] - 1):
            copy(s, c).start()
    for c in range(W_STREAM_STEPS):
        for s, (_, dst, stage) in enumerate(streams):
            slots, rows = stage.shape[:2]
            if c + slots - 1 < W_STREAM_STEPS:
                copy(s, c + slots - 1).start()
            copy(s, c).wait()
            dst[c * rows:(c + 1) * rows, :] = stage[c % slots].astype(BF16)


def _ffn_kernel(x_ref, win_ref, wout_ref, g_ref, b_ref, *rest, n_cast, stream):
    cast_src, o_ref, cast_dst = rest[:n_cast], rest[n_cast], rest[n_cast + 1:2 * n_cast + 1]
    act_ref = rest[2 * n_cast + 1]
    if stream is not None:
        win_bf, wout_bf, stage_in, stage_out, sems = rest[2 * n_cast + 2:]

        @pl.when(pl.program_id(0) == 0)
        def _():
            _stream_cast(stream, sems, [(win_ref, win_bf, stage_in),
                                        (wout_ref, wout_bf, stage_out)])

        win_ref, wout_ref = win_bf, wout_bf
    _do_casts(cast_src, cast_dst)
    x = x_ref[...]
    xb = x.astype(BF16)
    for c in range(D_FF // FF_CHUNK):
        lo = c * FF_CHUNK
        gate = jnp.dot(xb, win_ref[:, lo:lo + FF_CHUNK], preferred_element_type=F32)
        up = jnp.dot(xb, win_ref[:, D_FF + lo:D_FF + lo + FF_CHUNK], preferred_element_type=F32)
        act_ref[:, lo:lo + FF_CHUNK] = (gate * jax.nn.sigmoid(gate) * up).astype(BF16)
    bounds = [sum(FFN_ROW_GROUPS[:r]) for r in range(len(FFN_ROW_GROUPS) + 1)]
    groups = [slice(lo, hi) for lo, hi in zip(bounds[:-1], bounds[1:])]
    ys = [jnp.dot(act_ref[sl, :], wout_ref[...], preferred_element_type=F32) for sl in groups]
    for sl, y in zip(groups, ys):
        o_ref[sl, :] = _layer_norm(ALPHA * x_ref[sl, :] + 0.5 * y, g_ref[...], b_ref[...])


def _ffn_ln(x, w_in, w_out, g, b, casts=(), stream=None):
    t = x.shape[0]
    steps = t // TM
    cast_in, cast_out, cast_shapes = _cast_specs(casts, (steps,))
    scratch = [pltpu.VMEM((TM, D_FF), BF16)]
    if stream is None:
        weights = [_resident((D_MODEL, 2 * D_FF)), _resident((D_FF, D_MODEL))]
        stage_mib = 0
    else:
        weights = [pl.BlockSpec(memory_space=pl.ANY)] * 2
        rows_in, rows_out = D_MODEL // W_STREAM_STEPS, D_FF // W_STREAM_STEPS
        scratch += [pltpu.VMEM((D_MODEL, 2 * D_FF), BF16), pltpu.VMEM((D_FF, D_MODEL), BF16),
                    pltpu.VMEM((W_STREAM_SLOTS, rows_in, 2 * D_FF), F32),
                    pltpu.VMEM((W_STREAM_SLOTS, rows_out, D_MODEL), F32),
                    pltpu.SemaphoreType.DMA((2, W_STREAM_SLOTS))]
        stage_mib = -(-W_STREAM_SLOTS * 4 * (rows_in * 2 * D_FF + rows_out * D_MODEL) // MIB)
    out = pl.pallas_call(
        functools.partial(_ffn_kernel, n_cast=len(casts), stream=stream),
        grid=(steps,),
        in_specs=[pl.BlockSpec((TM, D_MODEL), lambda i: (i, 0))] + weights
                 + [_resident((1, D_MODEL)), _resident((1, D_MODEL))] + cast_in,
        out_specs=[pl.BlockSpec((TM, D_MODEL), lambda i: (i, 0))] + cast_out,
        out_shape=[jax.ShapeDtypeStruct((t, D_MODEL), F32)] + cast_shapes,
        scratch_shapes=scratch,
        compiler_params=_params(1, 45 + stage_mib + _cast_vmem_mib(casts, (steps,))),
        name="ffn_ln",
    )(x, w_in, w_out, g, b, *[arr for arr, _ in casts])
    return out[0], out[1:]


EVEN_OUT_COLS = 2 * DIFF_QK + DIFF_WIDTH + CONV_CH
CONV_TAIL = 8


def _even_proj_kernel(x_ref, w_ref, c_ref, sa_ref, sb_ref, cw_ref, *rest, n_cast, per_seq):
    cast_src, o_ref, cast_dst, tail_ref = (rest[:n_cast], rest[n_cast],
                                           rest[n_cast + 1:2 * n_cast + 1], rest[-1])
    _do_casts(cast_src, cast_dst)
    i = pl.program_id(0)

    @pl.when(i == 0)
    def _():
        tail_ref[...] = jnp.zeros_like(tail_ref)

    xb = x_ref[...].astype(BF16)
    c = c_ref[...]
    sa = sa_ref[...]
    sb = sb_ref[...]
    proj = lambda n: jnp.dot(xb, w_ref[:, n * 512:(n + 1) * 512], preferred_element_type=F32)
    for n in range(2):
        r = proj(n)
        scale = DIFF_DH ** -0.5 * LOG2E if n == 0 else 1.0
        for j in range(512 // LANES):
            blk = r[:, j * LANES:(j + 1) * LANES]
            rot = (blk * c + pltpu.roll(blk, LANES - 8, 1) * sa
                   + pltpu.roll(blk, 8, 1) * sb)
            if n == 0:
                rot = rot * scale
            o_ref[:, n * 512 + j * LANES:n * 512 + (j + 1) * LANES] = rot.astype(BF16)
    u = proj(4) * proj(5)
    prev = jnp.where(i % per_seq == 0, jnp.zeros_like(u[:CONV_TAIL]), tail_ref[...])
    tail_ref[...] = u[TM - CONV_TAIL:]
    row = lax.broadcasted_iota(jnp.int32, (CONV_TAIL, CONV_CH), 0)
    u1 = pltpu.roll(u, 1, 0)
    u2 = pltpu.roll(u, 2, 0)
    head1 = jnp.where(row == 0, prev[CONV_TAIL - 1:], u1[:CONV_TAIL])
    head2 = jnp.where(row == 0, prev[CONV_TAIL - 2:CONV_TAIL - 1], u2[:CONV_TAIL])
    head2 = jnp.where(row == 1, prev[CONV_TAIL - 1:], head2)
    u1 = jnp.concatenate([head1, u1[CONV_TAIL:]], 0)
    u2 = jnp.concatenate([head2, u2[CONV_TAIL:]], 0)
    cw = cw_ref[...]
    conv = proj(3) * (cw[0:1] * u2 + cw[1:2] * u1 + cw[2:3] * u)
    o_ref[:, 3 * 512:] = conv.astype(BF16)
    o_ref[:, 2 * 512:3 * 512] = proj(2).astype(BF16)


def _even_proj(x, w, c, sa, sb, conv_w, seq, casts=()):
    t = x.shape[0]
    steps = t // TM
    per_seq = seq // TM
    tbl = pl.BlockSpec((TM, LANES), lambda i: (i % per_seq, 0))
    cast_in, cast_out, cast_shapes = _cast_specs(casts, (steps,))
    out = pl.pallas_call(
        functools.partial(_even_proj_kernel, n_cast=len(casts), per_seq=per_seq),
        grid=(steps,),
        in_specs=[pl.BlockSpec((TM, D_MODEL), lambda i: (i, 0)),
                  _resident((D_MODEL, EVEN_IN)), tbl, tbl, tbl,
                  _resident((3, CONV_CH))] + cast_in,
        out_specs=[pl.BlockSpec((TM, EVEN_OUT_COLS), lambda i: (i, 0))] + cast_out,
        out_shape=[jax.ShapeDtypeStruct((t, EVEN_OUT_COLS), BF16)] + cast_shapes,
        scratch_shapes=[pltpu.VMEM((CONV_TAIL, CONV_CH), F32)],
        compiler_params=_params(1, 28 + _cast_vmem_mib(casts, (steps,))),
        name="even_proj",
    )(x, w, c, sa, sb, conv_w, *[arr for arr, _ in casts])
    return out[0], out[1:]


def _attn_kernel(q_ref, k_ref, v_ref, lv_ref, g_ref, *rest, lambda_init, seq, n_cast):
    cast_src, o_ref, cast_dst, s_ref, p_ref = (rest[:n_cast], rest[n_cast],
                                               rest[n_cast + 1:2 * n_cast + 1], rest[-2], rest[-1])
    _do_casts(cast_src, cast_dst)
    lv = lv_ref[...]
    lam = (jnp.exp(jnp.sum(lv[0:1] * lv[1:2], -1, keepdims=True))
           - jnp.exp(jnp.sum(lv[2:3] * lv[3:4], -1, keepdims=True)) + lambda_init)
    lane = lax.broadcasted_iota(jnp.int32, (ATT_TQ, LANES), 1)
    row = lax.broadcasted_iota(jnp.int32, (ATT_TQ, ATT_TQ), 0)
    col = lax.broadcasted_iota(jnp.int32, (ATT_TQ, ATT_TQ), 1)
    causal = col <= row
    gain = g_ref[...]
    nt = (((1,), (1,)), ((), ()))
    n_tiles = seq // ATT_TQ
    order = [t for pair in zip(range(n_tiles - 1, -1, -1), range(n_tiles)) for t in pair][:n_tiles]
    for pos, i in enumerate(order):
        for j in range(ATT_HEADS_PER_STEP):
            head = slice(j * LANES, (j + 1) * LANES)
            buf = pos % 2
            lo = i * ATT_TQ
            q = q_ref[lo:lo + ATT_TQ, head]
            zero = jnp.zeros_like(q)
            qs = (jnp.where(lane < DIFF_DH, q, zero), jnp.where(lane >= DIFF_DH, q, zero))
            hi = lo + ATT_TQ
            l = []
            for c, qc in enumerate(qs):
                s = lax.dot_general(qc, k_ref[:hi, head], nt, preferred_element_type=F32)
                sc_ref = s_ref.at[j, buf, c]
                if i > 0:
                    sc_ref[:, :lo] = s[:, :lo]
                sc_ref[:, lo:hi] = jnp.where(causal, s[:, lo:], NEG_INF)
                m = jnp.max(sc_ref[:, :hi], -1, keepdims=True)
                lsum = []
                for r in range(ATT_TQ // ATT_STRIP):
                    sl = slice(r * ATT_STRIP, (r + 1) * ATT_STRIP)
                    p = jnp.exp2(sc_ref[sl, :hi] - m[sl])
                    lsum.append(jnp.sum(p, -1, keepdims=True))
                    base = c * ATT_TQ + r * ATT_STRIP
                    p_ref[j, buf, base:base + ATT_STRIP, :hi] = p.astype(BF16)
                l.append(jnp.concatenate(lsum, 0))
            acc = jnp.dot(p_ref[j, buf, :, :hi], v_ref[:hi, head], preferred_element_type=F32)
            acc = acc[:ATT_TQ] * (1.0 / l[0]) - acc[ATT_TQ:] * (lam / l[1])
            y = acc * lax.rsqrt(jnp.mean(acc * acc, -1, keepdims=True) + LN_EPS)
            o_ref[lo:lo + ATT_TQ, head] = (y * gain * (1.0 - lambda_init)).astype(BF16)


def _diff_attention(hcat, lam_vecs, norm_g, batch, seq, lambda_init, casts=()):
    hp = ATT_HEADS_PER_STEP
    width = hp * LANES
    grid = (batch, DIFF_HEADS // hp)
    cast_in, cast_out, cast_shapes = _cast_specs(casts, grid)
    out = pl.pallas_call(
        functools.partial(_attn_kernel, lambda_init=lambda_init, seq=seq, n_cast=len(casts)),
        grid=grid,
        in_specs=[pl.BlockSpec((seq, width), lambda b, h: (b, h)),
                  pl.BlockSpec((seq, width), lambda b, h: (b, DIFF_QK // width + h)),
                  pl.BlockSpec((seq, width), lambda b, h: (b, 2 * DIFF_QK // width + h)),
                  _resident((4, DIFF_DH)),
                  _resident((1, DIFF_DV))] + cast_in,
        out_specs=[pl.BlockSpec((seq, width), lambda b, h: (b, h))] + cast_out,
        out_shape=[jax.ShapeDtypeStruct((batch * seq, DIFF_WIDTH), BF16)] + cast_shapes,
        scratch_shapes=[pltpu.VMEM((hp, 2, 2, ATT_TQ, seq), F32),
                        pltpu.VMEM((hp, 2, 2 * ATT_TQ, seq), BF16)],
        compiler_params=_params(2, 24 + _cast_vmem_mib(casts, grid)),
        name="diff_attn",
    )(hcat, hcat, hcat, lam_vecs, norm_g, *[arr for arr, _ in casts])
    return out[0], out[1:]


def _even_out_kernel(x_ref, attn_ref, conv_ref, wout_ref, g_ref, b_ref, o_ref):
    rows = TM // OUT_SPLIT
    groups = [slice(lo, lo + rows) for lo in range(0, EVEN_OUT_TM, rows)]
    ys = []
    for sl in groups:
        mix = jnp.concatenate([attn_ref[sl, :], conv_ref[sl, :]], 1)
        ys.append(jnp.dot(mix, wout_ref[...], preferred_element_type=F32))
    for sl, y in zip(groups, ys):
        o_ref[sl, :] = _layer_norm(ALPHA * x_ref[sl, :] + y, g_ref[...], b_ref[...])


def _even_out(x, attn, hcat, w_out, g, b):
    t = x.shape[0]
    tm = EVEN_OUT_TM
    conv_col = (2 * DIFF_QK + DIFF_WIDTH) // CONV_CH
    return pl.pallas_call(
        _even_out_kernel,
        grid=(t // tm,),
        in_specs=[pl.BlockSpec((tm, D_MODEL), lambda i: (i, 0)),
                  pl.BlockSpec((tm, DIFF_WIDTH), lambda i: (i, 0)),
                  pl.BlockSpec((tm, CONV_CH), lambda i: (i, conv_col)),
                  _resident((DIFF_WIDTH + CONV_CH, D_MODEL)),
                  _resident((1, D_MODEL)), _resident((1, D_MODEL))],
        out_specs=pl.BlockSpec((tm, D_MODEL), lambda i: (i, 0)),
        out_shape=jax.ShapeDtypeStruct((t, D_MODEL), F32),
        compiler_params=_params(1, 24),
        name="even_out",
    )(x, attn, hcat, w_out, g, b)


def _odd_mix_kernel(cd_ref, x_ref, w_ref, cos_ref, sin_ref, ng_ref, dmat_ref, qd_ref, kd_ref,
                    o_ref, state_ref):
    @pl.when(pl.program_id(1) == 0)
    def _():
        state_ref[...] = jnp.zeros_like(state_ref)

    xb = x_ref[...].astype(BF16)
    cos = cos_ref[...]
    sin = sin_ref[...]
    half = RET_DK // 2
    k0, v0, g0 = RET_QK, 2 * RET_QK, 2 * RET_QK + RET_VW

    def proj(lo, n):
        return jnp.dot(xb, w_ref[:, lo:lo + n], preferred_element_type=F32)

    def rotary(r):
        x1, x2 = r[:, :half], r[:, half:]
        return jnp.concatenate([x1 * cos - x2 * sin, x2 * cos + x1 * sin], 1)

    def project(j):
        q = rotary(proj(j * RET_DK, RET_DK)).astype(BF16)
        k = rotary(proj(k0 + j * RET_DK, RET_DK)) * RET_DK ** -0.5
        v = proj(v0 + j * RET_DV, RET_DV).astype(BF16)
        g = proj(g0 + j * RET_DV, RET_DV)
        gate = g * jax.nn.sigmoid(g) * ng_ref[:, j * RET_DV:(j + 1) * RET_DV]
        return q, k, k.astype(BF16), v, gate

    ahead = project(0)
    for j in range(RET_HEADS):
        q, k, kb, v, gate = ahead
        if j + 1 < RET_HEADS:
            ahead = project(j + 1)
        cd = cd_ref[j]
        dmat = dmat_ref[j]
        qd = jnp.concatenate([qd_ref[j]] * (RET_DV // LANES), axis=1)
        kd = jnp.concatenate([kd_ref[j]] * (RET_DK // LANES), axis=1)
        state = state_ref[j]
        for c in range(ODD_TM // RET_BLOCK):
            rows = slice(c * RET_BLOCK, (c + 1) * RET_BLOCK)
            qc, kc, vc = q[rows], kb[rows], v[rows]
            sc = lax.dot_general(qc, kc, (((1,), (1,)), ((), ())),
                                 preferred_element_type=F32) * dmat
            y = jnp.dot(sc.astype(BF16), vc, preferred_element_type=F32)
            y = y + jnp.dot(qc, state.astype(BF16), preferred_element_type=F32) * qd
            kdec = (k[rows] * kd).astype(BF16)
            state = state * cd + lax.dot_general(kdec, vc, (((0,), (0,)), ((), ())),
                                                 preferred_element_type=F32)
            mu = jnp.mean(y, -1, keepdims=True)
            d = y - mu
            var = jnp.mean(d * d, -1, keepdims=True)
            yn = d * lax.rsqrt(var + LN_EPS)
            o_ref[rows, j * RET_DV:(j + 1) * RET_DV] = (gate[rows] * yn).astype(BF16)
        state_ref[j] = state


def _odd_mix(x, w, cos, sin, norm_g, tables, batch, seq):
    cd, dmat, qd, kd = tables
    per_seq = seq // ODD_TM
    tile = lambda b, p: (b * per_seq + p, 0)
    tbl = pl.BlockSpec((ODD_TM, LANES), lambda b, p: (p, 0))
    return pl.pallas_call(
        _odd_mix_kernel,
        grid=(batch, per_seq),
        in_specs=[pl.BlockSpec(memory_space=pltpu.SMEM),
                  pl.BlockSpec((ODD_TM, D_MODEL), tile),
                  _resident((D_MODEL, ODD_IN)), tbl, tbl, _resident((1, RET_VW)),
                  _resident((RET_HEADS, RET_BLOCK, RET_BLOCK)),
                  _resident((RET_HEADS, RET_BLOCK, LANES)),
                  _resident((RET_HEADS, RET_BLOCK, LANES))],
        out_specs=pl.BlockSpec((ODD_TM, RET_VW), tile),
        out_shape=jax.ShapeDtypeStruct((batch * seq, RET_VW), BF16),
        scratch_shapes=[pltpu.VMEM((RET_HEADS, RET_DK, RET_DV), F32)],
        compiler_params=_params(2, 52),
        name="odd_mix",
    )(cd, x, w, cos, sin, norm_g, dmat, qd, kd)


def _odd_out_kernel(x_ref, y_ref, wout_ref, g_ref, b_ref, o_ref):
    rows = TM // OUT_SPLIT
    ys = [jnp.dot(y_ref[r * rows:(r + 1) * rows, :], wout_ref[...],
                  preferred_element_type=F32) for r in range(OUT_SPLIT)]
    for r in range(OUT_SPLIT):
        sl = slice(r * rows, (r + 1) * rows)
        o_ref[sl, :] = _layer_norm(ALPHA * x_ref[sl, :] + ys[r], g_ref[...], b_ref[...])


def _odd_out(x, y, w_out, g, b):
    t = x.shape[0]
    return pl.pallas_call(
        _odd_out_kernel,
        grid=(t // TM,),
        in_specs=[pl.BlockSpec((TM, D_MODEL), lambda i: (i, 0)),
                  pl.BlockSpec((TM, RET_VW), lambda i: (i, 0)),
                  _resident((RET_VW, D_MODEL)),
                  _resident((1, D_MODEL)), _resident((1, D_MODEL))],
        out_specs=pl.BlockSpec((TM, D_MODEL), lambda i: (i, 0)),
        out_shape=jax.ShapeDtypeStruct((t, D_MODEL), F32),
        compiler_params=_params(1, 32),
        name="odd_out",
    )(x, y, w_out, g, b)


def _rope_angles(seq, n_rot, theta):
    inv = np.exp(-math.log(theta) * np.arange(n_rot // 2, dtype=np.float64) * (2.0 / n_rot))
    return np.arange(seq, dtype=np.float64)[:, None] * inv[None, :]


def _even_rope_tables(seq):
    half = ROPE_DIMS // 2
    ang = _rope_angles(seq, ROPE_DIMS, ROPE_THETA)
    cos, sin = np.cos(ang), np.sin(ang)
    pad = np.zeros((seq, DIFF_DH - ROPE_DIMS))
    zeros = np.zeros((seq, half))
    c = np.concatenate([cos, cos, pad + 1.0], -1)
    sa = np.concatenate([-sin, zeros, pad], -1)
    sb = np.concatenate([zeros, sin, pad], -1)
    reps = LANES // DIFF_DH
    return tuple(jnp.asarray(np.tile(t, (1, reps)), F32) for t in (c, sa, sb))


def _odd_rope_tables(seq):
    ang = _rope_angles(seq, RET_DK, RET_ROPE_THETA)
    return jnp.asarray(np.cos(ang), F32), jnp.asarray(np.sin(ang), F32)


def _retention_tables():
    log_g = np.log1p(-np.exp2(-5.0 - np.arange(RET_HEADS, dtype=np.float64)))
    idx = np.arange(RET_BLOCK, dtype=np.float64)
    rel = idx[:, None] - idx[None, :]
    dmat = np.where(rel >= 0, np.exp(log_g[:, None, None] * np.maximum(rel, 0.0)), 0.0)
    q_decay = np.exp(log_g[:, None] * (idx + 1.0))
    k_decay = np.exp(log_g[:, None] * (RET_BLOCK - 1.0 - idx))
    cd = np.exp(log_g * RET_BLOCK)
    rep = lambda t: np.broadcast_to(t[:, :, None], (RET_HEADS, RET_BLOCK, LANES))
    return tuple(jnp.asarray(t, F32) for t in (cd, dmat, rep(q_decay), rep(k_decay)))


def kernel(x, ln_g, ln_b, ffn_w_in, ffn_w_out, even_w_in, even_w_out, diff_lambda,
           diff_norm_g, conv_w, odd_w_in, odd_w_out, ret_norm_g):
    batch, seq, _ = x.shape
    assert seq % TM == 0 and seq % ATT_TQ == 0 and DEPTH == 2
    h = x.reshape(batch * seq, D_MODEL)
    row = lambda v: v.reshape(1, -1)
    norm = lambda i, k: (row(ln_g[i, k]), row(ln_b[i, k]))
    ffn = lambda i, k: [(ffn_w_in, (i, k)), (ffn_w_out, (i, k))]


    h, (e_in, e_out) = _ffn_ln(h, ffn_w_in, ffn_w_out, *norm(0, 0), stream=(0, 0),
                               casts=[(even_w_in, (0,)), (even_w_out, (0,))])
    lambda_init = 0.8 - 0.6 * math.exp(-0.3 * 0)
    hcat, (w_in, w_out, *next_ffn) = _even_proj(
        h, e_in, *_even_rope_tables(seq), conv_w[0], seq, casts=ffn(0, 1) + ffn(1, 0))
    attn, _ = _diff_attention(hcat, diff_lambda[0], row(diff_norm_g[0]), batch, seq,
                              lambda_init)
    h = _even_out(h, attn, hcat, e_out, *norm(0, 1))
    h, (o_in, o_out) = _ffn_ln(h, w_in, w_out, *norm(0, 2),
                               casts=[(odd_w_in, (0,)), (odd_w_out, (0,))])

    h, last_ffn = _ffn_ln(h, *next_ffn, *norm(1, 0), casts=ffn(1, 1))
    y = _odd_mix(h, o_in, *_odd_rope_tables(seq), row(ret_norm_g[0]), _retention_tables(),
                 batch, seq)
    h = _odd_out(h, y, o_out, *norm(1, 1))
    h, _ = _ffn_ln(h, *last_ffn, *norm(1, 2))
    return h.reshape(batch, seq, D_MODEL)
```

```python
import functools
import math

import jax
import jax.numpy as jnp
import numpy as np
from jax import lax
from jax.experimental import pallas as pl
from jax.experimental.pallas import tpu as pltpu

F32 = jnp.float32
BF16 = jnp.bfloat16

D_MODEL = 1024
DEPTH = 2
D_FF = 2816
DIFF_HEADS = 4
DIFF_DH = 64
DIFF_DV = 128
DIFF_QK = 512
DIFF_WIDTH = 512
ROPE_THETA = 500000.0
ROPE_DIMS = 16
CONV_CH = 512
EVEN_IN = 3072
RET_HEADS = 4
RET_DK = 256
RET_DV = 512
RET_QK = 1024
RET_VW = 2048
RET_ROPE_THETA = 10000.0
ODD_IN = 6144
ALPHA = (2.0 * DEPTH) ** 0.25
LN_EPS = 1e-5
NEG_INF = -1e30
LOG2E = math.log2(math.e)

LANES = 128
MIB = 1024 * 1024
VMEM_LIMIT = 56 * MIB

TM = 1024
FF_CHUNK = 256
OUT_SPLIT = 4
FFN_ROW_GROUPS = (256, 256, 256, 256)
EVEN_OUT_TM = 1024
ODD_TM = 1024
RET_BLOCK = 256
ATT_TQ = 256
ATT_STRIP = 16
ATT_HEADS_PER_STEP = 1


def _layer_norm(z, g, b):
    mu = jnp.mean(z, -1, keepdims=True)
    d = z - mu
    var = jnp.mean(d * d, -1, keepdims=True)
    return d * lax.rsqrt(var + LN_EPS) * g + b


def _params(n_axes, vmem_mib):
    assert vmem_mib * MIB <= VMEM_LIMIT
    return pltpu.CompilerParams(dimension_semantics=("arbitrary",) * n_axes,
                                vmem_limit_bytes=vmem_mib * MIB)


def _resident(shape):
    nd = len(shape)
    return pl.BlockSpec(shape, lambda *_: (0,) * nd, pipeline_mode=pl.Buffered(1))


def _cast_specs(casts, grid):
    steps = math.prod(grid)

    def step(*idx):
        lin = idx[0]
        for size, i in zip(grid[1:], idx[1:]):
            lin = lin * size + i
        return lin

    in_specs, out_specs, out_shapes = [], [], []
    for arr, prefix in casts:
        r, c = arr.shape[-2:]
        assert arr.ndim == len(prefix) + 2
        rows = min(d for d in range(16, r + 1, 16) if r % d == 0 and d * steps >= r)
        last = r // rows - 1
        in_specs.append(pl.BlockSpec(
            (None,) * len(prefix) + (rows, c),
            lambda *idx, p=prefix, n=last: p + (jnp.minimum(step(*idx), n), 0)))
        out_specs.append(pl.BlockSpec(
            (rows, c), lambda *idx, n=last: (jnp.minimum(step(*idx), n), 0)))
        out_shapes.append(jax.ShapeDtypeStruct((r, c), BF16))
    return in_specs, out_specs, out_shapes


def _cast_vmem_mib(casts, grid):
    in_specs, _, _ = _cast_specs(casts, grid)
    elems = sum(math.prod(d for d in spec.block_shape if d is not None) for spec in in_specs)
    return -(-elems * (4 + 2) * 2 // MIB)


def _do_casts(src_refs, dst_refs):
    for src, dst in zip(src_refs, dst_refs):
        dst[...] = src[...].astype(BF16)


W_STREAM_STEPS = 16
W_STREAM_SLOTS = 3


def _stream_cast(pick, sem_ref, streams):
    def copy(s, c):
        src, _, stage = streams[s]
        slots, rows = stage.shape[:2]
        return pltpu.make_async_copy(src.at[pick[0], pick[1], pl.ds(c * rows, rows), :],
                                     stage.at[c % slots], sem_ref.at[s, c % slots])

    for s, (_, _, stage) in enumerate(streams):
        for c in range(stage.shape[0] - 1):
            copy(s, c).start()
    for c in range(W_STREAM_STEPS):
        for s, (_, dst, stage) in enumerate(streams):
            slots, rows = stage.shape[:2]
            if c + slots - 1 < W_STREAM_STEPS:
                copy(s, c + slots - 1).start()
            copy(s, c).wait()
            dst[c * rows:(c + 1) * rows, :] = stage[c % slots].astype(BF16)


def _ffn_kernel(x_ref, win_ref, wout_ref, g_ref, b_ref, *rest, n_cast, stream):
    cast_src, o_ref, cast_dst = rest[:n_cast], rest[n_cast], rest[n_cast + 1:2 * n_cast + 1]
    act_ref = rest[2 * n_cast + 1]
    if stream is not None:
        win_bf, wout_bf, stage_in, stage_out, sems = rest[2 * n_cast + 2:]

        @pl.when(pl.program_id(0) == 0)
        def _():
            _stream_cast(stream, sems, [(win_ref, win_bf, stage_in),
                                        (wout_ref, wout_bf, stage_out)])

        win_ref, wout_ref = win_bf, wout_bf
    _do_casts(cast_src, cast_dst)
    x = x_ref[...]
    xb = x.astype(BF16)
    for c in range(D_FF // FF_CHUNK):
        lo = c * FF_CHUNK
        gate = jnp.dot(xb, win_ref[:, lo:lo + FF_CHUNK], preferred_element_type=F32)
        up = jnp.dot(xb, win_ref[:, D_FF + lo:D_FF + lo + FF_CHUNK], preferred_element_type=F32)
        act_ref[:, lo:lo + FF_CHUNK] = (gate * jax.nn.sigmoid(gate) * up).astype(BF16)
    bounds = [sum(FFN_ROW_GROUPS[:r]) for r in range(len(FFN_ROW_GROUPS) + 1)]
    groups = [slice(lo, hi) for lo, hi in zip(bounds[:-1], bounds[1:])]
    ys = [jnp.dot(act_ref[sl, :], wout_ref[...], preferred_element_type=F32) for sl in groups]
    for sl, y in zip(groups, ys):
        o_ref[sl, :] = _layer_norm(ALPHA * x_ref[sl, :] + 0.5 * y, g_ref[...], b_ref[...])


def _ffn_ln(x, w_in, w_out, g, b, casts=(), stream=None):
    t = x.shape[0]
    steps = t // TM
    cast_in, cast_out, cast_shapes = _cast_specs(casts, (steps,))
    scratch = [pltpu.VMEM((TM, D_FF), BF16)]
    if stream is None:
        weights = [_resident((D_MODEL, 2 * D_FF)), _resident((D_FF, D_MODEL))]
        stage_mib = 0
    else:
        weights = [pl.BlockSpec(memory_space=pl.ANY)] * 2
        rows_in, rows_out = D_MODEL // W_STREAM_STEPS, D_FF // W_STREAM_STEPS
        scratch += [pltpu.VMEM((D_MODEL, 2 * D_FF), BF16), pltpu.VMEM((D_FF, D_MODEL), BF16),
                    pltpu.VMEM((W_STREAM_SLOTS, rows_in, 2 * D_FF), F32),
                    pltpu.VMEM((W_STREAM_SLOTS, rows_out, D_MODEL), F32),
                    pltpu.SemaphoreType.DMA((2, W_STREAM_SLOTS))]
        stage_mib = -(-W_STREAM_SLOTS * 4 * (rows_in * 2 * D_FF + rows_out * D_MODEL) // MIB)
    out = pl.pallas_call(
        functools.partial(_ffn_kernel, n_cast=len(casts), stream=stream),
        grid=(steps,),
        in_specs=[pl.BlockSpec((TM, D_MODEL), lambda i: (i, 0))] + weights
                 + [_resident((1, D_MODEL)), _resident((1, D_MODEL))] + cast_in,
        out_specs=[pl.BlockSpec((TM, D_MODEL), lambda i: (i, 0))] + cast_out,
        out_shape=[jax.ShapeDtypeStruct((t, D_MODEL), F32)] + cast_shapes,
        scratch_shapes=scratch,
        compiler_params=_params(1, 45 + stage_mib + _cast_vmem_mib(casts, (steps,))),
        name="ffn_ln",
    )(x, w_in, w_out, g, b, *[arr for arr, _ in casts])
    return out[0], out[1:]


EVEN_OUT_COLS = 2 * DIFF_QK + DIFF_WIDTH + CONV_CH
CONV_TAIL = 8


def _even_proj_kernel(x_ref, w_ref, c_ref, sa_ref, sb_ref, cw_ref, *rest, n_cast, per_seq):
    cast_src, o_ref, cast_dst, tail_ref = (rest[:n_cast], rest[n_cast],
                                           rest[n_cast + 1:2 * n_cast + 1], rest[-1])
    _do_casts(cast_src, cast_dst)
    i = pl.program_id(0)

    @pl.when(i == 0)
    def _():
        tail_ref[...] = jnp.zeros_like(tail_ref)

    xb = x_ref[...].astype(BF16)
    c = c_ref[...]
    sa = sa_ref[...]
    sb = sb_ref[...]
    proj = lambda n: jnp.dot(xb, w_ref[:, n * 512:(n + 1) * 512], preferred_element_type=F32)
    for n in range(2):
        r = proj(n)
        scale = DIFF_DH ** -0.5 * LOG2E if n == 0 else 1.0
        for j in range(512 // LANES):
            blk = r[:, j * LANES:(j + 1) * LANES]
            rot = (blk * c + pltpu.roll(blk, LANES - 8, 1) * sa
                   + pltpu.roll(blk, 8, 1) * sb)
            if n == 0:
                rot = rot * scale
            o_ref[:, n * 512 + j * LANES:n * 512 + (j + 1) * LANES] = rot.astype(BF16)
    u = proj(4) * proj(5)
    prev = jnp.where(i % per_seq == 0, jnp.zeros_like(u[:CONV_TAIL]), tail_ref[...])
    tail_ref[...] = u[TM - CONV_TAIL:]
    row = lax.broadcasted_iota(jnp.int32, (CONV_TAIL, CONV_CH), 0)
    u1 = pltpu.roll(u, 1, 0)
    u2 = pltpu.roll(u, 2, 0)
    head1 = jnp.where(row == 0, prev[CONV_TAIL - 1:], u1[:CONV_TAIL])
    head2 = jnp.where(row == 0, prev[CONV_TAIL - 2:CONV_TAIL - 1], u2[:CONV_TAIL])
    head2 = jnp.where(row == 1, prev[CONV_TAIL - 1:], head2)
    u1 = jnp.concatenate([head1, u1[CONV_TAIL:]], 0)
    u2 = jnp.concatenate([head2, u2[CONV_TAIL:]], 0)
    cw = cw_ref[...]
    conv = proj(3) * (cw[0:1] * u2 + cw[1:2] * u1 + cw[2:3] * u)
    o_ref[:, 3 * 512:] = conv.astype(BF16)
    o_ref[:, 2 * 512:3 * 512] = proj(2).astype(BF16)


def _even_proj(x, w, c, sa, sb, conv_w, seq, casts=()):
    t = x.shape[0]
    steps = t // TM
    per_seq = seq // TM
    tbl = pl.BlockSpec((TM, LANES), lambda i: (i % per_seq, 0))
    cast_in, cast_out, cast_shapes = _cast_specs(casts, (steps,))
    out = pl.pallas_call(
        functools.partial(_even_proj_kernel, n_cast=len(casts), per_seq=per_seq),
        grid=(steps,),
        in_specs=[pl.BlockSpec((TM, D_MODEL), lambda i: (i, 0)),
                  _resident((D_MODEL, EVEN_IN)), tbl, tbl, tbl,
                  _resident((3, CONV_CH))] + cast_in,
        out_specs=[pl.BlockSpec((TM, EVEN_OUT_COLS), lambda i: (i, 0))] + cast_out,
        out_shape=[jax.ShapeDtypeStruct((t, EVEN_OUT_COLS), BF16)] + cast_shapes,
        scratch_shapes=[pltpu.VMEM((CONV_TAIL, CONV_CH), F32)],
        compiler_params=_params(1, 28 + _cast_vmem_mib(casts, (steps,))),
        name="even_proj",
    )(x, w, c, sa, sb, conv_w, *[arr for arr, _ in casts])
    return out[0], out[1:]


def _attn_kernel(q_ref, k_ref, v_ref, lv_ref, g_ref, *rest, lambda_init, seq, n_cast):
    cast_src, o_ref, cast_dst, s_ref, p_ref = (rest[:n_cast], rest[n_cast],
                                               rest[n_cast + 1:2 * n_cast + 1], rest[-2], rest[-1])
    _do_casts(cast_src, cast_dst)
    lv = lv_ref[...]
    lam = (jnp.exp(jnp.sum(lv[0:1] * lv[1:2], -1, keepdims=True))
           - jnp.exp(jnp.sum(lv[2:3] * lv[3:4], -1, keepdims=True)) + lambda_init)
    lane = lax.broadcasted_iota(jnp.int32, (ATT_TQ, LANES), 1)
    row = lax.broadcasted_iota(jnp.int32, (ATT_TQ, ATT_TQ), 0)
    col = lax.broadcasted_iota(jnp.int32, (ATT_TQ, ATT_TQ), 1)
    causal = col <= row
    gain = g_ref[...]
    nt = (((1,), (1,)), ((), ()))
    n_tiles = seq // ATT_TQ
    order = [t for pair in zip(range(n_tiles - 1, -1, -1), range(n_tiles)) for t in pair][:n_tiles]
    for pos, i in enumerate(order):
        for j in range(ATT_HEADS_PER_STEP):
            head = slice(j * LANES, (j + 1) * LANES)
            buf = pos % 2
            lo = i * ATT_TQ
            q = q_ref[lo:lo + ATT_TQ, head]
            zero = jnp.zeros_like(q)
            qs = (jnp.where(lane < DIFF_DH, q, zero), jnp.where(lane >= DIFF_DH, q, zero))
            hi = lo + ATT_TQ
            l = []
            for c, qc in enumerate(qs):
                s = lax.dot_general(qc, k_ref[:hi, head], nt, preferred_element_type=F32)
                sc_ref = s_ref.at[j, buf, c]
                if i > 0:
                    sc_ref[:, :lo] = s[:, :lo]
                sc_ref[:, lo:hi] = jnp.where(causal, s[:, lo:], NEG_INF)
                m = jnp.max(sc_ref[:, :hi], -1, keepdims=True)
                lsum = []
                for r in range(ATT_TQ // ATT_STRIP):
                    sl = slice(r * ATT_STRIP, (r + 1) * ATT_STRIP)
                    p = jnp.exp2(sc_ref[sl, :hi] - m[sl])
                    lsum.append(jnp.sum(p, -1, keepdims=True))
                    base = c * ATT_TQ + r * ATT_STRIP
                    p_ref[j, buf, base:base + ATT_STRIP, :hi] = p.astype(BF16)
                l.append(jnp.concatenate(lsum, 0))
            acc = jnp.dot(p_ref[j, buf, :, :hi], v_ref[:hi, head], preferred_element_type=F32)
            acc = acc[:ATT_TQ] * (1.0 / l[0]) - acc[ATT_TQ:] * (lam / l[1])
            y = acc * lax.rsqrt(jnp.mean(acc * acc, -1, keepdims=True) + LN_EPS)
            o_ref[lo:lo + ATT_TQ, head] = (y * gain * (1.0 - lambda_init)).astype(BF16)


def _diff_attention(hcat, lam_vecs, norm_g, batch, seq, lambda_init, casts=()):
    hp = ATT_HEADS_PER_STEP
    width = hp * LANES
    grid = (batch, DIFF_HEADS // hp)
    cast_in, cast_out, cast_shapes = _cast_specs(casts, grid)
    out = pl.pallas_call(
        functools.partial(_attn_kernel, lambda_init=lambda_init, seq=seq, n_cast=len(casts)),
        grid=grid,
        in_specs=[pl.BlockSpec((seq, width), lambda b, h: (b, h)),
                  pl.BlockSpec((seq, width), lambda b, h: (b, DIFF_QK // width + h)),
                  pl.BlockSpec((seq, width), lambda b, h: (b, 2 * DIFF_QK // width + h)),
                  _resident((4, DIFF_DH)),
                  _resident((1, DIFF_DV))] + cast_in,
        out_specs=[pl.BlockSpec((seq, width), lambda b, h: (b, h))] + cast_out,
        out_shape=[jax.ShapeDtypeStruct((batch * seq, DIFF_WIDTH), BF16)] + cast_shapes,
        scratch_shapes=[pltpu.VMEM((hp, 2, 2, ATT_TQ, seq), F32),
                        pltpu.VMEM((hp, 2, 2 * ATT_TQ, seq), BF16)],
        compiler_params=_params(2, 24 + _cast_vmem_mib(casts, grid)),
        name="diff_attn",
    )(hcat, hcat, hcat, lam_vecs, norm_g, *[arr for arr, _ in casts])
    return out[0], out[1:]


def _even_out_kernel(x_ref, attn_ref, conv_ref, wout_ref, g_ref, b_ref, o_ref):
    rows = TM // OUT_SPLIT
    groups = [slice(lo, lo + rows) for lo in range(0, EVEN_OUT_TM, rows)]
    ys = []
    for sl in groups:
        mix = jnp.concatenate([attn_ref[sl, :], conv_ref[sl, :]], 1)
        ys.append(jnp.dot(mix, wout_ref[...], preferred_element_type=F32))
    for sl, y in zip(groups, ys):
        o_ref[sl, :] = _layer_norm(ALPHA * x_ref[sl, :] + y, g_ref[...], b_ref[...])


def _even_out(x, attn, hcat, w_out, g, b):
    t = x.shape[0]
    tm = EVEN_OUT_TM
    conv_col = (2 * DIFF_QK + DIFF_WIDTH) // CONV_CH
    return pl.pallas_call(
        _even_out_kernel,
        grid=(t // tm,),
        in_specs=[pl.BlockSpec((tm, D_MODEL), lambda i: (i, 0)),
                  pl.BlockSpec((tm, DIFF_WIDTH), lambda i: (i, 0)),
                  pl.BlockSpec((tm, CONV_CH), lambda i: (i, conv_col)),
                  _resident((DIFF_WIDTH + CONV_CH, D_MODEL)),
                  _resident((1, D_MODEL)), _resident((1, D_MODEL))],
        out_specs=pl.BlockSpec((tm, D_MODEL), lambda i: (i, 0)),
        out_shape=jax.ShapeDtypeStruct((t, D_MODEL), F32),
        compiler_params=_params(1, 24),
        name="even_out",
    )(x, attn, hcat, w_out, g, b)


def _odd_mix_kernel(cd_ref, x_ref, w_ref, cos_ref, sin_ref, ng_ref, dmat_ref, qd_ref, kd_ref,
                    wout_ref, g_ref, b_ref, o_ref, state_ref, y_ref):
    @pl.when(pl.program_id(1) == 0)
    def _():
        state_ref[...] = jnp.zeros_like(state_ref)

    xb = x_ref[...].astype(BF16)
    cos = cos_ref[...]
    sin = sin_ref[...]
    half = RET_DK // 2
    k0, v0, g0 = RET_QK, 2 * RET_QK, 2 * RET_QK + RET_VW

    def proj(lo, n):
        return jnp.dot(xb, w_ref[:, lo:lo + n], preferred_element_type=F32)

    def rotary(r):
        x1, x2 = r[:, :half], r[:, half:]
        return jnp.concatenate([x1 * cos - x2 * sin, x2 * cos + x1 * sin], 1)

    def project(j):
        q = rotary(proj(j * RET_DK, RET_DK)).astype(BF16)
        k = rotary(proj(k0 + j * RET_DK, RET_DK)) * RET_DK ** -0.5
        v = proj(v0 + j * RET_DV, RET_DV).astype(BF16)
        g = proj(g0 + j * RET_DV, RET_DV)
        gate = g * jax.nn.sigmoid(g) * ng_ref[:, j * RET_DV:(j + 1) * RET_DV]
        return q, k, k.astype(BF16), v, gate

    ahead = project(0)
    for j in range(RET_HEADS):
        q, k, kb, v, gate = ahead
        if j + 1 < RET_HEADS:
            ahead = project(j + 1)
        cd = cd_ref[j]
        dmat = dmat_ref[j]
        qd = jnp.concatenate([qd_ref[j]] * (RET_DV // LANES), axis=1)
        kd = jnp.concatenate([kd_ref[j]] * (RET_DK // LANES), axis=1)
        state = state_ref[j]
        for c in range(ODD_TM // RET_BLOCK):
            rows = slice(c * RET_BLOCK, (c + 1) * RET_BLOCK)
            qc, kc, vc = q[rows], kb[rows], v[rows]
            sc = lax.dot_general(qc, kc, (((1,), (1,)), ((), ())),
                                 preferred_element_type=F32) * dmat
            y = jnp.dot(sc.astype(BF16), vc, preferred_element_type=F32)
            y = y + jnp.dot(qc, state.astype(BF16), preferred_element_type=F32) * qd
            kdec = (k[rows] * kd).astype(BF16)
            state = state * cd + lax.dot_general(kdec, vc, (((0,), (0,)), ((), ())),
                                                 preferred_element_type=F32)
            mu = jnp.mean(y, -1, keepdims=True)
            d = y - mu
            var = jnp.mean(d * d, -1, keepdims=True)
            yn = d * lax.rsqrt(var + LN_EPS)
            y_ref[rows, j * RET_DV:(j + 1) * RET_DV] = (gate[rows] * yn).astype(BF16)
        state_ref[j] = state

    rows = ODD_TM // OUT_SPLIT
    groups = [slice(r * rows, (r + 1) * rows) for r in range(OUT_SPLIT)]
    ys = [jnp.dot(y_ref[sl, :], wout_ref[...], preferred_element_type=F32) for sl in groups]
    for sl, y in zip(groups, ys):
        o_ref[sl, :] = _layer_norm(ALPHA * x_ref[sl, :] + y, g_ref[...], b_ref[...])


def _odd_layer_mix(x, w, cos, sin, norm_g, tables, w_out, g, b, batch, seq):
    cd, dmat, qd, kd = tables
    per_seq = seq // ODD_TM
    tile = lambda b, p: (b * per_seq + p, 0)
    tbl = pl.BlockSpec((ODD_TM, LANES), lambda b, p: (p, 0))
    return pl.pallas_call(
        _odd_mix_kernel,
        grid=(batch, per_seq),
        in_specs=[pl.BlockSpec(memory_space=pltpu.SMEM),
                  pl.BlockSpec((ODD_TM, D_MODEL), tile),
                  _resident((D_MODEL, ODD_IN)), tbl, tbl, _resident((1, RET_VW)),
                  _resident((RET_HEADS, RET_BLOCK, RET_BLOCK)),
                  _resident((RET_HEADS, RET_BLOCK, LANES)),
                  _resident((RET_HEADS, RET_BLOCK, LANES)),
                  _resident((RET_VW, D_MODEL)),
                  _resident((1, D_MODEL)), _resident((1, D_MODEL))],
        out_specs=pl.BlockSpec((ODD_TM, D_MODEL), tile),
        out_shape=jax.ShapeDtypeStruct((batch * seq, D_MODEL), F32),
        scratch_shapes=[pltpu.VMEM((RET_HEADS, RET_DK, RET_DV), F32),
                        pltpu.VMEM((ODD_TM, RET_VW), BF16)],
        compiler_params=_params(2, 56),
        name="odd_mix",
    )(cd, x, w, cos, sin, norm_g, dmat, qd, kd, w_out, g, b)


def _rope_angles(seq, n_rot, theta):
    inv = np.exp(-math.log(theta) * np.arange(n_rot // 2, dtype=np.float64) * (2.0 / n_rot))
    return np.arange(seq, dtype=np.float64)[:, None] * inv[None, :]


def _even_rope_tables(seq):
    half = ROPE_DIMS // 2
    ang = _rope_angles(seq, ROPE_DIMS, ROPE_THETA)
    cos, sin = np.cos(ang), np.sin(ang)
    pad = np.zeros((seq, DIFF_DH - ROPE_DIMS))
    zeros = np.zeros((seq, half))
    c = np.concatenate([cos, cos, pad + 1.0], -1)
    sa = np.concatenate([-sin, zeros, pad], -1)
    sb = np.concatenate([zeros, sin, pad], -1)
    reps = LANES // DIFF_DH
    return tuple(jnp.asarray(np.tile(t, (1, reps)), F32) for t in (c, sa, sb))


def _odd_rope_tables(seq):
    ang = _rope_angles(seq, RET_DK, RET_ROPE_THETA)
    return jnp.asarray(np.cos(ang), F32), jnp.asarray(np.sin(ang), F32)


def _retention_tables():
    log_g = np.log1p(-np.exp2(-5.0 - np.arange(RET_HEADS, dtype=np.float64)))
    idx = np.arange(RET_BLOCK, dtype=np.float64)
    rel = idx[:, None] - idx[None, :]
    dmat = np.where(rel >= 0, np.exp(log_g[:, None, None] * np.maximum(rel, 0.0)), 0.0)
    q_decay = np.exp(log_g[:, None] * (idx + 1.0))
    k_decay = np.exp(log_g[:, None] * (RET_BLOCK - 1.0 - idx))
    cd = np.exp(log_g * RET_BLOCK)
    rep = lambda t: np.broadcast_to(t[:, :, None], (RET_HEADS, RET_BLOCK, LANES))
    return tuple(jnp.asarray(t, F32) for t in (cd, dmat, rep(q_decay), rep(k_decay)))


def kernel(x, ln_g, ln_b, ffn_w_in, ffn_w_out, even_w_in, even_w_out, diff_lambda,
           diff_norm_g, conv_w, odd_w_in, odd_w_out, ret_norm_g):
    batch, seq, _ = x.shape
    assert seq % TM == 0 and seq % ATT_TQ == 0 and DEPTH == 2
    h = x.reshape(batch * seq, D_MODEL)
    row = lambda v: v.reshape(1, -1)
    norm = lambda i, k: (row(ln_g[i, k]), row(ln_b[i, k]))
    ffn = lambda i, k: [(ffn_w_in, (i, k)), (ffn_w_out, (i, k))]


    h, (e_in, e_out) = _ffn_ln(h, ffn_w_in, ffn_w_out, *norm(0, 0), stream=(0, 0),
                               casts=[(even_w_in, (0,)), (even_w_out, (0,))])
    lambda_init = 0.8 - 0.6 * math.exp(-0.3 * 0)
    hcat, (w_in, w_out, *next_ffn) = _even_proj(
        h, e_in, *_even_rope_tables(seq), conv_w[0], seq, casts=ffn(0, 1) + ffn(1, 0))
    attn, _ = _diff_attention(hcat, diff_lambda[0], row(diff_norm_g[0]), batch, seq,
                              lambda_init)
    h = _even_out(h, attn, hcat, e_out, *norm(0, 1))
    h, (o_in, o_out) = _ffn_ln(h, w_in, w_out, *norm(0, 2),
                               casts=[(odd_w_in, (0,)), (odd_w_out, (0,))])

    h, last_ffn = _ffn_ln(h, *next_ffn, *norm(1, 0), casts=ffn(1, 1))
    h = _odd_layer_mix(h, o_in, *_odd_rope_tables(seq), row(ret_norm_g[0]),
                       _retention_tables(), o_out, *norm(1, 1), batch, seq)
    h, _ = _ffn_ln(h, *last_ffn, *norm(1, 2))
    return h.reshape(batch, seq, D_MODEL)
```

```python
import functools
import math

import jax
import jax.numpy as jnp
import numpy as np
from jax import lax
from jax.experimental import pallas as pl
from jax.experimental.pallas import tpu as pltpu

F32 = jnp.float32
BF16 = jnp.bfloat16

D_MODEL = 1024
DEPTH = 2
D_FF = 2816
DIFF_HEADS = 4
DIFF_DH = 64
DIFF_DV = 128
DIFF_QK = 512
DIFF_WIDTH = 512
ROPE_THETA = 500000.0
ROPE_DIMS = 16
CONV_CH = 512
EVEN_IN = 3072
RET_HEADS = 4
RET_DK = 256
RET_DV = 512
RET_QK = 1024
RET_VW = 2048
RET_ROPE_THETA = 10000.0
ODD_IN = 6144
ALPHA = (2.0 * DEPTH) ** 0.25
LN_EPS = 1e-5
NEG_INF = -1e30
LOG2E = math.log2(math.e)

LANES = 128
MIB = 1024 * 1024
VMEM_LIMIT = 56 * MIB

TM = 1024
FF_CHUNK = 256
OUT_SPLIT = 4
FFN_ROW_GROUPS = (256, 256, 256, 256)
ODD_TM = 1024
RET_BLOCK = 256
ATT_TQ = 256
ATT_STRIP = 16


def _layer_norm(z, g, b):
    mu = jnp.mean(z, -1, keepdims=True)
    d = z - mu
    var = jnp.mean(d * d, -1, keepdims=True)
    return d * lax.rsqrt(var + LN_EPS) * g + b


def _params(n_axes, vmem_mib):
    assert vmem_mib * MIB <= VMEM_LIMIT
    return pltpu.CompilerParams(dimension_semantics=("arbitrary",) * n_axes,
                                vmem_limit_bytes=vmem_mib * MIB)


def _resident(shape):
    nd = len(shape)
    return pl.BlockSpec(shape, lambda *_: (0,) * nd, pipeline_mode=pl.Buffered(1))


def _cast_specs(casts, grid):
    steps = math.prod(grid)

    def step(*idx):
        lin = idx[0]
        for size, i in zip(grid[1:], idx[1:]):
            lin = lin * size + i
        return lin

    in_specs, out_specs, out_shapes = [], [], []
    for arr, prefix in casts:
        r, c = arr.shape[-2:]
        assert arr.ndim == len(prefix) + 2
        rows = min(d for d in range(16, r + 1, 16) if r % d == 0 and d * steps >= r)
        last = r // rows - 1
        in_specs.append(pl.BlockSpec(
            (None,) * len(prefix) + (rows, c),
            lambda *idx, p=prefix, n=last: p + (jnp.minimum(step(*idx), n), 0)))
        out_specs.append(pl.BlockSpec(
            (rows, c), lambda *idx, n=last: (jnp.minimum(step(*idx), n), 0)))
        out_shapes.append(jax.ShapeDtypeStruct((r, c), BF16))
    return in_specs, out_specs, out_shapes


def _cast_vmem_mib(casts, grid):
    in_specs, _, _ = _cast_specs(casts, grid)
    elems = sum(math.prod(d for d in spec.block_shape if d is not None) for spec in in_specs)
    return -(-elems * (4 + 2) * 2 // MIB)


def _do_casts(src_refs, dst_refs):
    for src, dst in zip(src_refs, dst_refs):
        dst[...] = src[...].astype(BF16)


W_STREAM_STEPS = 16
W_STREAM_SLOTS = 3


def _stream_cast(pick, sem_ref, streams):
    def copy(s, c):
        src, _, stage = streams[s]
        slots, rows = stage.shape[:2]
        return pltpu.make_async_copy(src.at[pick[0], pick[1], pl.ds(c * rows, rows), :],
                                     stage.at[c % slots], sem_ref.at[s, c % slots])

    for s, (_, _, stage) in enumerate(streams):
        for c in range(stage.shape[0] - 1):
            copy(s, c).start()
    for c in range(W_STREAM_STEPS):
        for s, (_, dst, stage) in enumerate(streams):
            slots, rows = stage.shape[:2]
            if c + slots - 1 < W_STREAM_STEPS:
                copy(s, c + slots - 1).start()
            copy(s, c).wait()
            dst[c * rows:(c + 1) * rows, :] = stage[c % slots].astype(BF16)


def _ffn_kernel(x_ref, win_ref, wout_ref, g_ref, b_ref, *rest, n_cast, stream):
    cast_src, o_ref, cast_dst = rest[:n_cast], rest[n_cast], rest[n_cast + 1:2 * n_cast + 1]
    act_ref = rest[2 * n_cast + 1]
    if stream is not None:
        win_bf, wout_bf, stage_in, stage_out, sems = rest[2 * n_cast + 2:]

        @pl.when(pl.program_id(0) == 0)
        def _():
            _stream_cast(stream, sems, [(win_ref, win_bf, stage_in),
                                        (wout_ref, wout_bf, stage_out)])

        win_ref, wout_ref = win_bf, wout_bf
    _do_casts(cast_src, cast_dst)
    x = x_ref[...]
    xb = x.astype(BF16)
    for c in range(D_FF // FF_CHUNK):
        lo = c * FF_CHUNK
        gate = jnp.dot(xb, win_ref[:, lo:lo + FF_CHUNK], preferred_element_type=F32)
        up = jnp.dot(xb, win_ref[:, D_FF + lo:D_FF + lo + FF_CHUNK], preferred_element_type=F32)
        act_ref[:, lo:lo + FF_CHUNK] = (gate * jax.nn.sigmoid(gate) * up).astype(BF16)
    bounds = [sum(FFN_ROW_GROUPS[:r]) for r in range(len(FFN_ROW_GROUPS) + 1)]
    groups = [slice(lo, hi) for lo, hi in zip(bounds[:-1], bounds[1:])]
    ys = [jnp.dot(act_ref[sl, :], wout_ref[...], preferred_element_type=F32) for sl in groups]
    for sl, y in zip(groups, ys):
        o_ref[sl, :] = _layer_norm(ALPHA * x_ref[sl, :] + 0.5 * y, g_ref[...], b_ref[...])


def _ffn_ln(x, w_in, w_out, g, b, casts=(), stream=None):
    t = x.shape[0]
    steps = t // TM
    cast_in, cast_out, cast_shapes = _cast_specs(casts, (steps,))
    scratch = [pltpu.VMEM((TM, D_FF), BF16)]
    if stream is None:
        weights = [_resident((D_MODEL, 2 * D_FF)), _resident((D_FF, D_MODEL))]
        stage_mib = 0
    else:
        weights = [pl.BlockSpec(memory_space=pl.ANY)] * 2
        rows_in, rows_out = D_MODEL // W_STREAM_STEPS, D_FF // W_STREAM_STEPS
        scratch += [pltpu.VMEM((D_MODEL, 2 * D_FF), BF16), pltpu.VMEM((D_FF, D_MODEL), BF16),
                    pltpu.VMEM((W_STREAM_SLOTS, rows_in, 2 * D_FF), F32),
                    pltpu.VMEM((W_STREAM_SLOTS, rows_out, D_MODEL), F32),
                    pltpu.SemaphoreType.DMA((2, W_STREAM_SLOTS))]
        stage_mib = -(-W_STREAM_SLOTS * 4 * (rows_in * 2 * D_FF + rows_out * D_MODEL) // MIB)
    out = pl.pallas_call(
        functools.partial(_ffn_kernel, n_cast=len(casts), stream=stream),
        grid=(steps,),
        in_specs=[pl.BlockSpec((TM, D_MODEL), lambda i: (i, 0))] + weights
                 + [_resident((1, D_MODEL)), _resident((1, D_MODEL))] + cast_in,
        out_specs=[pl.BlockSpec((TM, D_MODEL), lambda i: (i, 0))] + cast_out,
        out_shape=[jax.ShapeDtypeStruct((t, D_MODEL), F32)] + cast_shapes,
        scratch_shapes=scratch,
        compiler_params=_params(1, 45 + stage_mib + _cast_vmem_mib(casts, (steps,))),
        name="ffn_ln",
    )(x, w_in, w_out, g, b, *[arr for arr, _ in casts])
    return out[0], out[1:]


EVEN_OUT_COLS = 2 * DIFF_QK + DIFF_WIDTH + CONV_CH
CONV_TAIL = 8


def _even_proj_kernel(x_ref, w_ref, c_ref, sa_ref, sb_ref, cw_ref, *rest, n_cast, per_seq):
    cast_src, o_ref, cast_dst, tail_ref = (rest[:n_cast], rest[n_cast],
                                           rest[n_cast + 1:2 * n_cast + 1], rest[-1])
    _do_casts(cast_src, cast_dst)
    i = pl.program_id(0)

    @pl.when(i == 0)
    def _():
        tail_ref[...] = jnp.zeros_like(tail_ref)

    xb = x_ref[...].astype(BF16)
    c = c_ref[...]
    sa = sa_ref[...]
    sb = sb_ref[...]
    proj = lambda n: jnp.dot(xb, w_ref[:, n * 512:(n + 1) * 512], preferred_element_type=F32)
    for n in range(2):
        r = proj(n)
        scale = DIFF_DH ** -0.5 * LOG2E if n == 0 else 1.0
        for j in range(512 // LANES):
            blk = r[:, j * LANES:(j + 1) * LANES]
            rot = (blk * c + pltpu.roll(blk, LANES - 8, 1) * sa
                   + pltpu.roll(blk, 8, 1) * sb)
            if n == 0:
                rot = rot * scale
            o_ref[:, n * 512 + j * LANES:n * 512 + (j + 1) * LANES] = rot.astype(BF16)
    u = proj(4) * proj(5)
    prev = jnp.where(i % per_seq == 0, jnp.zeros_like(u[:CONV_TAIL]), tail_ref[...])
    tail_ref[...] = u[TM - CONV_TAIL:]
    row = lax.broadcasted_iota(jnp.int32, (CONV_TAIL, CONV_CH), 0)
    u1 = pltpu.roll(u, 1, 0)
    u2 = pltpu.roll(u, 2, 0)
    head1 = jnp.where(row == 0, prev[CONV_TAIL - 1:], u1[:CONV_TAIL])
    head2 = jnp.where(row == 0, prev[CONV_TAIL - 2:CONV_TAIL - 1], u2[:CONV_TAIL])
    head2 = jnp.where(row == 1, prev[CONV_TAIL - 1:], head2)
    u1 = jnp.concatenate([head1, u1[CONV_TAIL:]], 0)
    u2 = jnp.concatenate([head2, u2[CONV_TAIL:]], 0)
    cw = cw_ref[...]
    conv = proj(3) * (cw[0:1] * u2 + cw[1:2] * u1 + cw[2:3] * u)
    o_ref[:, 3 * 512:] = conv.astype(BF16)
    o_ref[:, 2 * 512:3 * 512] = proj(2).astype(BF16)


def _even_proj(x, w, c, sa, sb, conv_w, seq, casts=()):
    t = x.shape[0]
    steps = t // TM
    per_seq = seq // TM
    tbl = pl.BlockSpec((TM, LANES), lambda i: (i % per_seq, 0))
    cast_in, cast_out, cast_shapes = _cast_specs(casts, (steps,))
    out = pl.pallas_call(
        functools.partial(_even_proj_kernel, n_cast=len(casts), per_seq=per_seq),
        grid=(steps,),
        in_specs=[pl.BlockSpec((TM, D_MODEL), lambda i: (i, 0)),
                  _resident((D_MODEL, EVEN_IN)), tbl, tbl, tbl,
                  _resident((3, CONV_CH))] + cast_in,
        out_specs=[pl.BlockSpec((TM, EVEN_OUT_COLS), lambda i: (i, 0))] + cast_out,
        out_shape=[jax.ShapeDtypeStruct((t, EVEN_OUT_COLS), BF16)] + cast_shapes,
        scratch_shapes=[pltpu.VMEM((CONV_TAIL, CONV_CH), F32)],
        compiler_params=_params(1, 28 + _cast_vmem_mib(casts, (steps,))),
        name="even_proj",
    )(x, w, c, sa, sb, conv_w, *[arr for arr, _ in casts])
    return out[0], out[1:]


def _attn_kernel(q_ref, k_ref, v_ref, lv_ref, g_ref, o_ref, s_ref, p_ref, *, lambda_init, seq):
    lv = lv_ref[...]
    lam = (jnp.exp(jnp.sum(lv[0:1] * lv[1:2], -1, keepdims=True))
           - jnp.exp(jnp.sum(lv[2:3] * lv[3:4], -1, keepdims=True)) + lambda_init)
    lane = lax.broadcasted_iota(jnp.int32, (ATT_TQ, LANES), 1)
    row = lax.broadcasted_iota(jnp.int32, (ATT_TQ, ATT_TQ), 0)
    col = lax.broadcasted_iota(jnp.int32, (ATT_TQ, ATT_TQ), 1)
    causal = col <= row
    gain = g_ref[...]
    nt = (((1,), (1,)), ((), ()))
    n_tiles = seq // ATT_TQ
    order = [t for pair in zip(range(n_tiles - 1, -1, -1), range(n_tiles)) for t in pair][:n_tiles]
    for pos, i in enumerate(order):
        buf = pos % 2
        lo = i * ATT_TQ
        q = q_ref[lo:lo + ATT_TQ, :]
        zero = jnp.zeros_like(q)
        qs = (jnp.where(lane < DIFF_DH, q, zero), jnp.where(lane >= DIFF_DH, q, zero))
        hi = lo + ATT_TQ
        l = []
        for c, qc in enumerate(qs):
            s = lax.dot_general(qc, k_ref[:hi, :], nt, preferred_element_type=F32)
            sc_ref = s_ref.at[buf, c]
            if i > 0:
                sc_ref[:, :lo] = s[:, :lo]
            sc_ref[:, lo:hi] = jnp.where(causal, s[:, lo:], NEG_INF)
            m = jnp.max(sc_ref[:, :hi], -1, keepdims=True)
            lsum = []
            for r in range(ATT_TQ // ATT_STRIP):
                sl = slice(r * ATT_STRIP, (r + 1) * ATT_STRIP)
                p = jnp.exp2(sc_ref[sl, :hi] - m[sl])
                lsum.append(jnp.sum(p, -1, keepdims=True))
                base = c * ATT_TQ + r * ATT_STRIP
                p_ref[buf, base:base + ATT_STRIP, :hi] = p.astype(BF16)
            l.append(jnp.concatenate(lsum, 0))
        acc = jnp.dot(p_ref[buf, :, :hi], v_ref[:hi, :], preferred_element_type=F32)
        acc = acc[:ATT_TQ] * (1.0 / l[0]) - acc[ATT_TQ:] * (lam / l[1])
        y = acc * lax.rsqrt(jnp.mean(acc * acc, -1, keepdims=True) + LN_EPS)
        o_ref[lo:lo + ATT_TQ, :] = (y * gain * (1.0 - lambda_init)).astype(BF16)


def _diff_attention(hcat, lam_vecs, norm_g, batch, seq, lambda_init):
    return pl.pallas_call(
        functools.partial(_attn_kernel, lambda_init=lambda_init, seq=seq),
        grid=(batch, DIFF_HEADS),
        in_specs=[pl.BlockSpec((seq, LANES), lambda b, h: (b, h)),
                  pl.BlockSpec((seq, LANES), lambda b, h: (b, DIFF_QK // LANES + h)),
                  pl.BlockSpec((seq, LANES), lambda b, h: (b, 2 * DIFF_QK // LANES + h)),
                  _resident((4, DIFF_DH)),
                  _resident((1, DIFF_DV))],
        out_specs=pl.BlockSpec((seq, DIFF_DV), lambda b, h: (b, h)),
        out_shape=jax.ShapeDtypeStruct((batch * seq, DIFF_WIDTH), BF16),
        scratch_shapes=[pltpu.VMEM((2, 2, ATT_TQ, seq), F32),
                        pltpu.VMEM((2, 2 * ATT_TQ, seq), BF16)],
        compiler_params=_params(2, 24),
        name="diff_attn",
    )(hcat, hcat, hcat, lam_vecs, norm_g)


def _even_out_kernel(x_ref, attn_ref, conv_ref, wout_ref, g_ref, b_ref, o_ref):
    rows = TM // OUT_SPLIT
    groups = [slice(r * rows, (r + 1) * rows) for r in range(OUT_SPLIT)]
    ys = []
    for sl in groups:
        mix = jnp.concatenate([attn_ref[sl, :], conv_ref[sl, :]], 1)
        ys.append(jnp.dot(mix, wout_ref[...], preferred_element_type=F32))
    for sl, y in zip(groups, ys):
        o_ref[sl, :] = _layer_norm(ALPHA * x_ref[sl, :] + y, g_ref[...], b_ref[...])


def _even_out(x, attn, hcat, w_out, g, b):
    t = x.shape[0]
    tm = TM
    conv_col = (2 * DIFF_QK + DIFF_WIDTH) // CONV_CH
    return pl.pallas_call(
        _even_out_kernel,
        grid=(t // tm,),
        in_specs=[pl.BlockSpec((tm, D_MODEL), lambda i: (i, 0)),
                  pl.BlockSpec((tm, DIFF_WIDTH), lambda i: (i, 0)),
                  pl.BlockSpec((tm, CONV_CH), lambda i: (i, conv_col)),
                  _resident((DIFF_WIDTH + CONV_CH, D_MODEL)),
                  _resident((1, D_MODEL)), _resident((1, D_MODEL))],
        out_specs=pl.BlockSpec((tm, D_MODEL), lambda i: (i, 0)),
        out_shape=jax.ShapeDtypeStruct((t, D_MODEL), F32),
        compiler_params=_params(1, 24),
        name="even_out",
    )(x, attn, hcat, w_out, g, b)


def _odd_mix_kernel(cd_ref, x_ref, w_ref, cos_ref, sin_ref, ng_ref, dmat_ref, qd_ref, kd_ref,
                    wout_ref, g_ref, b_ref, o_ref, state_ref, y_ref):
    @pl.when(pl.program_id(1) == 0)
    def _():
        state_ref[...] = jnp.zeros_like(state_ref)

    xb = x_ref[...].astype(BF16)
    cos = cos_ref[...]
    sin = sin_ref[...]
    half = RET_DK // 2
    k0, v0, g0 = RET_QK, 2 * RET_QK, 2 * RET_QK + RET_VW

    def proj(lo, n):
        return jnp.dot(xb, w_ref[:, lo:lo + n], preferred_element_type=F32)

    def rotary(r):
        x1, x2 = r[:, :half], r[:, half:]
        return jnp.concatenate([x1 * cos - x2 * sin, x2 * cos + x1 * sin], 1)

    def project(j):
        q = rotary(proj(j * RET_DK, RET_DK)).astype(BF16)
        k = rotary(proj(k0 + j * RET_DK, RET_DK)) * RET_DK ** -0.5
        v = proj(v0 + j * RET_DV, RET_DV).astype(BF16)
        g = proj(g0 + j * RET_DV, RET_DV)
        gate = g * jax.nn.sigmoid(g) * ng_ref[:, j * RET_DV:(j + 1) * RET_DV]
        return q, k, k.astype(BF16), v, gate

    ahead = project(0)
    for j in range(RET_HEADS):
        q, k, kb, v, gate = ahead
        if j + 1 < RET_HEADS:
            ahead = project(j + 1)
        cd = cd_ref[j]
        dmat = dmat_ref[j]
        qd = jnp.concatenate([qd_ref[j]] * (RET_DV // LANES), axis=1)
        kd = jnp.concatenate([kd_ref[j]] * (RET_DK // LANES), axis=1)
        state = state_ref[j]
        for c in range(ODD_TM // RET_BLOCK):
            rows = slice(c * RET_BLOCK, (c + 1) * RET_BLOCK)
            qc, kc, vc = q[rows], kb[rows], v[rows]
            sc = lax.dot_general(qc, kc, (((1,), (1,)), ((), ())),
                                 preferred_element_type=F32) * dmat
            y = jnp.dot(sc.astype(BF16), vc, preferred_element_type=F32)
            y = y + jnp.dot(qc, state.astype(BF16), preferred_element_type=F32) * qd
            kdec = (k[rows] * kd).astype(BF16)
            state = state * cd + lax.dot_general(kdec, vc, (((0,), (0,)), ((), ())),
                                                 preferred_element_type=F32)
            mu = jnp.mean(y, -1, keepdims=True)
            d = y - mu
            var = jnp.mean(d * d, -1, keepdims=True)
            yn = d * lax.rsqrt(var + LN_EPS)
            y_ref[rows, j * RET_DV:(j + 1) * RET_DV] = (gate[rows] * yn).astype(BF16)
        state_ref[j] = state

    rows = ODD_TM // OUT_SPLIT
    groups = [slice(r * rows, (r + 1) * rows) for r in range(OUT_SPLIT)]
    ys = [jnp.dot(y_ref[sl, :], wout_ref[...], preferred_element_type=F32) for sl in groups]
    for sl, y in zip(groups, ys):
        o_ref[sl, :] = _layer_norm(ALPHA * x_ref[sl, :] + y, g_ref[...], b_ref[...])


def _odd_layer_mix(x, w, cos, sin, norm_g, tables, w_out, g, b, batch, seq):
    cd, dmat, qd, kd = tables
    per_seq = seq // ODD_TM
    tile = lambda b, p: (b * per_seq + p, 0)
    tbl = pl.BlockSpec((ODD_TM, LANES), lambda b, p: (p, 0))
    return pl.pallas_call(
        _odd_mix_kernel,
        grid=(batch, per_seq),
        in_specs=[pl.BlockSpec(memory_space=pltpu.SMEM),
                  pl.BlockSpec((ODD_TM, D_MODEL), tile),
                  _resident((D_MODEL, ODD_IN)), tbl, tbl, _resident((1, RET_VW)),
                  _resident((RET_HEADS, RET_BLOCK, RET_BLOCK)),
                  _resident((RET_HEADS, RET_BLOCK, LANES)),
                  _resident((RET_HEADS, RET_BLOCK, LANES)),
                  _resident((RET_VW, D_MODEL)),
                  _resident((1, D_MODEL)), _resident((1, D_MODEL))],
        out_specs=pl.BlockSpec((ODD_TM, D_MODEL), tile),
        out_shape=jax.ShapeDtypeStruct((batch * seq, D_MODEL), F32),
        scratch_shapes=[pltpu.VMEM((RET_HEADS, RET_DK, RET_DV), F32),
                        pltpu.VMEM((ODD_TM, RET_VW), BF16)],
        compiler_params=_params(2, 56),
        name="odd_mix",
    )(cd, x, w, cos, sin, norm_g, dmat, qd, kd, w_out, g, b)


def _rope_angles(seq, n_rot, theta):
    inv = np.exp(-math.log(theta) * np.arange(n_rot // 2, dtype=np.float64) * (2.0 / n_rot))
    return np.arange(seq, dtype=np.float64)[:, None] * inv[None, :]


def _even_rope_tables(seq):
    half = ROPE_DIMS // 2
    ang = _rope_angles(seq, ROPE_DIMS, ROPE_THETA)
    cos, sin = np.cos(ang), np.sin(ang)
    pad = np.zeros((seq, DIFF_DH - ROPE_DIMS))
    zeros = np.zeros((seq, half))
    c = np.concatenate([cos, cos, pad + 1.0], -1)
    sa = np.concatenate([-sin, zeros, pad], -1)
    sb = np.concatenate([zeros, sin, pad], -1)
    reps = LANES // DIFF_DH
    return tuple(jnp.asarray(np.tile(t, (1, reps)), F32) for t in (c, sa, sb))


def _odd_rope_tables(seq):
    ang = _rope_angles(seq, RET_DK, RET_ROPE_THETA)
    return jnp.asarray(np.cos(ang), F32), jnp.asarray(np.sin(ang), F32)


def _retention_tables():
    log_g = np.log1p(-np.exp2(-5.0 - np.arange(RET_HEADS, dtype=np.float64)))
    idx = np.arange(RET_BLOCK, dtype=np.float64)
    rel = idx[:, None] - idx[None, :]
    dmat = np.where(rel >= 0, np.exp(log_g[:, None, None] * np.maximum(rel, 0.0)), 0.0)
    q_decay = np.exp(log_g[:, None] * (idx + 1.0))
    k_decay = np.exp(log_g[:, None] * (RET_BLOCK - 1.0 - idx))
    cd = np.exp(log_g * RET_BLOCK)
    rep = lambda t: np.broadcast_to(t[:, :, None], (RET_HEADS, RET_BLOCK, LANES))
    return tuple(jnp.asarray(t, F32) for t in (cd, dmat, rep(q_decay), rep(k_decay)))


def kernel(x, ln_g, ln_b, ffn_w_in, ffn_w_out, even_w_in, even_w_out, diff_lambda,
           diff_norm_g, conv_w, odd_w_in, odd_w_out, ret_norm_g):
    batch, seq, _ = x.shape
    assert seq % TM == 0 and seq % ODD_TM == 0 and seq % ATT_TQ == 0 and DEPTH == 2
    h = x.reshape(batch * seq, D_MODEL)
    row = lambda v: v.reshape(1, -1)
    norm = lambda i, k: (row(ln_g[i, k]), row(ln_b[i, k]))
    ffn = lambda i, k: [(ffn_w_in, (i, k)), (ffn_w_out, (i, k))]


    h, (e_in, e_out) = _ffn_ln(h, ffn_w_in, ffn_w_out, *norm(0, 0), stream=(0, 0),
                               casts=[(even_w_in, (0,)), (even_w_out, (0,))])
    lambda_init = 0.8 - 0.6 * math.exp(-0.3 * 0)
    hcat, (w_in, w_out, *next_ffn) = _even_proj(
        h, e_in, *_even_rope_tables(seq), conv_w[0], seq, casts=ffn(0, 1) + ffn(1, 0))
    attn = _diff_attention(hcat, diff_lambda[0], row(diff_norm_g[0]), batch, seq, lambda_init)
    h = _even_out(h, attn, hcat, e_out, *norm(0, 1))
    h, (o_in, o_out) = _ffn_ln(h, w_in, w_out, *norm(0, 2),
                               casts=[(odd_w_in, (0,)), (odd_w_out, (0,))])

    h, last_ffn = _ffn_ln(h, *next_ffn, *norm(1, 0), casts=ffn(1, 1))
    h = _odd_layer_mix(h, o_in, *_odd_rope_tables(seq), row(ret_norm_g[0]),
                       _retention_tables(), o_out, *norm(1, 1), batch, seq)
    h, _ = _ffn_ln(h, *last_ffn, *norm(1, 2))
    return h.reshape(batch, seq, D_MODEL)
```

```python
import functools
import math

import jax
import jax.numpy as jnp
import numpy as np
from jax import lax
from jax.experimental import pallas as pl
from jax.experimental.pallas import tpu as pltpu

F32 = jnp.float32
BF16 = jnp.bfloat16

D_MODEL = 1024
DEPTH = 2
D_FF = 2816
DIFF_HEADS = 4
DIFF_DH = 64
DIFF_DV = 128
DIFF_QK = 512
DIFF_WIDTH = 512
ROPE_THETA = 500000.0
ROPE_DIMS = 16
CONV_CH = 512
EVEN_IN = 3072
RET_HEADS = 4
RET_DK = 256
RET_DV = 512
RET_QK = 1024
RET_VW = 2048
RET_ROPE_THETA = 10000.0
ODD_IN = 6144
ALPHA = (2.0 * DEPTH) ** 0.25
LN_EPS = 1e-5
NEG_INF = -1e30
LOG2E = math.log2(math.e)

LANES = 128
MIB = 1024 * 1024
VMEM_LIMIT = 56 * MIB

TM = 1024
FF_CHUNK = 256
OUT_SPLIT = 4
FFN_ROW_GROUPS = (256, 256, 256, 256)
ODD_TM = 1024
RET_BLOCK = 256
ATT_TQ = 256
ATT_STRIP = 16


def _layer_norm(z, g, b):
    mu = jnp.mean(z, -1, keepdims=True)
    d = z - mu
    var = jnp.mean(d * d, -1, keepdims=True)
    return d * lax.rsqrt(var + LN_EPS) * g + b


def _params(n_axes, vmem_mib):
    assert vmem_mib * MIB <= VMEM_LIMIT
    return pltpu.CompilerParams(dimension_semantics=("arbitrary",) * n_axes,
                                vmem_limit_bytes=vmem_mib * MIB)


def _resident(shape):
    nd = len(shape)
    return pl.BlockSpec(shape, lambda *_: (0,) * nd, pipeline_mode=pl.Buffered(1))


def _cast_specs(casts, grid):
    steps = math.prod(grid)

    def step(*idx):
        lin = idx[0]
        for size, i in zip(grid[1:], idx[1:]):
            lin = lin * size + i
        return lin

    in_specs, out_specs, out_shapes = [], [], []
    for arr, prefix in casts:
        r, c = arr.shape[-2:]
        assert arr.ndim == len(prefix) + 2
        rows = min(d for d in range(16, r + 1, 16) if r % d == 0 and d * steps >= r)
        last = r // rows - 1
        in_specs.append(pl.BlockSpec(
            (None,) * len(prefix) + (rows, c),
            lambda *idx, p=prefix, n=last: p + (jnp.minimum(step(*idx), n), 0)))
        out_specs.append(pl.BlockSpec(
            (rows, c), lambda *idx, n=last: (jnp.minimum(step(*idx), n), 0)))
        out_shapes.append(jax.ShapeDtypeStruct((r, c), BF16))
    return in_specs, out_specs, out_shapes


def _cast_vmem_mib(casts, grid):
    in_specs, _, _ = _cast_specs(casts, grid)
    elems = sum(math.prod(d for d in spec.block_shape if d is not None) for spec in in_specs)
    return -(-elems * (4 + 2) * 2 // MIB)


def _do_casts(src_refs, dst_refs):
    for src, dst in zip(src_refs, dst_refs):
        dst[...] = src[...].astype(BF16)


W_STREAM_STEPS = 16
W_STREAM_SLOTS = 3


def _stream_cast(pick, sem_ref, streams):
    def copy(s, c):
        src, _, stage = streams[s]
        slots, rows = stage.shape[:2]
        return pltpu.make_async_copy(src.at[pick[0], pick[1], pl.ds(c * rows, rows), :],
                                     stage.at[c % slots], sem_ref.at[s, c % slots])

    for s, (_, _, stage) in enumerate(streams):
        for c in range(stage.shape[0] - 1):
            copy(s, c).start()
    for c in range(W_STREAM_STEPS):
        for s, (_, dst, stage) in enumerate(streams):
            slots, rows = stage.shape[:2]
            if c + slots - 1 < W_STREAM_STEPS:
                copy(s, c + slots - 1).start()
            copy(s, c).wait()
            dst[c * rows:(c + 1) * rows, :] = stage[c % slots].astype(BF16)


def _ffn_kernel(x_ref, win_ref, wout_ref, g_ref, b_ref, *rest, n_cast, stream):
    cast_src, o_ref, cast_dst = rest[:n_cast], rest[n_cast], rest[n_cast + 1:2 * n_cast + 1]
    act_ref = rest[2 * n_cast + 1]
    if stream is not None:
        win_bf, wout_bf, stage_in, stage_out, sems = rest[2 * n_cast + 2:]

        @pl.when(pl.program_id(0) == 0)
        def _():
            _stream_cast(stream, sems, [(win_ref, win_bf, stage_in),
                                        (wout_ref, wout_bf, stage_out)])

        win_ref, wout_ref = win_bf, wout_bf
    _do_casts(cast_src, cast_dst)
    x = x_ref[...]
    xb = x.astype(BF16)
    for c in range(D_FF // FF_CHUNK):
        lo = c * FF_CHUNK
        gate = jnp.dot(xb, win_ref[:, lo:lo + FF_CHUNK], preferred_element_type=F32)
        up = jnp.dot(xb, win_ref[:, D_FF + lo:D_FF + lo + FF_CHUNK], preferred_element_type=F32)
        act_ref[:, lo:lo + FF_CHUNK] = (gate * jax.nn.sigmoid(gate) * up).astype(BF16)
    bounds = [sum(FFN_ROW_GROUPS[:r]) for r in range(len(FFN_ROW_GROUPS) + 1)]
    groups = [slice(lo, hi) for lo, hi in zip(bounds[:-1], bounds[1:])]
    ys = [jnp.dot(act_ref[sl, :], wout_ref[...], preferred_element_type=F32) for sl in groups]
    for sl, y in zip(groups, ys):
        o_ref[sl, :] = _layer_norm(ALPHA * x_ref[sl, :] + 0.5 * y, g_ref[...], b_ref[...])


def _ffn_ln(x, w_in, w_out, g, b, casts=(), stream=None):
    t = x.shape[0]
    steps = t // TM
    cast_in, cast_out, cast_shapes = _cast_specs(casts, (steps,))
    scratch = [pltpu.VMEM((TM, D_FF), BF16)]
    if stream is None:
        weights = [_resident((D_MODEL, 2 * D_FF)), _resident((D_FF, D_MODEL))]
        stage_mib = 0
    else:
        weights = [pl.BlockSpec(memory_space=pl.ANY)] * 2
        rows_in, rows_out = D_MODEL // W_STREAM_STEPS, D_FF // W_STREAM_STEPS
        scratch += [pltpu.VMEM((D_MODEL, 2 * D_FF), BF16), pltpu.VMEM((D_FF, D_MODEL), BF16),
                    pltpu.VMEM((W_STREAM_SLOTS, rows_in, 2 * D_FF), F32),
                    pltpu.VMEM((W_STREAM_SLOTS, rows_out, D_MODEL), F32),
                    pltpu.SemaphoreType.DMA((2, W_STREAM_SLOTS))]
        stage_mib = -(-W_STREAM_SLOTS * 4 * (rows_in * 2 * D_FF + rows_out * D_MODEL) // MIB)
    out = pl.pallas_call(
        functools.partial(_ffn_kernel, n_cast=len(casts), stream=stream),
        grid=(steps,),
        in_specs=[pl.BlockSpec((TM, D_MODEL), lambda i: (i, 0))] + weights
                 + [_resident((1, D_MODEL)), _resident((1, D_MODEL))] + cast_in,
        out_specs=[pl.BlockSpec((TM, D_MODEL), lambda i: (i, 0))] + cast_out,
        out_shape=[jax.ShapeDtypeStruct((t, D_MODEL), F32)] + cast_shapes,
        scratch_shapes=scratch,
        compiler_params=_params(1, 45 + stage_mib + _cast_vmem_mib(casts, (steps,))),
        name="ffn_ln",
    )(x, w_in, w_out, g, b, *[arr for arr, _ in casts])
    return out[0], out[1:]


EVEN_OUT_COLS = 2 * DIFF_QK + DIFF_WIDTH + CONV_CH
CONV_TAIL = 8


def _even_proj_kernel(x_ref, w_ref, c_ref, sa_ref, sb_ref, cw_ref, *rest, n_cast, per_seq):
    cast_src, o_ref, cast_dst, tail_ref = (rest[:n_cast], rest[n_cast],
                                           rest[n_cast + 1:2 * n_cast + 1], rest[-1])
    _do_casts(cast_src, cast_dst)
    i = pl.program_id(0)

    @pl.when(i == 0)
    def _():
        tail_ref[...] = jnp.zeros_like(tail_ref)

    xb = x_ref[...].astype(BF16)
    c = c_ref[...]
    sa = sa_ref[...]
    sb = sb_ref[...]
    proj = lambda n: jnp.dot(xb, w_ref[:, n * 512:(n + 1) * 512], preferred_element_type=F32)
    for n in range(2):
        r = proj(n)
        scale = DIFF_DH ** -0.5 * LOG2E if n == 0 else 1.0
        for j in range(512 // LANES):
            blk = r[:, j * LANES:(j + 1) * LANES]
            rot = (blk * c + pltpu.roll(blk, LANES - 8, 1) * sa
                   + pltpu.roll(blk, 8, 1) * sb)
            if n == 0:
                rot = rot * scale
            o_ref[:, n * 512 + j * LANES:n * 512 + (j + 1) * LANES] = rot.astype(BF16)
    u = proj(4) * proj(5)
    prev = jnp.where(i % per_seq == 0, jnp.zeros_like(u[:CONV_TAIL]), tail_ref[...])
    tail_ref[...] = u[TM - CONV_TAIL:]
    row = lax.broadcasted_iota(jnp.int32, (CONV_TAIL, CONV_CH), 0)
    u1 = pltpu.roll(u, 1, 0)
    u2 = pltpu.roll(u, 2, 0)
    head1 = jnp.where(row == 0, prev[CONV_TAIL - 1:], u1[:CONV_TAIL])
    head2 = jnp.where(row == 0, prev[CONV_TAIL - 2:CONV_TAIL - 1], u2[:CONV_TAIL])
    head2 = jnp.where(row == 1, prev[CONV_TAIL - 1:], head2)
    u1 = jnp.concatenate([head1, u1[CONV_TAIL:]], 0)
    u2 = jnp.concatenate([head2, u2[CONV_TAIL:]], 0)
    cw = cw_ref[...]
    conv = proj(3) * (cw[0:1] * u2 + cw[1:2] * u1 + cw[2:3] * u)
    o_ref[:, 3 * 512:] = conv.astype(BF16)
    o_ref[:, 2 * 512:3 * 512] = proj(2).astype(BF16)


def _even_proj(x, w, c, sa, sb, conv_w, seq, casts=()):
    t = x.shape[0]
    steps = t // TM
    per_seq = seq // TM
    tbl = pl.BlockSpec((TM, LANES), lambda i: (i % per_seq, 0))
    cast_in, cast_out, cast_shapes = _cast_specs(casts, (steps,))
    out = pl.pallas_call(
        functools.partial(_even_proj_kernel, n_cast=len(casts), per_seq=per_seq),
        grid=(steps,),
        in_specs=[pl.BlockSpec((TM, D_MODEL), lambda i: (i, 0)),
                  _resident((D_MODEL, EVEN_IN)), tbl, tbl, tbl,
                  _resident((3, CONV_CH))] + cast_in,
        out_specs=[pl.BlockSpec((TM, EVEN_OUT_COLS), lambda i: (i, 0))] + cast_out,
        out_shape=[jax.ShapeDtypeStruct((t, EVEN_OUT_COLS), BF16)] + cast_shapes,
        scratch_shapes=[pltpu.VMEM((CONV_TAIL, CONV_CH), F32)],
        compiler_params=_params(1, 28 + _cast_vmem_mib(casts, (steps,))),
        name="even_proj",
    )(x, w, c, sa, sb, conv_w, *[arr for arr, _ in casts])
    return out[0], out[1:]


def _attn_kernel(q_ref, k_ref, v_ref, lv_ref, g_ref, o_ref, s_ref, p_ref, *, lambda_init, seq):
    lv = lv_ref[...]
    lam = (jnp.exp(jnp.sum(lv[0:1] * lv[1:2], -1, keepdims=True))
           - jnp.exp(jnp.sum(lv[2:3] * lv[3:4], -1, keepdims=True)) + lambda_init)
    lane = lax.broadcasted_iota(jnp.int32, (ATT_TQ, LANES), 1)
    row = lax.broadcasted_iota(jnp.int32, (ATT_TQ, ATT_TQ), 0)
    col = lax.broadcasted_iota(jnp.int32, (ATT_TQ, ATT_TQ), 1)
    causal = col <= row
    gain = g_ref[...]
    nt = (((1,), (1,)), ((), ()))
    n_tiles = seq // ATT_TQ
    order = [t for pair in zip(range(n_tiles - 1, -1, -1), range(n_tiles)) for t in pair][:n_tiles]
    for pos, i in enumerate(order):
        buf = pos % 2
        lo = i * ATT_TQ
        q = q_ref[lo:lo + ATT_TQ, :]
        zero = jnp.zeros_like(q)
        qs = (jnp.where(lane < DIFF_DH, q, zero), jnp.where(lane >= DIFF_DH, q, zero))
        hi = lo + ATT_TQ
        l = []
        for c, qc in enumerate(qs):
            s = lax.dot_general(qc, k_ref[:hi, :], nt, preferred_element_type=F32)
            sc_ref = s_ref.at[buf, c]
            if i > 0:
                sc_ref[:, :lo] = s[:, :lo]
            sc_ref[:, lo:hi] = jnp.where(causal, s[:, lo:], NEG_INF)
            m = jnp.max(sc_ref[:, :hi], -1, keepdims=True)
            lsum = []
            for r in range(ATT_TQ // ATT_STRIP):
                sl = slice(r * ATT_STRIP, (r + 1) * ATT_STRIP)
                p = jnp.exp2(sc_ref[sl, :hi] - m[sl])
                lsum.append(jnp.sum(p, -1, keepdims=True))
                base = c * ATT_TQ + r * ATT_STRIP
                p_ref[buf, base:base + ATT_STRIP, :hi] = p.astype(BF16)
            l.append(jnp.concatenate(lsum, 0))
        acc = jnp.dot(p_ref[buf, :, :hi], v_ref[:hi, :], preferred_element_type=F32)
        acc = acc[:ATT_TQ] * (1.0 / l[0]) - acc[ATT_TQ:] * (lam / l[1])
        y = acc * lax.rsqrt(jnp.mean(acc * acc, -1, keepdims=True) + LN_EPS)
        o_ref[lo:lo + ATT_TQ, :] = (y * gain * (1.0 - lambda_init)).astype(BF16)


def _diff_attention(hcat, lam_vecs, norm_g, batch, seq, lambda_init):
    return pl.pallas_call(
        functools.partial(_attn_kernel, lambda_init=lambda_init, seq=seq),
        grid=(batch, DIFF_HEADS),
        in_specs=[pl.BlockSpec((seq, LANES), lambda b, h: (b, h)),
                  pl.BlockSpec((seq, LANES), lambda b, h: (b, DIFF_QK // LANES + h)),
                  pl.BlockSpec((seq, LANES), lambda b, h: (b, 2 * DIFF_QK // LANES + h)),
                  _resident((4, DIFF_DH)),
                  _resident((1, DIFF_DV))],
        out_specs=pl.BlockSpec((seq, DIFF_DV), lambda b, h: (b, h)),
        out_shape=jax.ShapeDtypeStruct((batch * seq, DIFF_WIDTH), BF16),
        scratch_shapes=[pltpu.VMEM((2, 2, ATT_TQ, seq), F32),
                        pltpu.VMEM((2, 2 * ATT_TQ, seq), BF16)],
        compiler_params=_params(2, 24),
        name="diff_attn",
    )(hcat, hcat, hcat, lam_vecs, norm_g)


def _even_out_kernel(x_ref, attn_ref, conv_ref, wout_ref, g_ref, b_ref, o_ref):
    rows = TM // OUT_SPLIT
    groups = [slice(r * rows, (r + 1) * rows) for r in range(OUT_SPLIT)]
    ys = []
    for sl in groups:
        mix = jnp.concatenate([attn_ref[sl, :], conv_ref[sl, :]], 1)
        ys.append(jnp.dot(mix, wout_ref[...], preferred_element_type=F32))
    for sl, y in zip(groups, ys):
        o_ref[sl, :] = _layer_norm(ALPHA * x_ref[sl, :] + y, g_ref[...], b_ref[...])


def _even_out(x, attn, hcat, w_out, g, b):
    t = x.shape[0]
    tm = TM
    conv_col = (2 * DIFF_QK + DIFF_WIDTH) // CONV_CH
    return pl.pallas_call(
        _even_out_kernel,
        grid=(t // tm,),
        in_specs=[pl.BlockSpec((tm, D_MODEL), lambda i: (i, 0)),
                  pl.BlockSpec((tm, DIFF_WIDTH), lambda i: (i, 0)),
                  pl.BlockSpec((tm, CONV_CH), lambda i: (i, conv_col)),
                  _resident((DIFF_WIDTH + CONV_CH, D_MODEL)),
                  _resident((1, D_MODEL)), _resident((1, D_MODEL))],
        out_specs=pl.BlockSpec((tm, D_MODEL), lambda i: (i, 0)),
        out_shape=jax.ShapeDtypeStruct((t, D_MODEL), F32),
        compiler_params=_params(1, 24),
        name="even_out",
    )(x, attn, hcat, w_out, g, b)


def _odd_mix_kernel(cd_ref, x_ref, w_ref, cos_ref, sin_ref, ng_ref, dmat_ref, qd_ref, kd_ref,
                    wout_ref, g_ref, b_ref, o_ref, state_ref, y_ref):
    @pl.when(pl.program_id(1) == 0)
    def _():
        state_ref[...] = jnp.zeros_like(state_ref)

    xb = x_ref[...].astype(BF16)
    cos = cos_ref[...]
    sin = sin_ref[...]
    half = RET_DK // 2
    k0, v0, g0 = RET_QK, 2 * RET_QK, 2 * RET_QK + RET_VW

    def proj(lo, n):
        return jnp.dot(xb, w_ref[:, lo:lo + n], preferred_element_type=F32)

    def rotary(r):
        x1, x2 = r[:, :half], r[:, half:]
        return jnp.concatenate([x1 * cos - x2 * sin, x2 * cos + x1 * sin], 1)

    def gate_of(lo, n):
        g = proj(g0 + lo, n)
        return g * jax.nn.sigmoid(g) * ng_ref[:, lo:lo + n]

    def projection(j):
        hv = RET_DV // 2
        return [lambda: rotary(proj(j * RET_DK, RET_DK)).astype(BF16),
                lambda: rotary(proj(k0 + j * RET_DK, RET_DK)) * RET_DK ** -0.5,
                lambda: proj(v0 + j * RET_DV, hv).astype(BF16),
                lambda: proj(v0 + j * RET_DV + hv, hv).astype(BF16),
                lambda: gate_of(j * RET_DV, hv),
                lambda: gate_of(j * RET_DV + hv, hv)]

    def assemble(pieces):
        q, k, v_lo, v_hi, gate_lo, gate_hi = pieces
        return q, k, jnp.concatenate([v_lo, v_hi], 1), jnp.concatenate([gate_lo, gate_hi], 1)

    n_blocks = ODD_TM // RET_BLOCK
    ahead = [piece() for piece in projection(0)]
    for j in range(RET_HEADS):
        q, k, v, gate = assemble(ahead)
        kb = k.astype(BF16)
        pending = projection(j + 1) if j + 1 < RET_HEADS else []
        ahead = []
        cd = cd_ref[j]
        dmat = dmat_ref[j]
        qd = jnp.concatenate([qd_ref[j]] * (RET_DV // LANES), axis=1)
        kd = jnp.concatenate([kd_ref[j]] * (RET_DK // LANES), axis=1)
        state = state_ref[j]
        for c in range(n_blocks):
            share = range(c * len(pending) // n_blocks, (c + 1) * len(pending) // n_blocks)
            ahead += [pending[i]() for i in share]
            rows = slice(c * RET_BLOCK, (c + 1) * RET_BLOCK)
            qc, kc, vc = q[rows], kb[rows], v[rows]
            sc = lax.dot_general(qc, kc, (((1,), (1,)), ((), ())),
                                 preferred_element_type=F32) * dmat
            y = jnp.dot(sc.astype(BF16), vc, preferred_element_type=F32)
            y = y + jnp.dot(qc, state.astype(BF16), preferred_element_type=F32) * qd
            kdec = (k[rows] * kd).astype(BF16)
            state = state * cd + lax.dot_general(kdec, vc, (((0,), (0,)), ((), ())),
                                                 preferred_element_type=F32)
            mu = jnp.mean(y, -1, keepdims=True)
            d = y - mu
            var = jnp.mean(d * d, -1, keepdims=True)
            yn = d * lax.rsqrt(var + LN_EPS)
            y_ref[rows, j * RET_DV:(j + 1) * RET_DV] = (gate[rows] * yn).astype(BF16)
        state_ref[j] = state

    rows = ODD_TM // OUT_SPLIT
    groups = [slice(r * rows, (r + 1) * rows) for r in range(OUT_SPLIT)]
    ys = [jnp.dot(y_ref[sl, :], wout_ref[...], preferred_element_type=F32) for sl in groups]
    for sl, y in zip(groups, ys):
        o_ref[sl, :] = _layer_norm(ALPHA * x_ref[sl, :] + y, g_ref[...], b_ref[...])


def _odd_layer_mix(x, w, cos, sin, norm_g, tables, w_out, g, b, batch, seq):
    cd, dmat, qd, kd = tables
    per_seq = seq // ODD_TM
    tile = lambda b, p: (b * per_seq + p, 0)
    tbl = pl.BlockSpec((ODD_TM, LANES), lambda b, p: (p, 0))
    return pl.pallas_call(
        _odd_mix_kernel,
        grid=(batch, per_seq),
        in_specs=[pl.BlockSpec(memory_space=pltpu.SMEM),
                  pl.BlockSpec((ODD_TM, D_MODEL), tile),
                  _resident((D_MODEL, ODD_IN)), tbl, tbl, _resident((1, RET_VW)),
                  _resident((RET_HEADS, RET_BLOCK, RET_BLOCK)),
                  _resident((RET_HEADS, RET_BLOCK, LANES)),
                  _resident((RET_HEADS, RET_BLOCK, LANES)),
                  _resident((RET_VW, D_MODEL)),
                  _resident((1, D_MODEL)), _resident((1, D_MODEL))],
        out_specs=pl.BlockSpec((ODD_TM, D_MODEL), tile),
        out_shape=jax.ShapeDtypeStruct((batch * seq, D_MODEL), F32),
        scratch_shapes=[pltpu.VMEM((RET_HEADS, RET_DK, RET_DV), F32),
                        pltpu.VMEM((ODD_TM, RET_VW), BF16)],
        compiler_params=_params(2, 56),
        name="odd_mix",
    )(cd, x, w, cos, sin, norm_g, dmat, qd, kd, w_out, g, b)


def _rope_angles(seq, n_rot, theta):
    inv = np.exp(-math.log(theta) * np.arange(n_rot // 2, dtype=np.float64) * (2.0 / n_rot))
    return np.arange(seq, dtype=np.float64)[:, None] * inv[None, :]


def _even_rope_tables(seq):
    half = ROPE_DIMS // 2
    ang = _rope_angles(seq, ROPE_DIMS, ROPE_THETA)
    cos, sin = np.cos(ang), np.sin(ang)
    pad = np.zeros((seq, DIFF_DH - ROPE_DIMS))
    zeros = np.zeros((seq, half))
    c = np.concatenate([cos, cos, pad + 1.0], -1)
    sa = np.concatenate([-sin, zeros, pad], -1)
    sb = np.concatenate([zeros, sin, pad], -1)
    reps = LANES // DIFF_DH
    return tuple(jnp.asarray(np.tile(t, (1, reps)), F32) for t in (c, sa, sb))


def _odd_rope_tables(seq):
    ang = _rope_angles(seq, RET_DK, RET_ROPE_THETA)
    return jnp.asarray(np.cos(ang), F32), jnp.asarray(np.sin(ang), F32)


def _retention_tables():
    log_g = np.log1p(-np.exp2(-5.0 - np.arange(RET_HEADS, dtype=np.float64)))
    idx = np.arange(RET_BLOCK, dtype=np.float64)
    rel = idx[:, None] - idx[None, :]
    dmat = np.where(rel >= 0, np.exp(log_g[:, None, None] * np.maximum(rel, 0.0)), 0.0)
    q_decay = np.exp(log_g[:, None] * (idx + 1.0))
    k_decay = np.exp(log_g[:, None] * (RET_BLOCK - 1.0 - idx))
    cd = np.exp(log_g * RET_BLOCK)
    rep = lambda t: np.broadcast_to(t[:, :, None], (RET_HEADS, RET_BLOCK, LANES))
    return tuple(jnp.asarray(t, F32) for t in (cd, dmat, rep(q_decay), rep(k_decay)))


def kernel(x, ln_g, ln_b, ffn_w_in, ffn_w_out, even_w_in, even_w_out, diff_lambda,
           diff_norm_g, conv_w, odd_w_in, odd_w_out, ret_norm_g):
    batch, seq, _ = x.shape
    assert seq % TM == 0 and seq % ODD_TM == 0 and seq % ATT_TQ == 0 and DEPTH == 2
    h = x.reshape(batch * seq, D_MODEL)
    row = lambda v: v.reshape(1, -1)
    norm = lambda i, k: (row(ln_g[i, k]), row(ln_b[i, k]))
    ffn = lambda i, k: [(ffn_w_in, (i, k)), (ffn_w_out, (i, k))]


    h, (e_in, e_out) = _ffn_ln(h, ffn_w_in, ffn_w_out, *norm(0, 0), stream=(0, 0),
                               casts=[(even_w_in, (0,)), (even_w_out, (0,))])
    lambda_init = 0.8 - 0.6 * math.exp(-0.3 * 0)
    hcat, (w_in, w_out, *next_ffn) = _even_proj(
        h, e_in, *_even_rope_tables(seq), conv_w[0], seq, casts=ffn(0, 1) + ffn(1, 0))
    attn = _diff_attention(hcat, diff_lambda[0], row(diff_norm_g[0]), batch, seq, lambda_init)
    h = _even_out(h, attn, hcat, e_out, *norm(0, 1))
    h, (o_in, o_out) = _ffn_ln(h, w_in, w_out, *norm(0, 2),
                               casts=[(odd_w_in, (0,)), (odd_w_out, (0,))])

    h, last_ffn = _ffn_ln(h, *next_ffn, *norm(1, 0), casts=ffn(1, 1))
    h = _odd_layer_mix(h, o_in, *_odd_rope_tables(seq), row(ret_norm_g[0]),
                       _retention_tables(), o_out, *norm(1, 1), batch, seq)
    h, _ = _ffn_ln(h, *last_ffn, *norm(1, 2))
    return h.reshape(batch, seq, D_MODEL)
```

```python
import functools
import math

import jax
import jax.numpy as jnp
import numpy as np
from jax import lax
from jax.experimental import pallas as pl
from jax.experimental.pallas import tpu as pltpu

F32 = jnp.float32
BF16 = jnp.bfloat16

D_MODEL = 1024
DEPTH = 2
D_FF = 2816
DIFF_HEADS = 4
DIFF_DH = 64
DIFF_DV = 128
DIFF_QK = 512
DIFF_WIDTH = 512
ROPE_THETA = 500000.0
ROPE_DIMS = 16
CONV_CH = 512
EVEN_IN = 3072
RET_HEADS = 4
RET_DK = 256
RET_DV = 512
RET_QK = 1024
RET_VW = 2048
RET_ROPE_THETA = 10000.0
ODD_IN = 6144
ALPHA = (2.0 * DEPTH) ** 0.25
LN_EPS = 1e-5
NEG_INF = -1e30
LOG2E = math.log2(math.e)

LANES = 128
MIB = 1024 * 1024
VMEM_LIMIT = 56 * MIB

TM = 1024
FF_CHUNK = 256
OUT_SPLIT = 4
FFN_ROW_GROUPS = (256, 256, 256, 256)
ODD_TM = 1024
RET_BLOCK = 256
ATT_TQ = 256
ATT_STRIP = 16


def _layer_norm(z, g, b):
    mu = jnp.mean(z, -1, keepdims=True)
    d = z - mu
    var = jnp.mean(d * d, -1, keepdims=True)
    return d * lax.rsqrt(var + LN_EPS) * g + b


def _params(n_axes, vmem_mib):
    assert vmem_mib * MIB <= VMEM_LIMIT
    return pltpu.CompilerParams(dimension_semantics=("arbitrary",) * n_axes,
                                vmem_limit_bytes=vmem_mib * MIB)


def _resident(shape):
    nd = len(shape)
    return pl.BlockSpec(shape, lambda *_: (0,) * nd, pipeline_mode=pl.Buffered(1))


def _cast_specs(casts, grid):
    steps = math.prod(grid)

    def step(*idx):
        lin = idx[0]
        for size, i in zip(grid[1:], idx[1:]):
            lin = lin * size + i
        return lin

    in_specs, out_specs, out_shapes = [], [], []
    for arr, prefix in casts:
        r, c = arr.shape[-2:]
        assert arr.ndim == len(prefix) + 2
        rows = min(d for d in range(16, r + 1, 16) if r % d == 0 and d * steps >= r)
        last = r // rows - 1
        in_specs.append(pl.BlockSpec(
            (None,) * len(prefix) + (rows, c),
            lambda *idx, p=prefix, n=last: p + (jnp.minimum(step(*idx), n), 0)))
        out_specs.append(pl.BlockSpec(
            (rows, c), lambda *idx, n=last: (jnp.minimum(step(*idx), n), 0)))
        out_shapes.append(jax.ShapeDtypeStruct((r, c), BF16))
    return in_specs, out_specs, out_shapes


def _cast_vmem_mib(casts, grid):
    in_specs, _, _ = _cast_specs(casts, grid)
    elems = sum(math.prod(d for d in spec.block_shape if d is not None) for spec in in_specs)
    return -(-elems * (4 + 2) * 2 // MIB)


def _do_casts(src_refs, dst_refs):
    for src, dst in zip(src_refs, dst_refs):
        dst[...] = src[...].astype(BF16)


W_STREAM_STEPS = 16
W_STREAM_SLOTS = 3


def _stream_cast(pick, sem_ref, streams):
    def copy(s, c):
        src, _, stage = streams[s]
        slots, rows = stage.shape[:2]
        return pltpu.make_async_copy(src.at[pick[0], pick[1], pl.ds(c * rows, rows), :],
                                     stage.at[c % slots], sem_ref.at[s, c % slots])

    for s, (_, _, stage) in enumerate(streams):
        for c in range(stage.shape[0] - 1):
            copy(s, c).start()
    for c in range(W_STREAM_STEPS):
        for s, (_, dst, stage) in enumerate(streams):
            slots, rows = stage.shape[:2]
            if c + slots - 1 < W_STREAM_STEPS:
                copy(s, c + slots - 1).start()
            copy(s, c).wait()
            dst[c * rows:(c + 1) * rows, :] = stage[c % slots].astype(BF16)


def _ffn_kernel(x_ref, win_ref, wout_ref, g_ref, b_ref, *rest, n_cast, stream):
    cast_src, o_ref, cast_dst = rest[:n_cast], rest[n_cast], rest[n_cast + 1:2 * n_cast + 1]
    act_ref = rest[2 * n_cast + 1]
    if stream is not None:
        win_bf, wout_bf, stage_in, stage_out, sems = rest[2 * n_cast + 2:]

        @pl.when(pl.program_id(0) == 0)
        def _():
            _stream_cast(stream, sems, [(win_ref, win_bf, stage_in),
                                        (wout_ref, wout_bf, stage_out)])

        win_ref, wout_ref = win_bf, wout_bf
    _do_casts(cast_src, cast_dst)
    x = x_ref[...]
    xb = x.astype(BF16)
    for c in range(D_FF // FF_CHUNK):
        lo = c * FF_CHUNK
        gate = jnp.dot(xb, win_ref[:, lo:lo + FF_CHUNK], preferred_element_type=F32)
        up = jnp.dot(xb, win_ref[:, D_FF + lo:D_FF + lo + FF_CHUNK], preferred_element_type=F32)
        act_ref[:, lo:lo + FF_CHUNK] = (gate * jax.nn.sigmoid(gate) * up).astype(BF16)
    bounds = [sum(FFN_ROW_GROUPS[:r]) for r in range(len(FFN_ROW_GROUPS) + 1)]
    groups = [slice(lo, hi) for lo, hi in zip(bounds[:-1], bounds[1:])]
    ys = [jnp.dot(act_ref[sl, :], wout_ref[...], preferred_element_type=F32) for sl in groups]
    for sl, y in zip(groups, ys):
        o_ref[sl, :] = _layer_norm(ALPHA * x_ref[sl, :] + 0.5 * y, g_ref[...], b_ref[...])


def _ffn_ln(x, w_in, w_out, g, b, casts=(), stream=None):
    t = x.shape[0]
    steps = t // TM
    cast_in, cast_out, cast_shapes = _cast_specs(casts, (steps,))
    scratch = [pltpu.VMEM((TM, D_FF), BF16)]
    if stream is None:
        weights = [_resident((D_MODEL, 2 * D_FF)), _resident((D_FF, D_MODEL))]
        stage_mib = 0
    else:
        weights = [pl.BlockSpec(memory_space=pl.ANY)] * 2
        rows_in, rows_out = D_MODEL // W_STREAM_STEPS, D_FF // W_STREAM_STEPS
        scratch += [pltpu.VMEM((D_MODEL, 2 * D_FF), BF16), pltpu.VMEM((D_FF, D_MODEL), BF16),
                    pltpu.VMEM((W_STREAM_SLOTS, rows_in, 2 * D_FF), F32),
                    pltpu.VMEM((W_STREAM_SLOTS, rows_out, D_MODEL), F32),
                    pltpu.SemaphoreType.DMA((2, W_STREAM_SLOTS))]
        stage_mib = -(-W_STREAM_SLOTS * 4 * (rows_in * 2 * D_FF + rows_out * D_MODEL) // MIB)
    out = pl.pallas_call(
        functools.partial(_ffn_kernel, n_cast=len(casts), stream=stream),
        grid=(steps,),
        in_specs=[pl.BlockSpec((TM, D_MODEL), lambda i: (i, 0))] + weights
                 + [_resident((1, D_MODEL)), _resident((1, D_MODEL))] + cast_in,
        out_specs=[pl.BlockSpec((TM, D_MODEL), lambda i: (i, 0))] + cast_out,
        out_shape=[jax.ShapeDtypeStruct((t, D_MODEL), F32)] + cast_shapes,
        scratch_shapes=scratch,
        compiler_params=_params(1, 45 + stage_mib + _cast_vmem_mib(casts, (steps,))),
        name="ffn_ln",
    )(x, w_in, w_out, g, b, *[arr for arr, _ in casts])
    return out[0], out[1:]


EVEN_OUT_COLS = 2 * DIFF_QK + DIFF_WIDTH + CONV_CH
CONV_TAIL = 8


def _even_proj_kernel(x_ref, w_ref, c_ref, sa_ref, sb_ref, cw_ref, *rest, n_cast, per_seq):
    cast_src, o_ref, cast_dst, tail_ref = (rest[:n_cast], rest[n_cast],
                                           rest[n_cast + 1:2 * n_cast + 1], rest[-1])
    _do_casts(cast_src, cast_dst)
    i = pl.program_id(0)

    @pl.when(i == 0)
    def _():
        tail_ref[...] = jnp.zeros_like(tail_ref)

    xb = x_ref[...].astype(BF16)
    c = c_ref[...]
    sa = sa_ref[...]
    sb = sb_ref[...]
    proj = lambda n: jnp.dot(xb, w_ref[:, n * 512:(n + 1) * 512], preferred_element_type=F32)
    for n in range(2):
        r = proj(n)
        scale = DIFF_DH ** -0.5 * LOG2E if n == 0 else 1.0
        for j in range(512 // LANES):
            blk = r[:, j * LANES:(j + 1) * LANES]
            rot = (blk * c + pltpu.roll(blk, LANES - 8, 1) * sa
                   + pltpu.roll(blk, 8, 1) * sb)
            if n == 0:
                rot = rot * scale
            o_ref[:, n * 512 + j * LANES:n * 512 + (j + 1) * LANES] = rot.astype(BF16)
    u = proj(4) * proj(5)
    prev = jnp.where(i % per_seq == 0, jnp.zeros_like(u[:CONV_TAIL]), tail_ref[...])
    tail_ref[...] = u[TM - CONV_TAIL:]
    row = lax.broadcasted_iota(jnp.int32, (CONV_TAIL, CONV_CH), 0)
    u1 = pltpu.roll(u, 1, 0)
    u2 = pltpu.roll(u, 2, 0)
    head1 = jnp.where(row == 0, prev[CONV_TAIL - 1:], u1[:CONV_TAIL])
    head2 = jnp.where(row == 0, prev[CONV_TAIL - 2:CONV_TAIL - 1], u2[:CONV_TAIL])
    head2 = jnp.where(row == 1, prev[CONV_TAIL - 1:], head2)
    u1 = jnp.concatenate([head1, u1[CONV_TAIL:]], 0)
    u2 = jnp.concatenate([head2, u2[CONV_TAIL:]], 0)
    cw = cw_ref[...]
    conv = proj(3) * (cw[0:1] * u2 + cw[1:2] * u1 + cw[2:3] * u)
    o_ref[:, 3 * 512:] = conv.astype(BF16)
    o_ref[:, 2 * 512:3 * 512] = proj(2).astype(BF16)


def _even_proj(x, w, c, sa, sb, conv_w, seq, casts=()):
    t = x.shape[0]
    steps = t // TM
    per_seq = seq // TM
    tbl = pl.BlockSpec((TM, LANES), lambda i: (i % per_seq, 0))
    cast_in, cast_out, cast_shapes = _cast_specs(casts, (steps,))
    out = pl.pallas_call(
        functools.partial(_even_proj_kernel, n_cast=len(casts), per_seq=per_seq),
        grid=(steps,),
        in_specs=[pl.BlockSpec((TM, D_MODEL), lambda i: (i, 0)),
                  _resident((D_MODEL, EVEN_IN)), tbl, tbl, tbl,
                  _resident((3, CONV_CH))] + cast_in,
        out_specs=[pl.BlockSpec((TM, EVEN_OUT_COLS), lambda i: (i, 0))] + cast_out,
        out_shape=[jax.ShapeDtypeStruct((t, EVEN_OUT_COLS), BF16)] + cast_shapes,
        scratch_shapes=[pltpu.VMEM((CONV_TAIL, CONV_CH), F32)],
        compiler_params=_params(1, 28 + _cast_vmem_mib(casts, (steps,))),
        name="even_proj",
    )(x, w, c, sa, sb, conv_w, *[arr for arr, _ in casts])
    return out[0], out[1:]


def _attn_kernel(q_ref, k_ref, v_ref, lv_ref, g_ref, o_ref, s_ref, p_ref, *, lambda_init, seq):
    lv = lv_ref[...]
    lam = (jnp.exp(jnp.sum(lv[0:1] * lv[1:2], -1, keepdims=True))
           - jnp.exp(jnp.sum(lv[2:3] * lv[3:4], -1, keepdims=True)) + lambda_init)
    lane = lax.broadcasted_iota(jnp.int32, (ATT_TQ, LANES), 1)
    row = lax.broadcasted_iota(jnp.int32, (ATT_TQ, ATT_TQ), 0)
    col = lax.broadcasted_iota(jnp.int32, (ATT_TQ, ATT_TQ), 1)
    causal = col <= row
    gain = g_ref[...]
    nt = (((1,), (1,)), ((), ()))
    n_tiles = seq // ATT_TQ
    order = [t for pair in zip(range(n_tiles - 1, -1, -1), range(n_tiles)) for t in pair][:n_tiles]

    def scores(pos):
        lo = order[pos] * ATT_TQ
        hi = lo + ATT_TQ
        q = q_ref[lo:hi, :]
        zero = jnp.zeros_like(q)
        qs = (jnp.where(lane < DIFF_DH, q, zero), jnp.where(lane >= DIFF_DH, q, zero))
        maxima = []
        for c, qc in enumerate(qs):
            s = lax.dot_general(qc, k_ref[:hi, :], nt, preferred_element_type=F32)
            sc_ref = s_ref.at[pos % 2, c]
            if lo > 0:
                sc_ref[:, :lo] = s[:, :lo]
            sc_ref[:, lo:hi] = jnp.where(causal, s[:, lo:], NEG_INF)
            maxima.append(jnp.max(sc_ref[:, :hi], -1, keepdims=True))
        return maxima

    maxima = scores(0)
    for pos, i in enumerate(order):
        buf = pos % 2
        lo = i * ATT_TQ
        hi = lo + ATT_TQ
        m = maxima
        if pos + 1 < n_tiles:
            maxima = scores(pos + 1)
        l = []
        for c in range(2):
            lsum = []
            for r in range(ATT_TQ // ATT_STRIP):
                sl = slice(r * ATT_STRIP, (r + 1) * ATT_STRIP)
                p = jnp.exp2(s_ref[buf, c, sl, :hi] - m[c][sl])
                lsum.append(jnp.sum(p, -1, keepdims=True))
                base = c * ATT_TQ + r * ATT_STRIP
                p_ref[buf, base:base + ATT_STRIP, :hi] = p.astype(BF16)
            l.append(jnp.concatenate(lsum, 0))
        acc = jnp.dot(p_ref[buf, :, :hi], v_ref[:hi, :], preferred_element_type=F32)
        acc = acc[:ATT_TQ] * (1.0 / l[0]) - acc[ATT_TQ:] * (lam / l[1])
        y = acc * lax.rsqrt(jnp.mean(acc * acc, -1, keepdims=True) + LN_EPS)
        o_ref[lo:hi, :] = (y * gain * (1.0 - lambda_init)).astype(BF16)


def _diff_attention(hcat, lam_vecs, norm_g, batch, seq, lambda_init):
    return pl.pallas_call(
        functools.partial(_attn_kernel, lambda_init=lambda_init, seq=seq),
        grid=(batch, DIFF_HEADS),
        in_specs=[pl.BlockSpec((seq, LANES), lambda b, h: (b, h)),
                  pl.BlockSpec((seq, LANES), lambda b, h: (b, DIFF_QK // LANES + h)),
                  pl.BlockSpec((seq, LANES), lambda b, h: (b, 2 * DIFF_QK // LANES + h)),
                  _resident((4, DIFF_DH)),
                  _resident((1, DIFF_DV))],
        out_specs=pl.BlockSpec((seq, DIFF_DV), lambda b, h: (b, h)),
        out_shape=jax.ShapeDtypeStruct((batch * seq, DIFF_WIDTH), BF16),
        scratch_shapes=[pltpu.VMEM((2, 2, ATT_TQ, seq), F32),
                        pltpu.VMEM((2, 2 * ATT_TQ, seq), BF16)],
        compiler_params=_params(2, 24),
        name="diff_attn",
    )(hcat, hcat, hcat, lam_vecs, norm_g)


def _even_out_kernel(x_ref, attn_ref, conv_ref, wout_ref, g_ref, b_ref, o_ref):
    rows = TM // OUT_SPLIT
    groups = [slice(r * rows, (r + 1) * rows) for r in range(OUT_SPLIT)]
    ys = []
    for sl in groups:
        mix = jnp.concatenate([attn_ref[sl, :], conv_ref[sl, :]], 1)
        ys.append(jnp.dot(mix, wout_ref[...], preferred_element_type=F32))
    for sl, y in zip(groups, ys):
        o_ref[sl, :] = _layer_norm(ALPHA * x_ref[sl, :] + y, g_ref[...], b_ref[...])


def _even_out(x, attn, hcat, w_out, g, b):
    t = x.shape[0]
    tm = TM
    conv_col = (2 * DIFF_QK + DIFF_WIDTH) // CONV_CH
    return pl.pallas_call(
        _even_out_kernel,
        grid=(t // tm,),
        in_specs=[pl.BlockSpec((tm, D_MODEL), lambda i: (i, 0)),
                  pl.BlockSpec((tm, DIFF_WIDTH), lambda i: (i, 0)),
                  pl.BlockSpec((tm, CONV_CH), lambda i: (i, conv_col)),
                  _resident((DIFF_WIDTH + CONV_CH, D_MODEL)),
                  _resident((1, D_MODEL)), _resident((1, D_MODEL))],
        out_specs=pl.BlockSpec((tm, D_MODEL), lambda i: (i, 0)),
        out_shape=jax.ShapeDtypeStruct((t, D_MODEL), F32),
        compiler_params=_params(1, 24),
        name="even_out",
    )(x, attn, hcat, w_out, g, b)


def _odd_mix_kernel(cd_ref, x_ref, w_ref, cos_ref, sin_ref, ng_ref, dmat_ref, qd_ref, kd_ref,
                    wout_ref, g_ref, b_ref, o_ref, state_ref, y_ref):
    @pl.when(pl.program_id(1) == 0)
    def _():
        state_ref[...] = jnp.zeros_like(state_ref)

    xb = x_ref[...].astype(BF16)
    cos = cos_ref[...]
    sin = sin_ref[...]
    half = RET_DK // 2
    k0, v0, g0 = RET_QK, 2 * RET_QK, 2 * RET_QK + RET_VW

    def proj(lo, n):
        return jnp.dot(xb, w_ref[:, lo:lo + n], preferred_element_type=F32)

    def rotary(r):
        x1, x2 = r[:, :half], r[:, half:]
        return jnp.concatenate([x1 * cos - x2 * sin, x2 * cos + x1 * sin], 1)

    def gate_of(lo, n):
        g = proj(g0 + lo, n)
        return g * jax.nn.sigmoid(g) * ng_ref[:, lo:lo + n]

    def projection(j):
        hv = RET_DV // 2
        return [lambda: rotary(proj(j * RET_DK, RET_DK)).astype(BF16),
                lambda: rotary(proj(k0 + j * RET_DK, RET_DK)) * RET_DK ** -0.5,
                lambda: proj(v0 + j * RET_DV, hv).astype(BF16),
                lambda: proj(v0 + j * RET_DV + hv, hv).astype(BF16),
                lambda: gate_of(j * RET_DV, hv),
                lambda: gate_of(j * RET_DV + hv, hv)]

    def assemble(pieces):
        q, k, v_lo, v_hi, gate_lo, gate_hi = pieces
        return q, k, jnp.concatenate([v_lo, v_hi], 1), jnp.concatenate([gate_lo, gate_hi], 1)

    n_blocks = ODD_TM // RET_BLOCK
    ahead = [piece() for piece in projection(0)]
    for j in range(RET_HEADS):
        q, k, v, gate = assemble(ahead)
        kb = k.astype(BF16)
        pending = projection(j + 1) if j + 1 < RET_HEADS else []
        ahead = []
        cd = cd_ref[j]
        dmat = dmat_ref[j]
        qd = jnp.concatenate([qd_ref[j]] * (RET_DV // LANES), axis=1)
        kd = jnp.concatenate([kd_ref[j]] * (RET_DK // LANES), axis=1)
        state = state_ref[j]
        for c in range(n_blocks):
            share = range(c * len(pending) // n_blocks, (c + 1) * len(pending) // n_blocks)
            ahead += [pending[i]() for i in share]
            rows = slice(c * RET_BLOCK, (c + 1) * RET_BLOCK)
            qc, kc, vc = q[rows], kb[rows], v[rows]
            sc = lax.dot_general(qc, kc, (((1,), (1,)), ((), ())),
                                 preferred_element_type=F32) * dmat
            y = jnp.dot(sc.astype(BF16), vc, preferred_element_type=F32)
            y = y + jnp.dot(qc, state.astype(BF16), preferred_element_type=F32) * qd
            kdec = (k[rows] * kd).astype(BF16)
            state = state * cd + lax.dot_general(kdec, vc, (((0,), (0,)), ((), ())),
                                                 preferred_element_type=F32)
            mu = jnp.mean(y, -1, keepdims=True)
            d = y - mu
            var = jnp.mean(d * d, -1, keepdims=True)
            yn = d * lax.rsqrt(var + LN_EPS)
            y_ref[rows, j * RET_DV:(j + 1) * RET_DV] = (gate[rows] * yn).astype(BF16)
        state_ref[j] = state

    rows = ODD_TM // OUT_SPLIT
    groups = [slice(r * rows, (r + 1) * rows) for r in range(OUT_SPLIT)]
    ys = [jnp.dot(y_ref[sl, :], wout_ref[...], preferred_element_type=F32) for sl in groups]
    for sl, y in zip(groups, ys):
        o_ref[sl, :] = _layer_norm(ALPHA * x_ref[sl, :] + y, g_ref[...], b_ref[...])


def _odd_layer_mix(x, w, cos, sin, norm_g, tables, w_out, g, b, batch, seq):
    cd, dmat, qd, kd = tables
    per_seq = seq // ODD_TM
    tile = lambda b, p: (b * per_seq + p, 0)
    tbl = pl.BlockSpec((ODD_TM, LANES), lambda b, p: (p, 0))
    return pl.pallas_call(
        _odd_mix_kernel,
        grid=(batch, per_seq),
        in_specs=[pl.BlockSpec(memory_space=pltpu.SMEM),
                  pl.BlockSpec((ODD_TM, D_MODEL), tile),
                  _resident((D_MODEL, ODD_IN)), tbl, tbl, _resident((1, RET_VW)),
                  _resident((RET_HEADS, RET_BLOCK, RET_BLOCK)),
                  _resident((RET_HEADS, RET_BLOCK, LANES)),
                  _resident((RET_HEADS, RET_BLOCK, LANES)),
                  _resident((RET_VW, D_MODEL)),
                  _resident((1, D_MODEL)), _resident((1, D_MODEL))],
        out_specs=pl.BlockSpec((ODD_TM, D_MODEL), tile),
        out_shape=jax.ShapeDtypeStruct((batch * seq, D_MODEL), F32),
        scratch_shapes=[pltpu.VMEM((RET_HEADS, RET_DK, RET_DV), F32),
                        pltpu.VMEM((ODD_TM, RET_VW), BF16)],
        compiler_params=_params(2, 56),
        name="odd_mix",
    )(cd, x, w, cos, sin, norm_g, dmat, qd, kd, w_out, g, b)


def _rope_angles(seq, n_rot, theta):
    inv = np.exp(-math.log(theta) * np.arange(n_rot // 2, dtype=np.float64) * (2.0 / n_rot))
    return np.arange(seq, dtype=np.float64)[:, None] * inv[None, :]


def _even_rope_tables(seq):
    half = ROPE_DIMS // 2
    ang = _rope_angles(seq, ROPE_DIMS, ROPE_THETA)
    cos, sin = np.cos(ang), np.sin(ang)
    pad = np.zeros((seq, DIFF_DH - ROPE_DIMS))
    zeros = np.zeros((seq, half))
    c = np.concatenate([cos, cos, pad + 1.0], -1)
    sa = np.concatenate([-sin, zeros, pad], -1)
    sb = np.concatenate([zeros, sin, pad], -1)
    reps = LANES // DIFF_DH
    return tuple(jnp.asarray(np.tile(t, (1, reps)), F32) for t in (c, sa, sb))


def _odd_rope_tables(seq):
    ang = _rope_angles(seq, RET_DK, RET_ROPE_THETA)
    return jnp.asarray(np.cos(ang), F32), jnp.asarray(np.sin(ang), F32)


def _retention_tables():
    log_g = np.log1p(-np.exp2(-5.0 - np.arange(RET_HEADS, dtype=np.float64)))
    idx = np.arange(RET_BLOCK, dtype=np.float64)
    rel = idx[:, None] - idx[None, :]
    dmat = np.where(rel >= 0, np.exp(log_g[:, None, None] * np.maximum(rel, 0.0)), 0.0)
    q_decay = np.exp(log_g[:, None] * (idx + 1.0))
    k_decay = np.exp(log_g[:, None] * (RET_BLOCK - 1.0 - idx))
    cd = np.exp(log_g * RET_BLOCK)
    rep = lambda t: np.broadcast_to(t[:, :, None], (RET_HEADS, RET_BLOCK, LANES))
    return tuple(jnp.asarray(t, F32) for t in (cd, dmat, rep(q_decay), rep(k_decay)))


def kernel(x, ln_g, ln_b, ffn_w_in, ffn_w_out, even_w_in, even_w_out, diff_lambda,
           diff_norm_g, conv_w, odd_w_in, odd_w_out, ret_norm_g):
    batch, seq, _ = x.shape
    assert seq % TM == 0 and seq % ODD_TM == 0 and seq % ATT_TQ == 0 and DEPTH == 2
    h = x.reshape(batch * seq, D_MODEL)
    row = lambda v: v.reshape(1, -1)
    norm = lambda i, k: (row(ln_g[i, k]), row(ln_b[i, k]))
    ffn = lambda i, k: [(ffn_w_in, (i, k)), (ffn_w_out, (i, k))]


    h, (e_in, e_out) = _ffn_ln(h, ffn_w_in, ffn_w_out, *norm(0, 0), stream=(0, 0),
                               casts=[(even_w_in, (0,)), (even_w_out, (0,))])
    lambda_init = 0.8 - 0.6 * math.exp(-0.3 * 0)
    hcat, (w_in, w_out, *next_ffn) = _even_proj(
        h, e_in, *_even_rope_tables(seq), conv_w[0], seq, casts=ffn(0, 1) + ffn(1, 0))
    attn = _diff_attention(hcat, diff_lambda[0], row(diff_norm_g[0]), batch, seq, lambda_init)
    h = _even_out(h, attn, hcat, e_out, *norm(0, 1))
    h, (o_in, o_out) = _ffn_ln(h, w_in, w_out, *norm(0, 2),
                               casts=[(odd_w_in, (0,)), (odd_w_out, (0,))])

    h, last_ffn = _ffn_ln(h, *next_ffn, *norm(1, 0), casts=ffn(1, 1))
    h = _odd_layer_mix(h, o_in, *_odd_rope_tables(seq), row(ret_norm_g[0]),
                       _retention_tables(), o_out, *norm(1, 1), batch, seq)
    h, _ = _ffn_ln(h, *last_ffn, *norm(1, 2))
    return h.reshape(batch, seq, D_MODEL)
```

```python
import functools
import math

import jax
import jax.numpy as jnp
import numpy as np
from jax import lax
from jax.experimental import pallas as pl
from jax.experimental.pallas import tpu as pltpu

F32 = jnp.float32
BF16 = jnp.bfloat16

D_MODEL = 1024
DEPTH = 2
D_FF = 2816
DIFF_HEADS = 4
DIFF_DH = 64
DIFF_DV = 128
DIFF_QK = 512
DIFF_WIDTH = 512
ROPE_THETA = 500000.0
ROPE_DIMS = 16
CONV_CH = 512
EVEN_IN = 3072
RET_HEADS = 4
RET_DK = 256
RET_DV = 512
RET_QK = 1024
RET_VW = 2048
RET_ROPE_THETA = 10000.0
ODD_IN = 6144
ALPHA = (2.0 * DEPTH) ** 0.25
LN_EPS = 1e-5
NEG_INF = -1e30
LOG2E = math.log2(math.e)

LANES = 128
MIB = 1024 * 1024
VMEM_LIMIT = 56 * MIB

TM = 1024
FF_CHUNK = 256
OUT_SPLIT = 4
FFN_ROW_GROUPS = (256, 256, 256, 256)
ODD_TM = 1024
RET_BLOCK = 256
ATT_TQ = 256
ATT_STRIP = 16


def _layer_norm(z, g, b):
    mu = jnp.mean(z, -1, keepdims=True)
    d = z - mu
    var = jnp.mean(d * d, -1, keepdims=True)
    return d * lax.rsqrt(var + LN_EPS) * g + b


def _params(n_axes, vmem_mib):
    assert vmem_mib * MIB <= VMEM_LIMIT
    return pltpu.CompilerParams(dimension_semantics=("arbitrary",) * n_axes,
                                vmem_limit_bytes=vmem_mib * MIB)


def _resident(shape):
    nd = len(shape)
    return pl.BlockSpec(shape, lambda *_: (0,) * nd, pipeline_mode=pl.Buffered(1))


def _cast_specs(casts, grid):
    steps = math.prod(grid)

    def step(*idx):
        lin = idx[0]
        for size, i in zip(grid[1:], idx[1:]):
            lin = lin * size + i
        return lin

    in_specs, out_specs, out_shapes = [], [], []
    for arr, prefix in casts:
        r, c = arr.shape[-2:]
        assert arr.ndim == len(prefix) + 2
        rows = min(d for d in range(16, r + 1, 16) if r % d == 0 and d * steps >= r)
        last = r // rows - 1
        in_specs.append(pl.BlockSpec(
            (None,) * len(prefix) + (rows, c),
            lambda *idx, p=prefix, n=last: p + (jnp.minimum(step(*idx), n), 0)))
        out_specs.append(pl.BlockSpec(
            (rows, c), lambda *idx, n=last: (jnp.minimum(step(*idx), n), 0)))
        out_shapes.append(jax.ShapeDtypeStruct((r, c), BF16))
    return in_specs, out_specs, out_shapes


def _cast_vmem_mib(casts, grid):
    in_specs, _, _ = _cast_specs(casts, grid)
    elems = sum(math.prod(d for d in spec.block_shape if d is not None) for spec in in_specs)
    return -(-elems * (4 + 2) * 2 // MIB)


def _do_casts(src_refs, dst_refs):
    for src, dst in zip(src_refs, dst_refs):
        dst[...] = src[...].astype(BF16)


W_STREAM_STEPS = 16
W_STREAM_SLOTS = 3


def _stream_cast(pick, sem_ref, streams):
    def copy(s, c):
        src, _, stage = streams[s]
        slots, rows = stage.shape[:2]
        return pltpu.make_async_copy(src.at[pick[0], pick[1], pl.ds(c * rows, rows), :],
                                     stage.at[c % slots], sem_ref.at[s, c % slots])

    for s, (_, _, stage) in enumerate(streams):
        for c in range(stage.shape[0] - 1):
            copy(s, c).start()
    for c in range(W_STREAM_STEPS):
        for s, (_, dst, stage) in enumerate(streams):
            slots, rows = stage.shape[:2]
            if c + slots - 1 < W_STREAM_STEPS:
                copy(s, c + slots - 1).start()
            copy(s, c).wait()
            dst[c * rows:(c + 1) * rows, :] = stage[c % slots].astype(BF16)


def _ffn_kernel(x_ref, win_ref, wout_ref, g_ref, b_ref, *rest, n_cast, stream):
    cast_src, o_ref, cast_dst = rest[:n_cast], rest[n_cast], rest[n_cast + 1:2 * n_cast + 1]
    act_ref = rest[2 * n_cast + 1]
    if stream is not None:
        win_bf, wout_bf, stage_in, stage_out, sems = rest[2 * n_cast + 2:]

        @pl.when(pl.program_id(0) == 0)
        def _():
            _stream_cast(stream, sems, [(win_ref, win_bf, stage_in),
                                        (wout_ref, wout_bf, stage_out)])

        win_ref, wout_ref = win_bf, wout_bf
    _do_casts(cast_src, cast_dst)
    x = x_ref[...]
    xb = x.astype(BF16)
    for c in range(D_FF // FF_CHUNK):
        lo = c * FF_CHUNK
        gate = jnp.dot(xb, win_ref[:, lo:lo + FF_CHUNK], preferred_element_type=F32)
        up = jnp.dot(xb, win_ref[:, D_FF + lo:D_FF + lo + FF_CHUNK], preferred_element_type=F32)
        act_ref[:, lo:lo + FF_CHUNK] = (gate * jax.nn.sigmoid(gate) * up).astype(BF16)
    bounds = [sum(FFN_ROW_GROUPS[:r]) for r in range(len(FFN_ROW_GROUPS) + 1)]
    groups = [slice(lo, hi) for lo, hi in zip(bounds[:-1], bounds[1:])]
    ys = [jnp.dot(act_ref[sl, :], wout_ref[...], preferred_element_type=F32) for sl in groups]
    for sl, y in zip(groups, ys):
        o_ref[sl, :] = _layer_norm(ALPHA * x_ref[sl, :] + 0.5 * y, g_ref[...], b_ref[...])


def _ffn_ln(x, w_in, w_out, g, b, casts=(), stream=None):
    t = x.shape[0]
    steps = t // TM
    cast_in, cast_out, cast_shapes = _cast_specs(casts, (steps,))
    scratch = [pltpu.VMEM((TM, D_FF), BF16)]
    if stream is None:
        weights = [_resident((D_MODEL, 2 * D_FF)), _resident((D_FF, D_MODEL))]
        stage_mib = 0
    else:
        weights = [pl.BlockSpec(memory_space=pl.ANY)] * 2
        rows_in, rows_out = D_MODEL // W_STREAM_STEPS, D_FF // W_STREAM_STEPS
        scratch += [pltpu.VMEM((D_MODEL, 2 * D_FF), BF16), pltpu.VMEM((D_FF, D_MODEL), BF16),
                    pltpu.VMEM((W_STREAM_SLOTS, rows_in, 2 * D_FF), F32),
                    pltpu.VMEM((W_STREAM_SLOTS, rows_out, D_MODEL), F32),
                    pltpu.SemaphoreType.DMA((2, W_STREAM_SLOTS))]
        stage_mib = -(-W_STREAM_SLOTS * 4 * (rows_in * 2 * D_FF + rows_out * D_MODEL) // MIB)
    out = pl.pallas_call(
        functools.partial(_ffn_kernel, n_cast=len(casts), stream=stream),
        grid=(steps,),
        in_specs=[pl.BlockSpec((TM, D_MODEL), lambda i: (i, 0))] + weights
                 + [_resident((1, D_MODEL)), _resident((1, D_MODEL))] + cast_in,
        out_specs=[pl.BlockSpec((TM, D_MODEL), lambda i: (i, 0))] + cast_out,
        out_shape=[jax.ShapeDtypeStruct((t, D_MODEL), F32)] + cast_shapes,
        scratch_shapes=scratch,
        compiler_params=_params(1, 45 + stage_mib + _cast_vmem_mib(casts, (steps,))),
        name="ffn_ln",
    )(x, w_in, w_out, g, b, *[arr for arr, _ in casts])
    return out[0], out[1:]


EVEN_OUT_COLS = 2 * DIFF_QK + DIFF_WIDTH + CONV_CH
CONV_TAIL = 8


def _even_proj_kernel(x_ref, w_ref, c_ref, sa_ref, sb_ref, cw_ref, *rest, n_cast, per_seq):
    cast_src, o_ref, cast_dst, tail_ref = (rest[:n_cast], rest[n_cast],
                                           rest[n_cast + 1:2 * n_cast + 1], rest[-1])
    i = pl.program_id(0)

    @pl.when(i == 0)
    def _():
        tail_ref[...] = jnp.zeros_like(tail_ref)

    xb = x_ref[...].astype(BF16)
    c = c_ref[...]
    sa = sa_ref[...]
    sb = sb_ref[...]
    proj = lambda n: jnp.dot(xb, w_ref[:, n * 512:(n + 1) * 512], preferred_element_type=F32)
    for n in range(2):
        r = proj(n)
        if n == 0:
            _do_casts(cast_src, cast_dst)
        scale = DIFF_DH ** -0.5 * LOG2E if n == 0 else 1.0
        for j in range(512 // LANES):
            blk = r[:, j * LANES:(j + 1) * LANES]
            rot = (blk * c + pltpu.roll(blk, LANES - 8, 1) * sa
                   + pltpu.roll(blk, 8, 1) * sb)
            if n == 0:
                rot = rot * scale
            o_ref[:, n * 512 + j * LANES:n * 512 + (j + 1) * LANES] = rot.astype(BF16)
    u = proj(4) * proj(5)
    prev = jnp.where(i % per_seq == 0, jnp.zeros_like(u[:CONV_TAIL]), tail_ref[...])
    tail_ref[...] = u[TM - CONV_TAIL:]
    row = lax.broadcasted_iota(jnp.int32, (CONV_TAIL, CONV_CH), 0)
    u1 = pltpu.roll(u, 1, 0)
    u2 = pltpu.roll(u, 2, 0)
    head1 = jnp.where(row == 0, prev[CONV_TAIL - 1:], u1[:CONV_TAIL])
    head2 = jnp.where(row == 0, prev[CONV_TAIL - 2:CONV_TAIL - 1], u2[:CONV_TAIL])
    head2 = jnp.where(row == 1, prev[CONV_TAIL - 1:], head2)
    u1 = jnp.concatenate([head1, u1[CONV_TAIL:]], 0)
    u2 = jnp.concatenate([head2, u2[CONV_TAIL:]], 0)
    cw = cw_ref[...]
    conv = proj(3) * (cw[0:1] * u2 + cw[1:2] * u1 + cw[2:3] * u)
    o_ref[:, 3 * 512:] = conv.astype(BF16)
    o_ref[:, 2 * 512:3 * 512] = proj(2).astype(BF16)


def _even_proj(x, w, c, sa, sb, conv_w, seq, casts=()):
    t = x.shape[0]
    steps = t // TM
    per_seq = seq // TM
    tbl = pl.BlockSpec((TM, LANES), lambda i: (i % per_seq, 0))
    cast_in, cast_out, cast_shapes = _cast_specs(casts, (steps,))
    out = pl.pallas_call(
        functools.partial(_even_proj_kernel, n_cast=len(casts), per_seq=per_seq),
        grid=(steps,),
        in_specs=[pl.BlockSpec((TM, D_MODEL), lambda i: (i, 0)),
                  _resident((D_MODEL, EVEN_IN)), tbl, tbl, tbl,
                  _resident((3, CONV_CH))] + cast_in,
        out_specs=[pl.BlockSpec((TM, EVEN_OUT_COLS), lambda i: (i, 0))] + cast_out,
        out_shape=[jax.ShapeDtypeStruct((t, EVEN_OUT_COLS), BF16)] + cast_shapes,
        scratch_shapes=[pltpu.VMEM((CONV_TAIL, CONV_CH), F32)],
        compiler_params=_params(1, 28 + _cast_vmem_mib(casts, (steps,))),
        name="even_proj",
    )(x, w, c, sa, sb, conv_w, *[arr for arr, _ in casts])
    return out[0], out[1:]


def _attn_kernel(q_ref, k_ref, v_ref, lv_ref, g_ref, o_ref, s_ref, p_ref, *, lambda_init, seq):
    lv = lv_ref[...]
    lam = (jnp.exp(jnp.sum(lv[0:1] * lv[1:2], -1, keepdims=True))
           - jnp.exp(jnp.sum(lv[2:3] * lv[3:4], -1, keepdims=True)) + lambda_init)
    lane = lax.broadcasted_iota(jnp.int32, (ATT_TQ, LANES), 1)
    row = lax.broadcasted_iota(jnp.int32, (ATT_TQ, ATT_TQ), 0)
    col = lax.broadcasted_iota(jnp.int32, (ATT_TQ, ATT_TQ), 1)
    causal = col <= row
    gain = g_ref[...]
    nt = (((1,), (1,)), ((), ()))
    n_tiles = seq // ATT_TQ
    order = [t for pair in zip(range(n_tiles - 1, -1, -1), range(n_tiles)) for t in pair][:n_tiles]

    def scores(pos):
        lo = order[pos] * ATT_TQ
        hi = lo + ATT_TQ
        q = q_ref[lo:hi, :]
        zero = jnp.zeros_like(q)
        qs = (jnp.where(lane < DIFF_DH, q, zero), jnp.where(lane >= DIFF_DH, q, zero))
        maxima = []
        for c, qc in enumerate(qs):
            s = lax.dot_general(qc, k_ref[:hi, :], nt, preferred_element_type=F32)
            sc_ref = s_ref.at[pos % 2, c]
            if lo > 0:
                sc_ref[:, :lo] = s[:, :lo]
            sc_ref[:, lo:hi] = jnp.where(causal, s[:, lo:], NEG_INF)
            maxima.append(jnp.max(sc_ref[:, :hi], -1, keepdims=True))
        return maxima

    maxima = scores(0)
    for pos, i in enumerate(order):
        buf = pos % 2
        lo = i * ATT_TQ
        hi = lo + ATT_TQ
        m = maxima
        if pos + 1 < n_tiles:
            maxima = scores(pos + 1)
        l = []
        for c in range(2):
            lsum = []
            for r in range(ATT_TQ // ATT_STRIP):
                sl = slice(r * ATT_STRIP, (r + 1) * ATT_STRIP)
                p = jnp.exp2(s_ref[buf, c, sl, :hi] - m[c][sl])
                lsum.append(jnp.sum(p, -1, keepdims=True))
                base = c * ATT_TQ + r * ATT_STRIP
                p_ref[buf, base:base + ATT_STRIP, :hi] = p.astype(BF16)
            l.append(jnp.concatenate(lsum, 0))
        acc = jnp.dot(p_ref[buf, :, :hi], v_ref[:hi, :], preferred_element_type=F32)
        acc = acc[:ATT_TQ] * (1.0 / l[0]) - acc[ATT_TQ:] * (lam / l[1])
        y = acc * lax.rsqrt(jnp.mean(acc * acc, -1, keepdims=True) + LN_EPS)
        o_ref[lo:hi, :] = (y * gain * (1.0 - lambda_init)).astype(BF16)


def _diff_attention(hcat, lam_vecs, norm_g, batch, seq, lambda_init):
    return pl.pallas_call(
        functools.partial(_attn_kernel, lambda_init=lambda_init, seq=seq),
        grid=(batch, DIFF_HEADS),
        in_specs=[pl.BlockSpec((seq, LANES), lambda b, h: (b, h)),
                  pl.BlockSpec((seq, LANES), lambda b, h: (b, DIFF_QK // LANES + h)),
                  pl.BlockSpec((seq, LANES), lambda b, h: (b, 2 * DIFF_QK // LANES + h)),
                  _resident((4, DIFF_DH)),
                  _resident((1, DIFF_DV))],
        out_specs=pl.BlockSpec((seq, DIFF_DV), lambda b, h: (b, h)),
        out_shape=jax.ShapeDtypeStruct((batch * seq, DIFF_WIDTH), BF16),
        scratch_shapes=[pltpu.VMEM((2, 2, ATT_TQ, seq), F32),
                        pltpu.VMEM((2, 2 * ATT_TQ, seq), BF16)],
        compiler_params=_params(2, 24),
        name="diff_attn",
    )(hcat, hcat, hcat, lam_vecs, norm_g)


def _even_out_kernel(x_ref, attn_ref, conv_ref, wout_ref, g_ref, b_ref, o_ref):
    rows = TM // OUT_SPLIT
    groups = [slice(r * rows, (r + 1) * rows) for r in range(OUT_SPLIT)]
    ys = []
    for sl in groups:
        mix = jnp.concatenate([attn_ref[sl, :], conv_ref[sl, :]], 1)
        ys.append(jnp.dot(mix, wout_ref[...], preferred_element_type=F32))
    for sl, y in zip(groups, ys):
        o_ref[sl, :] = _layer_norm(ALPHA * x_ref[sl, :] + y, g_ref[...], b_ref[...])


def _even_out(x, attn, hcat, w_out, g, b):
    t = x.shape[0]
    tm = TM
    conv_col = (2 * DIFF_QK + DIFF_WIDTH) // CONV_CH
    return pl.pallas_call(
        _even_out_kernel,
        grid=(t // tm,),
        in_specs=[pl.BlockSpec((tm, D_MODEL), lambda i: (i, 0)),
                  pl.BlockSpec((tm, DIFF_WIDTH), lambda i: (i, 0)),
                  pl.BlockSpec((tm, CONV_CH), lambda i: (i, conv_col)),
                  _resident((DIFF_WIDTH + CONV_CH, D_MODEL)),
                  _resident((1, D_MODEL)), _resident((1, D_MODEL))],
        out_specs=pl.BlockSpec((tm, D_MODEL), lambda i: (i, 0)),
        out_shape=jax.ShapeDtypeStruct((t, D_MODEL), F32),
        compiler_params=_params(1, 24),
        name="even_out",
    )(x, attn, hcat, w_out, g, b)


def _odd_mix_kernel(cd_ref, x_ref, w_ref, cos_ref, sin_ref, ng_ref, dmat_ref, qd_ref, kd_ref,
                    wout_ref, g_ref, b_ref, o_ref, state_ref, y_ref):
    @pl.when(pl.program_id(1) == 0)
    def _():
        state_ref[...] = jnp.zeros_like(state_ref)

    xb = x_ref[...].astype(BF16)
    cos = cos_ref[...]
    sin = sin_ref[...]
    half = RET_DK // 2
    k0, v0, g0 = RET_QK, 2 * RET_QK, 2 * RET_QK + RET_VW

    def proj(lo, n):
        return jnp.dot(xb, w_ref[:, lo:lo + n], preferred_element_type=F32)

    def rotary(r):
        x1, x2 = r[:, :half], r[:, half:]
        return jnp.concatenate([x1 * cos - x2 * sin, x2 * cos + x1 * sin], 1)

    def gate_of(lo, n):
        g = proj(g0 + lo, n)
        return g * jax.nn.sigmoid(g) * ng_ref[:, lo:lo + n]

    def projection(j):
        hv = RET_DV // 2
        return [lambda: rotary(proj(j * RET_DK, RET_DK)).astype(BF16),
                lambda: rotary(proj(k0 + j * RET_DK, RET_DK)) * RET_DK ** -0.5,
                lambda: proj(v0 + j * RET_DV, hv).astype(BF16),
                lambda: proj(v0 + j * RET_DV + hv, hv).astype(BF16),
                lambda: gate_of(j * RET_DV, hv),
                lambda: gate_of(j * RET_DV + hv, hv)]

    def assemble(pieces):
        q, k, v_lo, v_hi, gate_lo, gate_hi = pieces
        return q, k, jnp.concatenate([v_lo, v_hi], 1), jnp.concatenate([gate_lo, gate_hi], 1)

    n_blocks = ODD_TM // RET_BLOCK
    ahead = [piece() for piece in projection(0)]
    for j in range(RET_HEADS):
        q, k, v, gate = assemble(ahead)
        kb = k.astype(BF16)
        pending = projection(j + 1) if j + 1 < RET_HEADS else []
        ahead = []
        cd = cd_ref[j]
        dmat = dmat_ref[j]
        qd = jnp.concatenate([qd_ref[j]] * (RET_DV // LANES), axis=1)
        kd = jnp.concatenate([kd_ref[j]] * (RET_DK // LANES), axis=1)
        state = state_ref[j]
        for c in range(n_blocks):
            share = range(c * len(pending) // n_blocks, (c + 1) * len(pending) // n_blocks)
            ahead += [pending[i]() for i in share]
            rows = slice(c * RET_BLOCK, (c + 1) * RET_BLOCK)
            qc, kc, vc = q[rows], kb[rows], v[rows]
            sc = lax.dot_general(qc, kc, (((1,), (1,)), ((), ())),
                                 preferred_element_type=F32) * dmat
            y = jnp.dot(sc.astype(BF16), vc, preferred_element_type=F32)
            y = y + jnp.dot(qc, state.astype(BF16), preferred_element_type=F32) * qd
            kdec = (k[rows] * kd).astype(BF16)
            state = state * cd + lax.dot_general(kdec, vc, (((0,), (0,)), ((), ())),
                                                 preferred_element_type=F32)
            mu = jnp.mean(y, -1, keepdims=True)
            d = y - mu
            var = jnp.mean(d * d, -1, keepdims=True)
            yn = d * lax.rsqrt(var + LN_EPS)
            y_ref[rows, j * RET_DV:(j + 1) * RET_DV] = (gate[rows] * yn).astype(BF16)
        state_ref[j] = state

    rows = ODD_TM // OUT_SPLIT
    groups = [slice(r * rows, (r + 1) * rows) for r in range(OUT_SPLIT)]
    ys = [jnp.dot(y_ref[sl, :], wout_ref[...], preferred_element_type=F32) for sl in groups]
    for sl, y in zip(groups, ys):
        o_ref[sl, :] = _layer_norm(ALPHA * x_ref[sl, :] + y, g_ref[...], b_ref[...])


def _odd_layer_mix(x, w, cos, sin, norm_g, tables, w_out, g, b, batch, seq):
    cd, dmat, qd, kd = tables
    per_seq = seq // ODD_TM
    tile = lambda b, p: (b * per_seq + p, 0)
    tbl = pl.BlockSpec((ODD_TM, LANES), lambda b, p: (p, 0))
    return pl.pallas_call(
        _odd_mix_kernel,
        grid=(batch, per_seq),
        in_specs=[pl.BlockSpec(memory_space=pltpu.SMEM),
                  pl.BlockSpec((ODD_TM, D_MODEL), tile),
                  _resident((D_MODEL, ODD_IN)), tbl, tbl, _resident((1, RET_VW)),
                  _resident((RET_HEADS, RET_BLOCK, RET_BLOCK)),
                  _resident((RET_HEADS, RET_BLOCK, LANES)),
                  _resident((RET_HEADS, RET_BLOCK, LANES)),
                  _resident((RET_VW, D_MODEL)),
                  _resident((1, D_MODEL)), _resident((1, D_MODEL))],
        out_specs=pl.BlockSpec((ODD_TM, D_MODEL), tile),
        out_shape=jax.ShapeDtypeStruct((batch * seq, D_MODEL), F32),
        scratch_shapes=[pltpu.VMEM((RET_HEADS, RET_DK, RET_DV), F32),
                        pltpu.VMEM((ODD_TM, RET_VW), BF16)],
        compiler_params=_params(2, 56),
        name="odd_mix",
    )(cd, x, w, cos, sin, norm_g, dmat, qd, kd, w_out, g, b)


def _rope_angles(seq, n_rot, theta):
    inv = np.exp(-math.log(theta) * np.arange(n_rot // 2, dtype=np.float64) * (2.0 / n_rot))
    return np.arange(seq, dtype=np.float64)[:, None] * inv[None, :]


def _even_rope_tables(seq):
    half = ROPE_DIMS // 2
    ang = _rope_angles(seq, ROPE_DIMS, ROPE_THETA)
    cos, sin = np.cos(ang), np.sin(ang)
    pad = np.zeros((seq, DIFF_DH - ROPE_DIMS))
    zeros = np.zeros((seq, half))
    c = np.concatenate([cos, cos, pad + 1.0], -1)
    sa = np.concatenate([-sin, zeros, pad], -1)
    sb = np.concatenate([zeros, sin, pad], -1)
    reps = LANES // DIFF_DH
    return tuple(jnp.asarray(np.tile(t, (1, reps)), F32) for t in (c, sa, sb))


def _odd_rope_tables(seq):
    ang = _rope_angles(seq, RET_DK, RET_ROPE_THETA)
    return jnp.asarray(np.cos(ang), F32), jnp.asarray(np.sin(ang), F32)


def _retention_tables():
    log_g = np.log1p(-np.exp2(-5.0 - np.arange(RET_HEADS, dtype=np.float64)))
    idx = np.arange(RET_BLOCK, dtype=np.float64)
    rel = idx[:, None] - idx[None, :]
    dmat = np.where(rel >= 0, np.exp(log_g[:, None, None] * np.maximum(rel, 0.0)), 0.0)
    q_decay = np.exp(log_g[:, None] * (idx + 1.0))
    k_decay = np.exp(log_g[:, None] * (RET_BLOCK - 1.0 - idx))
    cd = np.exp(log_g * RET_BLOCK)
    rep = lambda t: np.broadcast_to(t[:, :, None], (RET_HEADS, RET_BLOCK, LANES))
    return tuple(jnp.asarray(t, F32) for t in (cd, dmat, rep(q_decay), rep(k_decay)))


def kernel(x, ln_g, ln_b, ffn_w_in, ffn_w_out, even_w_in, even_w_out, diff_lambda,
           diff_norm_g, conv_w, odd_w_in, odd_w_out, ret_norm_g):
    batch, seq, _ = x.shape
    assert seq % TM == 0 and seq % ODD_TM == 0 and seq % ATT_TQ == 0 and DEPTH == 2
    h = x.reshape(batch * seq, D_MODEL)
    row = lambda v: v.reshape(1, -1)
    norm = lambda i, k: (row(ln_g[i, k]), row(ln_b[i, k]))
    ffn = lambda i, k: [(ffn_w_in, (i, k)), (ffn_w_out, (i, k))]


    h, (e_in, e_out) = _ffn_ln(h, ffn_w_in, ffn_w_out, *norm(0, 0), stream=(0, 0),
                               casts=[(even_w_in, (0,)), (even_w_out, (0,))])
    lambda_init = 0.8 - 0.6 * math.exp(-0.3 * 0)
    hcat, (w_in, w_out, *next_ffn) = _even_proj(
        h, e_in, *_even_rope_tables(seq), conv_w[0], seq, casts=ffn(0, 1) + ffn(1, 0))
    attn = _diff_attention(hcat, diff_lambda[0], row(diff_norm_g[0]), batch, seq, lambda_init)
    h = _even_out(h, attn, hcat, e_out, *norm(0, 1))
    h, (o_in, o_out) = _ffn_ln(h, w_in, w_out, *norm(0, 2),
                               casts=[(odd_w_in, (0,)), (odd_w_out, (0,))])

    h, last_ffn = _ffn_ln(h, *next_ffn, *norm(1, 0), casts=ffn(1, 1))
    h = _odd_layer_mix(h, o_in, *_odd_rope_tables(seq), row(ret_norm_g[0]),
                       _retention_tables(), o_out, *norm(1, 1), batch, seq)
    h, _ = _ffn_ln(h, *last_ffn, *norm(1, 2))
    return h.reshape(batch, seq, D_MODEL)
```
